```python
import math
import jax, jax.numpy as jnp
from jax import lax
import numpy as np

D_MODEL = 1024
BATCH = 8
SEQ = 2048
DEPTH = 2

N_A = DEPTH // 2
N_B = DEPTH - N_A
N_DENSE = (DEPTH + 1) // 2
N_MOE = DEPTH // 2

CONV_WIDTH = 3
HEAD_DIM = 64
N_HEADS = D_MODEL // (2 * HEAD_DIM)
V_DIM = 2 * HEAD_DIM
D_FF = 256 * ((8 * D_MODEL // 3 + 255) // 256)
N_EXPERTS = 8
TOP_K = 2
D_FF_EXPERT = 7 * D_MODEL // 2
Q_BLOCK = 128
EPS = 1e-6

kernel_name = "yoco_shortconv_diffattn_moe"


def rmsnorm(x, g):
    xf = x.astype(jnp.float32)
    y = xf * lax.rsqrt(jnp.mean(xf * xf, axis=-1, keepdims=True) + EPS)
    return (y * g.astype(jnp.float32)).astype(x.dtype)


def lambda_init(layer_idx_1based):
    return 0.8 - 0.6 * math.exp(-0.3 * (layer_idx_1based - 1))


def swiglu(h, w_gu, w_down):
    gu = h @ w_gu
    g, u = jnp.split(gu, 2, axis=-1)
    return (jax.nn.silu(g) * u) @ w_down


def short_conv_mixer(h, w_in, conv_w, w_out):
    S = h.shape[1]
    bcv = h @ w_in
    b, c, v = jnp.split(bcv, 3, axis=-1)
    u = c * v
    up = jnp.pad(u, ((0, 0), (CONV_WIDTH - 1, 0), (0, 0)))
    z = up[:, 0:S] * conv_w[0]
    for j in range(1, CONV_WIDTH):
        z = z + up[:, j:j + S] * conv_w[j]
    return (b * z) @ w_out


def shared_kv(stream, ln_kv, w_kv, k_norm):
    Bsz, S, _ = stream.shape
    h = rmsnorm(stream, ln_kv)
    kv = h @ w_kv
    k = kv[..., :2 * N_HEADS * HEAD_DIM].reshape(Bsz, S, N_HEADS, 2, HEAD_DIM)
    k = rmsnorm(k, k_norm)
    v = kv[..., 2 * N_HEADS * HEAD_DIM:].reshape(Bsz, S, N_HEADS, V_DIM)
    return k, v


def diff_attention(h, k, v, w_q, q_norm, lam_params, sub_norm, w_o, lam_init):
    Bsz, S, _ = h.shape
    scale = HEAD_DIM ** -0.5
    q = (h @ w_q).reshape(Bsz, S, N_HEADS, 2, HEAD_DIM)
    q = rmsnorm(q, q_norm) * scale
    lp = lam_params.astype(jnp.float32)
    lam = (jnp.exp(jnp.sum(lp[0] * lp[1])) - jnp.exp(jnp.sum(lp[2] * lp[3])) + lam_init)
    n_blk = S // Q_BLOCK
    qb = q.reshape(Bsz, n_blk, Q_BLOCK, N_HEADS, 2, HEAD_DIM).transpose(1, 0, 2, 3, 4, 5)
    kpos = jnp.arange(S)

    def block(args):
        qi, bi = args
        s = jnp.einsum('bqhcd,bkhcd->bhcqk', qi, k).astype(jnp.float32)
        qpos = bi * Q_BLOCK + jnp.arange(Q_BLOCK)
        mask = kpos[None, :] <= qpos[:, None]
        s = jnp.where(mask, s, -jnp.inf)
        p = jax.nn.softmax(s, axis=-1)
        a = p[:, :, 0] - lam * p[:, :, 1]
        return jnp.einsum('bhqk,bkhe->bqhe', a.astype(v.dtype), v)

    o = lax.map(block, (qb, jnp.arange(n_blk)))
    o = o.transpose(1, 0, 2, 3, 4).reshape(Bsz, S, N_HEADS, V_DIM)
    o = rmsnorm(o, sub_norm) * (1.0 - lam_init)
    return o.reshape(Bsz, S, N_HEADS * V_DIM) @ w_o


def moe_swiglu(h, w_router, w_gu_e, w_down_e):
    logits = (h @ w_router).astype(jnp.float32)
    top_v, top_i = lax.top_k(logits, TOP_K)
    gates = jax.nn.softmax(top_v, axis=-1)
    combine = jnp.sum(jax.nn.one_hot(top_i, N_EXPERTS, dtype=jnp.float32) * gates[..., None], axis=-2)
    combine = combine.astype(h.dtype)
    out = jnp.zeros_like(h)
    for e in range(N_EXPERTS):
        out = out + combine[..., e:e + 1] * swiglu(h, w_gu_e[e], w_down_e[e])
    return out


def setup_inputs(seed: int = 0) -> dict:
    key = jax.random.key(seed)
    ks = jax.random.split(key, 24)
    D = D_MODEL
    f32 = jnp.float32

    def nrm(k, shape, scale):
        return jax.random.normal(k, shape, f32) * scale

    def gain(k, shape):
        return 1.0 + 0.02 * jax.random.normal(k, shape, f32)

    kv_width = 2 * N_HEADS * HEAD_DIM + N_HEADS * V_DIM
    return {
        "x": jax.random.normal(ks[0], (BATCH, SEQ, D), f32),
        "ln_mix": gain(ks[1], (DEPTH, D)),
        "ln_ffn": gain(ks[2], (DEPTH, D)),
        "conv_w_in": nrm(ks[3], (N_A, D, 3 * D), D ** -0.5),
        "conv_w": nrm(ks[4], (N_A, CONV_WIDTH, D), CONV_WIDTH ** -0.5),
        "conv_w_out": nrm(ks[5], (N_A, D, D), D ** -0.5),
        "ln_kv": gain(ks[6], (D,)),
        "w_kv": nrm(ks[7], (D, kv_width), D ** -0.5),
        "k_norm": gain(ks[8], (HEAD_DIM,)),
        "attn_w_q": nrm(ks[9], (N_B, D, 2 * N_HEADS * HEAD_DIM), D ** -0.5),
        "q_norm": gain(ks[10], (N_B, HEAD_DIM)),
        "lam_params": nrm(ks[11], (N_B, 4, HEAD_DIM), 0.1),
        "sub_norm": gain(ks[12], (N_B, V_DIM)),
        "attn_w_o": nrm(ks[13], (N_B, N_HEADS * V_DIM, D), (N_HEADS * V_DIM) ** -0.5),
        "ffn_w_gu": nrm(ks[14], (N_DENSE, D, 2 * D_FF), D ** -0.5),
        "ffn_w_down": nrm(ks[15], (N_DENSE, D_FF, D), D_FF ** -0.5),
        "router_w": nrm(ks[16], (N_MOE, D, N_EXPERTS), D ** -0.5),
        "moe_w_gu": nrm(ks[17], (N_MOE, N_EXPERTS, D, 2 * D_FF_EXPERT), D ** -0.5),
        "moe_w_down": nrm(ks[18], (N_MOE, N_EXPERTS, D_FF_EXPERT, D), D_FF_EXPERT ** -0.5),
    }


def reference(x, ln_mix, ln_ffn, conv_w_in, conv_w, conv_w_out, ln_kv, w_kv, k_norm,
              attn_w_q, q_norm, lam_params, sub_norm, attn_w_o, ffn_w_gu, ffn_w_down,
              router_w, moe_w_gu, moe_w_down):
    k_sh = None
    v_sh = None
    for i in range(DEPTH):
        h = rmsnorm(x, ln_mix[i])
        if i < N_A:
            x = x + short_conv_mixer(h, conv_w_in[i], conv_w[i], conv_w_out[i])
        else:
            j = i - N_A
            x = x + diff_attention(h, k_sh, v_sh, attn_w_q[j], q_norm[j], lam_params[j],
                                   sub_norm[j], attn_w_o[j], lambda_init(i + 1))
        h = rmsnorm(x, ln_ffn[i])
        if i % 2 == 0:
            x = x + swiglu(h, ffn_w_gu[i // 2], ffn_w_down[i // 2])
        else:
            x = x + moe_swiglu(h, router_w[i // 2], moe_w_gu[i // 2], moe_w_down[i // 2])
        if i == N_A - 1:
            k_sh, v_sh = shared_kv(x, ln_kv, w_kv, k_norm)
    return x
```

```python
import functools
import math

import jax
import jax.numpy as jnp
from jax import lax
from jax.experimental import pallas as pl
from jax.experimental.pallas import tpu as pltpu

F32 = jnp.float32
BF16 = jnp.bfloat16

EPS = 1e-6
HEAD_DIM = 64
V_DIM = 2 * HEAD_DIM
CONV_WIDTH = 3
TOP_K = 2

LANES = 128
SUBLANES = 8
VMEM_LIMIT = 56 * 1024 * 1024

ROW_TILE = 512
ATTN_TILE = 256
GROUP_TILE = 512
MOE_F_TILE = 1792
DISPATCH_BLOCK = 1024
DMA_UNROLL = 16


def _rms_scale(x):
    return x * lax.rsqrt(jnp.mean(x * x, axis=-1, keepdims=True) + EPS)


def _dot(a, b):
    return jnp.dot(a, b, preferred_element_type=F32)


def _params(*sem):
    return pltpu.CompilerParams(dimension_semantics=sem, vmem_limit_bytes=VMEM_LIMIT)


def _mixer_a_kernel(x_ref, g_ref, win_ref, cw_ref, wout_ref, o_ref, ubuf, *, tm, d, tiles_per_seq):
    i = pl.program_id(0)
    x = x_ref[...]
    h = (_rms_scale(x) * g_ref[...]).astype(BF16)
    c = _dot(h, win_ref[:, d:2 * d])
    v = _dot(h, win_ref[:, 2 * d:])
    u = c * v

    @pl.when(i % tiles_per_seq == 0)
    def _():
        ubuf[0:SUBLANES, :] = jnp.zeros((SUBLANES, d), F32)

    ubuf[SUBLANES:tm + SUBLANES, :] = u
    u1 = ubuf[SUBLANES - 1:tm + SUBLANES - 1, :]
    u2 = ubuf[SUBLANES - 2:tm + SUBLANES - 2, :]
    cw = cw_ref[...]
    z = u2 * cw[0:1] + u1 * cw[1:2] + u * cw[2:3]
    b = _dot(h, win_ref[:, 0:d])
    y = (b * z).astype(BF16)
    o_ref[...] = x + _dot(y, wout_ref[...])
    ubuf[0:SUBLANES, :] = ubuf[tm:tm + SUBLANES, :]


def _mixer_a(x2d, g, w_in, conv_w, w_out, seq):
    t, d = x2d.shape
    tm = ROW_TILE
    kern = functools.partial(_mixer_a_kernel, tm=tm, d=d, tiles_per_seq=seq // tm)
    return pl.pallas_call(
        kern,
        grid=(t // tm,),
        in_specs=[
            pl.BlockSpec((tm, d), lambda i: (i, 0)),
            pl.BlockSpec((1, d), lambda i: (0, 0)),
            pl.BlockSpec((d, 3 * d), lambda i: (0, 0)),
            pl.BlockSpec((CONV_WIDTH, d), lambda i: (0, 0)),
            pl.BlockSpec((d, d), lambda i: (0, 0)),
        ],
        out_specs=pl.BlockSpec((tm, d), lambda i: (i, 0)),
        out_shape=jax.ShapeDtypeStruct((t, d), F32),
        scratch_shapes=[pltpu.VMEM((tm + SUBLANES, d), F32)],
        compiler_params=_params("arbitrary"),
        name="mixer_a",
    )(x2d, g, w_in, conv_w, w_out)


def _ffn_chunks(f):
    step = 1024
    return [(lo, min(lo + step, f)) for lo in range(0, f, step)]


def _ffn_kernel(x_ref, g_ref, wgu_ref, wd_ref, o_ref, *, f):
    x = x_ref[...]
    h = (_rms_scale(x) * g_ref[...]).astype(BF16)
    acc = x
    for lo, hi in _ffn_chunks(f):
        gate = _dot(h, wgu_ref[:, lo:hi])
        up = _dot(h, wgu_ref[:, f + lo:f + hi])
        a = (gate * jax.nn.sigmoid(gate) * up).astype(BF16)
        acc = acc + _dot(a, wd_ref[lo:hi, :])
    o_ref[...] = acc


def _ffn_dense(x2d, g, w_gu, w_down):
    t, d = x2d.shape
    f = w_down.shape[0]
    tm = ROW_TILE
    return pl.pallas_call(
        functools.partial(_ffn_kernel, f=f),
        grid=(t // tm,),
        in_specs=[
            pl.BlockSpec((tm, d), lambda i: (i, 0)),
            pl.BlockSpec((1, d), lambda i: (0, 0)),
            pl.BlockSpec((d, 2 * f), lambda i: (0, 0)),
            pl.BlockSpec((f, d), lambda i: (0, 0)),
        ],
        out_specs=pl.BlockSpec((tm, d), lambda i: (i, 0)),
        out_shape=jax.ShapeDtypeStruct((t, d), F32),
        compiler_params=_params("parallel"),
        name="ffn_dense",
    )(x2d, g, w_gu, w_down)


def _qkv_kernel(x_ref, gq_ref, gkv_ref, wq_ref, wkt_ref, wv_ref, kn_ref, q_ref, kt_ref, v_ref, *, tm, d):
    y = _rms_scale(x_ref[...])
    hq = (y * gq_ref[...]).astype(BF16)
    hkv = (y * gkv_ref[...]).astype(BF16)
    q_ref[...] = _dot(hq, wq_ref[...]).astype(BF16)
    v_ref[...] = _dot(hkv, wv_ref[...]).astype(BF16)
    kt = lax.dot_general(wkt_ref[...], hkv, (((1,), (1,)), ((), ())), preferred_element_type=F32)
    k3 = kt.reshape(d // HEAD_DIM, HEAD_DIM, tm)
    k3 = k3 * lax.rsqrt(jnp.mean(k3 * k3, axis=1, keepdims=True) + EPS) * kn_ref[...][None]
    kt_ref[...] = k3.reshape(d, tm).astype(BF16)


def _qkv(x2d, g_q, g_kv, w_q, w_kt, w_v, k_norm_col):
    t, d = x2d.shape
    tm = ROW_TILE
    const = lambda i: (0, 0)
    return pl.pallas_call(
        functools.partial(_qkv_kernel, tm=tm, d=d),
        grid=(t // tm,),
        in_specs=[
            pl.BlockSpec((tm, d), lambda i: (i, 0)),
            pl.BlockSpec((1, d), const),
            pl.BlockSpec((1, d), const),
            pl.BlockSpec((d, d), const),
            pl.BlockSpec((d, d), const),
            pl.BlockSpec((d, d), const),
            pl.BlockSpec((HEAD_DIM, 1), const),
        ],
        out_specs=[
            pl.BlockSpec((tm, d), lambda i: (i, 0)),
            pl.BlockSpec((d, tm), lambda i: (0, i)),
            pl.BlockSpec((tm, d), lambda i: (i, 0)),
        ],
        out_shape=[
            jax.ShapeDtypeStruct((t, d), BF16),
            jax.ShapeDtypeStruct((d, t), BF16),
            jax.ShapeDtypeStruct((t, d), BF16),
        ],
        compiler_params=_params("parallel"),
        name="qkv_proj",
    )(x2d, g_q, g_kv, w_q, w_kt, w_v, k_norm_col)


def _attn_kernel(q_ref, kt_ref, v_ref, qn_ref, lam_ref, sn_ref, o_ref, *, seq, tq, lam_init):
    lp = lam_ref[...]
    lam = (jnp.exp(jnp.sum(lp[0:1] * lp[1:2], axis=-1, keepdims=True))
           - jnp.exp(jnp.sum(lp[2:3] * lp[3:4], axis=-1, keepdims=True)) + lam_init)
    qg = qn_ref[...] * (HEAD_DIM ** -0.5)
    row = lax.broadcasted_iota(jnp.int32, (tq, tq), 0)
    col = lax.broadcasted_iota(jnp.int32, (tq, tq), 1)
    causal = col <= row

    def q_body(qi, carry):
        q0 = pl.multiple_of(qi * tq, tq)
        q = q_ref[pl.ds(q0, tq), :].astype(F32)
        qs = [(_rms_scale(q[:, c * HEAD_DIM:(c + 1) * HEAD_DIM]) * qg).astype(BF16) for c in range(2)]

        def tile(j, state, masked):
            k0 = pl.multiple_of(j * tq, tq)
            kt = kt_ref[:, pl.ds(k0, tq)]
            vv = v_ref[pl.ds(k0, tq), :]
            out = []
            for c in range(2):
                m, l, acc = state[c]
                s = _dot(qs[c], kt[c * HEAD_DIM:(c + 1) * HEAD_DIM, :])
                if masked:
                    s = jnp.where(causal, s, -jnp.inf)
                m_new = jnp.maximum(m, jnp.max(s, axis=-1, keepdims=True))
                alpha = jnp.exp(m - m_new)
                p = jnp.exp(s - m_new)
                l = alpha * l + jnp.sum(p, axis=-1, keepdims=True)
                acc = alpha * acc + _dot(p.astype(BF16), vv)
                out.append((m_new, l, acc))
            return tuple(out)

        one = (jnp.full((tq, 1), -jnp.inf, F32), jnp.zeros((tq, 1), F32), jnp.zeros((tq, V_DIM), F32))
        state = lax.fori_loop(0, qi, lambda j, s: tile(j, s, False), (one, one))
        (_, l1, a1), (_, l2, a2) = tile(qi, state, True)
        o = a1 / l1 - lam * (a2 / l2)
        o = _rms_scale(o) * sn_ref[...] * (1.0 - lam_init)
        o_ref[pl.ds(q0, tq), :] = o.astype(BF16)
        return carry

    lax.fori_loop(0, seq // tq, q_body, 0)


def _attention(q, kt, v, q_norm, lam_params, sub_norm, batch, seq, lam_init):
    t, d = q.shape
    n_heads = d // V_DIM
    kern = functools.partial(_attn_kernel, seq=seq, tq=ATTN_TILE, lam_init=lam_init)
    return pl.pallas_call(
        kern,
        grid=(batch, n_heads),
        in_specs=[
            pl.BlockSpec((seq, V_DIM), lambda b, h: (b, h)),
            pl.BlockSpec((V_DIM, seq), lambda b, h: (h, b)),
            pl.BlockSpec((seq, V_DIM), lambda b, h: (b, h)),
            pl.BlockSpec((1, HEAD_DIM), lambda b, h: (0, 0)),
            pl.BlockSpec((4, HEAD_DIM), lambda b, h: (0, 0)),
            pl.BlockSpec((1, V_DIM), lambda b, h: (0, 0)),
        ],
        out_specs=pl.BlockSpec((seq, V_DIM), lambda b, h: (b, h)),
        out_shape=jax.ShapeDtypeStruct((t, d), BF16),
        compiler_params=_params("parallel", "parallel"),
        name="diff_attn",
    )(q, kt, v, q_norm, lam_params, sub_norm)


def _oproj_router_kernel(x_ref, o_ref, wo_ref, g_ref, rw_ref,
                         x3_ref, h8_ref, topi_ref, gate_ref, rank_ref, cnt_ref, carry,
                         *, tm, d, n_experts):
    i = pl.program_id(0)

    @pl.when(i == 0)
    def _():
        carry[...] = jnp.zeros_like(carry)

    x3 = x_ref[...] + _dot(o_ref[...], wo_ref[...])
    x3_ref[...] = x3
    h = _rms_scale(x3) * g_ref[...]
    for c in range(d // LANES):
        h8_ref[pl.ds(c, tm, stride=d // LANES), :] = h[:, c * LANES:(c + 1) * LANES]

    rw = rw_ref[...]
    h_hi = h.astype(BF16)
    h_lo = (h - h_hi.astype(F32)).astype(BF16)
    w_hi = rw.astype(BF16)
    w_lo = (rw - w_hi.astype(F32)).astype(BF16)
    logits = _dot(h_hi, w_hi) + (_dot(h_hi, w_lo) + _dot(h_lo, w_hi))

    lane = lax.broadcasted_iota(jnp.int32, (tm, LANES), 1)
    neg = -jnp.inf
    l1 = jnp.where(lane < n_experts, logits, neg)
    m1 = jnp.max(l1, axis=-1, keepdims=True)
    i1 = jnp.min(jnp.where(l1 == m1, lane, LANES), axis=-1, keepdims=True)
    l2 = jnp.where(lane == i1, neg, l1)
    m2 = jnp.max(l2, axis=-1, keepdims=True)
    i2 = jnp.min(jnp.where(l2 == m2, lane, LANES), axis=-1, keepdims=True)
    e2 = jnp.exp(m2 - m1)
    den = 1.0 + e2
    first = lax.broadcasted_iota(jnp.int32, (tm, TOP_K), 1) == 0
    topi_ref[...] = jnp.where(first, i1, i2)
    gate_ref[...] = jnp.where(first, 1.0 / den, e2 / den)

    oh1 = lane == i1
    oh2 = lane == i2
    cnt = jnp.where(oh1, 1.0, 0.0) + jnp.where(oh2, 1.0, 0.0)
    r = lax.broadcasted_iota(jnp.int32, (tm, tm), 0)
    cidx = lax.broadcasted_iota(jnp.int32, (tm, tm), 1)
    lower = jnp.where(cidx < r, 1.0, 0.0).astype(BF16)
    before = _dot(lower, cnt.astype(BF16)) + carry[...]
    r1 = jnp.sum(jnp.where(oh1, before, 0.0), axis=-1, keepdims=True)
    r2 = jnp.sum(jnp.where(oh2, before, 0.0), axis=-1, keepdims=True)
    rank_ref[...] = jnp.where(first, r1, r2).astype(jnp.int32)
    total = carry[...] + jnp.sum(cnt, axis=0, keepdims=True)
    carry[...] = total
    cnt_ref[...] = total.astype(jnp.int32)


def _oproj_router(x2d, o, w_o, g, rw_pad, n_experts):
    t, d = x2d.shape
    tm = ROW_TILE
    const = lambda i: (0, 0)
    kern = functools.partial(_oproj_router_kernel, tm=tm, d=d, n_experts=n_experts)
    return pl.pallas_call(
        kern,
        grid=(t // tm,),
        in_specs=[
            pl.BlockSpec((tm, d), lambda i: (i, 0)),
            pl.BlockSpec((tm, d), lambda i: (i, 0)),
            pl.BlockSpec((d, d), const),
            pl.BlockSpec((1, d), const),
            pl.BlockSpec((d, LANES), const),
        ],
        out_specs=[
            pl.BlockSpec((tm, d), lambda i: (i, 0)),
            pl.BlockSpec((tm * (d // LANES), LANES), lambda i: (i, 0)),
            pl.BlockSpec((tm, TOP_K), lambda i: (i, 0)),
            pl.BlockSpec((tm, TOP_K), lambda i: (i, 0)),
            pl.BlockSpec((tm, TOP_K), lambda i: (i, 0)),
            pl.BlockSpec((1, LANES), const),
        ],
        out_shape=[
            jax.ShapeDtypeStruct((t, d), F32),
            jax.ShapeDtypeStruct((t * (d // LANES), LANES), F32),
            jax.ShapeDtypeStruct((t, TOP_K), jnp.int32),
            jax.ShapeDtypeStruct((t, TOP_K), F32),
            jax.ShapeDtypeStruct((t, TOP_K), jnp.int32),
            jax.ShapeDtypeStruct((1, LANES), jnp.int32),
        ],
        scratch_shapes=[pltpu.VMEM((1, LANES), F32)],
        compiler_params=_params("arbitrary"),
        name="oproj_router",
    )(x2d, o, w_o, g, rw_pad)


def _pad_bits(tile):
    return [1 << b for b in reversed(range(int(math.log2(tile))))]


def _dispatch_kernel(s0_ref, s1_ref, pad_start_ref, pad_len_ref, h_hbm, hs_hbm, zeros, sem, zsem,
                     *, tb, n_experts, tile):
    i = pl.program_id(0)
    base = i * tb

    def row_copy(t, slot):
        return pltpu.make_async_copy(h_hbm.at[t], hs_hbm.at[slot], sem)

    def issue(g, carry):
        for r in range(DMA_UNROLL):
            t = base + g * DMA_UNROLL + r
            row_copy(t, s0_ref[t]).start()
            row_copy(t, s1_ref[t]).start()
        return carry

    lax.fori_loop(0, tb // DMA_UNROLL, issue, 0)

    def pad_copies(do):
        for e in range(n_experts):
            n = pad_len_ref[e]
            pos = pad_start_ref[e]
            for bit in _pad_bits(tile):
                @pl.when((n & bit) != 0)
                def _(pos=pos, bit=bit):
                    do(pltpu.make_async_copy(zeros.at[pl.ds(0, bit)], hs_hbm.at[pl.ds(pos, bit)], zsem))
                pos = pos + (n & bit)
        for k in range(n_experts):
            @pl.when(k < pad_len_ref[n_experts])
            def _(k=k):
                pos = pad_start_ref[n_experts] + k * tile
                do(pltpu.make_async_copy(zeros, hs_hbm.at[pl.ds(pos, tile)], zsem))

    @pl.when(i == 0)
    def _():
        zeros[...] = jnp.zeros_like(zeros)
        pad_copies(lambda cp: cp.start())
        pad_copies(lambda cp: cp.wait())

    def drain(g, carry):
        for r in range(2 * DMA_UNROLL):
            row_copy(0, 0).wait()
        return carry

    lax.fori_loop(0, tb // DMA_UNROLL, drain, 0)


def _dispatch(slot0, slot1, pad_start, pad_len, h3, n_slots, n_experts):
    t = h3.shape[0]
    tb = DISPATCH_BLOCK
    kern = functools.partial(_dispatch_kernel, tb=tb, n_experts=n_experts, tile=GROUP_TILE)
    grid_spec = pltpu.PrefetchScalarGridSpec(
        num_scalar_prefetch=4,
        grid=(t // tb,),
        in_specs=[pl.BlockSpec(memory_space=pl.ANY)],
        out_specs=pl.BlockSpec(memory_space=pl.ANY),
        scratch_shapes=[
            pltpu.VMEM((GROUP_TILE,) + h3.shape[1:], F32),
            pltpu.SemaphoreType.DMA,
            pltpu.SemaphoreType.DMA,
        ],
    )
    return pl.pallas_call(
        kern,
        grid_spec=grid_spec,
        out_shape=jax.ShapeDtypeStruct((n_slots,) + h3.shape[1:], F32),
        compiler_params=_params("arbitrary"),
        name="moe_dispatch",
    )(slot0, slot1, pad_start, pad_len, h3)


def _group_chunks(tf):
    step = 512
    return [(lo, min(lo + step, tf)) for lo in range(0, tf, step)]


def _moe_group_kernel(te_ref, nt_ref, hs_ref, wg_ref, wu_ref, wd_ref, ys_ref, hb, acc, *, tm, d, tf, nf):
    j = pl.program_id(0)
    f = pl.program_id(1)
    nc = d // LANES

    @pl.when(j >= nt_ref[0])
    def _():
        ys_ref[...] = jnp.zeros_like(ys_ref)

    @pl.when(j < nt_ref[0])
    def _():
        @pl.when(f == 0)
        def _():
            for c in range(nc):
                hb[:, c * LANES:(c + 1) * LANES] = hs_ref[pl.ds(c, tm, stride=nc), :].astype(BF16)

        h = hb[...]
        part = None
        for lo, hi in _group_chunks(tf):
            gate = _dot(h, wg_ref[:, lo:hi])
            up = _dot(h, wu_ref[:, lo:hi])
            a = (gate * jax.nn.sigmoid(gate) * up).astype(BF16)
            p = _dot(a, wd_ref[lo:hi, :])
            part = p if part is None else part + p

        @pl.when(f == 0)
        def _():
            acc[...] = part

        @pl.when(f != 0)
        def _():
            acc[...] = acc[...] + part

        @pl.when(f == nf - 1)
        def _():
            y = acc[...]
            for c in range(nc):
                ys_ref[pl.ds(c, tm, stride=nc), :] = y[:, c * LANES:(c + 1) * LANES]


def _moe_group(tile_expert, n_tiles, hs2d, w_gu, w_down, max_tiles):
    n_e, d, f2 = w_gu.shape
    f_exp = f2 // 2
    tm, tf = GROUP_TILE, MOE_F_TILE
    nf = f_exp // tf
    nc = d // LANES

    def row_map(j, f, te, nt):
        return (jnp.minimum(j, nt[0] - 1), 0)

    kern = functools.partial(_moe_group_kernel, tm=tm, d=d, tf=tf, nf=nf)
    grid_spec = pltpu.PrefetchScalarGridSpec(
        num_scalar_prefetch=2,
        grid=(max_tiles, nf),
        in_specs=[
            pl.BlockSpec((tm * nc, LANES), row_map),
            pl.BlockSpec((None, d, tf), lambda j, f, te, nt: (te[j], 0, f)),
            pl.BlockSpec((None, d, tf), lambda j, f, te, nt: (te[j], 0, nf + f)),
            pl.BlockSpec((None, tf, d), lambda j, f, te, nt: (te[j], f, 0)),
        ],
        out_specs=pl.BlockSpec((tm * nc, LANES), lambda j, f, te, nt: (j, 0)),
        scratch_shapes=[pltpu.VMEM((tm, d), BF16), pltpu.VMEM((tm, d), F32)],
    )
    return pl.pallas_call(
        kern,
        grid_spec=grid_spec,
        out_shape=jax.ShapeDtypeStruct(hs2d.shape, F32),
        compiler_params=_params("arbitrary", "arbitrary"),
        name="moe_group",
    )(tile_expert, n_tiles, hs2d, w_gu, w_gu, w_down)


def _combine_kernel(s0_ref, s1_ref, x_ref, gate_ref, ys_hbm, o_ref, buf0, buf1, sem, *, tm, d):
    i = pl.program_id(0)
    base = i * tm
    nc = d // LANES

    def row_copy(slot, buf, r):
        src = ys_hbm.at[pl.ds(pl.multiple_of(slot * nc, nc), nc), :]
        dst = buf.at[pl.ds(pl.multiple_of(r * nc, nc), nc), :]
        return pltpu.make_async_copy(src, dst, sem)

    def issue(g, carry):
        for u in range(DMA_UNROLL):
            r = g * DMA_UNROLL + u
            row_copy(s0_ref[base + r], buf0, r).start()
            row_copy(s1_ref[base + r], buf1, r).start()
        return carry

    lax.fori_loop(0, tm // DMA_UNROLL, issue, 0)

    def drain(g, carry):
        for u in range(DMA_UNROLL):
            row_copy(0, buf0, 0).wait()
            row_copy(0, buf1, 0).wait()
        return carry

    lax.fori_loop(0, tm // DMA_UNROLL, drain, 0)

    gates = gate_ref[...]
    g0 = gates[:, 0:1]
    g1 = gates[:, 1:2]
    for c in range(nc):
        y0 = buf0[pl.ds(c, tm, stride=nc), :]
        y1 = buf1[pl.ds(c, tm, stride=nc), :]
        cols = slice(c * LANES, (c + 1) * LANES)
        o_ref[:, cols] = x_ref[:, cols] + (g0 * y0 + g1 * y1)


def _combine(slot0, slot1, x3, gates, ys2d):
    t, d = x3.shape
    tm = ROW_TILE
    nc = d // LANES
    grid_spec = pltpu.PrefetchScalarGridSpec(
        num_scalar_prefetch=2,
        grid=(t // tm,),
        in_specs=[
            pl.BlockSpec((tm, d), lambda i, s0, s1: (i, 0)),
            pl.BlockSpec((tm, TOP_K), lambda i, s0, s1: (i, 0)),
            pl.BlockSpec(memory_space=pl.ANY),
        ],
        out_specs=pl.BlockSpec((tm, d), lambda i, s0, s1: (i, 0)),
        scratch_shapes=[
            pltpu.VMEM((tm * nc, LANES), F32),
            pltpu.VMEM((tm * nc, LANES), F32),
            pltpu.SemaphoreType.DMA,
        ],
    )
    return pl.pallas_call(
        functools.partial(_combine_kernel, tm=tm, d=d),
        grid_spec=grid_spec,
        out_shape=jax.ShapeDtypeStruct((t, d), F32),
        compiler_params=_params("arbitrary"),
        name="moe_combine",
    )(slot0, slot1, x3, gates, ys2d)


def _routing_tables(topi, rank, counts, n_experts, max_tiles):
    tile = GROUP_TILE
    padded = ((counts + tile - 1) // tile) * tile
    ends = jnp.cumsum(padded)
    offs = ends - padded
    onehot = topi[..., None] == jnp.arange(n_experts, dtype=jnp.int32)
    slot = jnp.sum(jnp.where(onehot, offs, 0), axis=-1) + rank
    tile_ends = ends // tile
    n_tiles = tile_ends[-1]
    j = jnp.minimum(jnp.arange(max_tiles, dtype=jnp.int32), n_tiles - 1)
    tile_expert = jnp.sum((j[:, None] >= tile_ends[None, :]).astype(jnp.int32), axis=-1)
    pad_start = jnp.concatenate([offs + counts, ends[-1:]]).astype(jnp.int32)
    pad_len = jnp.concatenate([padded - counts, max_tiles - n_tiles.reshape(1)]).astype(jnp.int32)
    return (slot[:, 0], slot[:, 1], pad_start, pad_len,
            tile_expert.astype(jnp.int32), n_tiles.reshape(1).astype(jnp.int32))


def _lambda_init(layer_idx_1based):
    return 0.8 - 0.6 * math.exp(-0.3 * (layer_idx_1based - 1))


def kernel(x, ln_mix, ln_ffn, conv_w_in, conv_w, conv_w_out, ln_kv, w_kv, k_norm, attn_w_q, q_norm, lam_params,
           sub_norm, attn_w_o, ffn_w_gu, ffn_w_down, router_w, moe_w_gu, moe_w_down):
    batch, seq, d = x.shape
    t = batch * seq
    n_experts = router_w.shape[-1]
    assert ln_mix.shape[0] == 2 and conv_w_in.shape[0] == 1 and attn_w_q.shape[0] == 1
    assert seq % ROW_TILE == 0 and seq % ATTN_TILE == 0 and d % LANES == 0
    nc = d // LANES
    x2d = x.reshape(t, d)

    x1 = _mixer_a(x2d, ln_mix[0:1], conv_w_in[0].astype(BF16), conv_w[0], conv_w_out[0].astype(BF16), seq)
    x2 = _ffn_dense(x1, ln_ffn[0:1], ffn_w_gu[0].astype(BF16), ffn_w_down[0].astype(BF16))

    n_k = d
    w_kt = w_kv[:, :n_k].T.astype(BF16)
    w_v = w_kv[:, n_k:].astype(BF16)
    q, kt, v = _qkv(x2, ln_mix[1:2], ln_kv.reshape(1, d), attn_w_q[0].astype(BF16), w_kt, w_v,
                    k_norm.reshape(HEAD_DIM, 1))

    o = _attention(q, kt, v, q_norm[0:1], lam_params[0], sub_norm[0:1], batch, seq, _lambda_init(2))

    rw_pad = jnp.pad(router_w[0], ((0, 0), (0, LANES - n_experts)))
    x3, h8, topi, gates, rank, counts = _oproj_router(x2, o, attn_w_o[0].astype(BF16), ln_ffn[1:2], rw_pad,
                                                      n_experts)
    max_tiles = (TOP_K * t) // GROUP_TILE + n_experts
    n_slots = max_tiles * GROUP_TILE
    slot0, slot1, pad_start, pad_len, tile_expert, n_tiles = _routing_tables(
        topi, rank, counts[0, :n_experts], n_experts, max_tiles)
    hs = _dispatch(slot0, slot1, pad_start, pad_len, h8.reshape(t, nc, LANES), n_slots, n_experts)
    ys = _moe_group(tile_expert, n_tiles, hs.reshape(n_slots * nc, LANES),
                    moe_w_gu[0].astype(BF16), moe_w_down[0].astype(BF16), max_tiles)
    out = _combine(slot0, slot1, x3, gates, ys)
    return out.reshape(batch, seq, d)
```

```python
import functools
import math

import jax
import jax.numpy as jnp
from jax import lax
from jax.experimental import pallas as pl
from jax.experimental.pallas import tpu as pltpu

F32 = jnp.float32
BF16 = jnp.bfloat16

EPS = 1e-6
HEAD_DIM = 64
V_DIM = 2 * HEAD_DIM
CONV_WIDTH = 3
TOP_K = 2

LANES = 128
SUBLANES = 8
VMEM_LIMIT = 56 * 1024 * 1024

ROW_TILE = 512
ATTN_TILE = 256
GROUP_TILE = 512
MOE_F_TILE = 1792
DISPATCH_BLOCK = 1024
DMA_UNROLL = 16

LOG2E = 1.4426950408889634
SAFE_SHIFT = 40.0


def _rms_scale(x):
    return x * lax.rsqrt(jnp.mean(x * x, axis=-1, keepdims=True) + EPS)


def _dot(a, b):
    return jnp.dot(a, b, preferred_element_type=F32)


def _params(*sem):
    return pltpu.CompilerParams(dimension_semantics=sem, vmem_limit_bytes=VMEM_LIMIT)


def _mixer_a_kernel(x_ref, g_ref, win_ref, cw_ref, wout_ref, o_ref, ubuf, *, tm, d, tiles_per_seq):
    i = pl.program_id(0)
    x = x_ref[...]
    h = (_rms_scale(x) * g_ref[...]).astype(BF16)
    c = _dot(h, win_ref[:, d:2 * d])
    v = _dot(h, win_ref[:, 2 * d:])
    u = c * v

    @pl.when(i % tiles_per_seq == 0)
    def _():
        ubuf[0:SUBLANES, :] = jnp.zeros((SUBLANES, d), F32)

    ubuf[SUBLANES:tm + SUBLANES, :] = u
    u1 = ubuf[SUBLANES - 1:tm + SUBLANES - 1, :]
    u2 = ubuf[SUBLANES - 2:tm + SUBLANES - 2, :]
    cw = cw_ref[...]
    z = u2 * cw[0:1] + u1 * cw[1:2] + u * cw[2:3]
    b = _dot(h, win_ref[:, 0:d])
    y = (b * z).astype(BF16)
    o_ref[...] = x + _dot(y, wout_ref[...])
    ubuf[0:SUBLANES, :] = ubuf[tm:tm + SUBLANES, :]


def _mixer_a(x2d, g, w_in, conv_w, w_out, seq):
    t, d = x2d.shape
    tm = ROW_TILE
    kern = functools.partial(_mixer_a_kernel, tm=tm, d=d, tiles_per_seq=seq // tm)
    return pl.pallas_call(
        kern,
        grid=(t // tm,),
        in_specs=[
            pl.BlockSpec((tm, d), lambda i: (i, 0)),
            pl.BlockSpec((1, d), lambda i: (0, 0)),
            pl.BlockSpec((d, 3 * d), lambda i: (0, 0)),
            pl.BlockSpec((CONV_WIDTH, d), lambda i: (0, 0)),
            pl.BlockSpec((d, d), lambda i: (0, 0)),
        ],
        out_specs=pl.BlockSpec((tm, d), lambda i: (i, 0)),
        out_shape=jax.ShapeDtypeStruct((t, d), F32),
        scratch_shapes=[pltpu.VMEM((tm + SUBLANES, d), F32)],
        compiler_params=_params("arbitrary"),
        name="mixer_a",
    )(x2d, g, w_in, conv_w, w_out)


def _ffn_chunks(f):
    step = 1024
    return [(lo, min(lo + step, f)) for lo in range(0, f, step)]


def _ffn_kernel(x_ref, g_ref, wgu_ref, wd_ref, o_ref, *, f):
    x = x_ref[...]
    h = (_rms_scale(x) * g_ref[...]).astype(BF16)
    acc = x
    for lo, hi in _ffn_chunks(f):
        gate = _dot(h, wgu_ref[:, lo:hi])
        up = _dot(h, wgu_ref[:, f + lo:f + hi])
        a = (gate * jax.nn.sigmoid(gate) * up).astype(BF16)
        acc = acc + _dot(a, wd_ref[lo:hi, :])
    o_ref[...] = acc


def _ffn_dense(x2d, g, w_gu, w_down):
    t, d = x2d.shape
    f = w_down.shape[0]
    tm = ROW_TILE
    return pl.pallas_call(
        functools.partial(_ffn_kernel, f=f),
        grid=(t // tm,),
        in_specs=[
            pl.BlockSpec((tm, d), lambda i: (i, 0)),
            pl.BlockSpec((1, d), lambda i: (0, 0)),
            pl.BlockSpec((d, 2 * f), lambda i: (0, 0)),
            pl.BlockSpec((f, d), lambda i: (0, 0)),
        ],
        out_specs=pl.BlockSpec((tm, d), lambda i: (i, 0)),
        out_shape=jax.ShapeDtypeStruct((t, d), F32),
        compiler_params=_params("parallel"),
        name="ffn_dense",
    )(x2d, g, w_gu, w_down)


def _qkv_kernel(x_ref, gq_ref, gkv_ref, wq_ref, wkt_ref, wv_ref, kn_ref, q_ref, kt_ref, v_ref, *, tm, d):
    y = _rms_scale(x_ref[...])
    hq = (y * gq_ref[...]).astype(BF16)
    hkv = (y * gkv_ref[...]).astype(BF16)
    q_ref[...] = _dot(hq, wq_ref[...]).astype(BF16)
    v_ref[...] = _dot(hkv, wv_ref[...]).astype(BF16)
    kt = lax.dot_general(wkt_ref[...], hkv, (((1,), (1,)), ((), ())), preferred_element_type=F32)
    k3 = kt.reshape(d // HEAD_DIM, HEAD_DIM, tm)
    k3 = k3 * lax.rsqrt(jnp.mean(k3 * k3, axis=1, keepdims=True) + EPS) * kn_ref[...][None]
    kt_ref[...] = k3.reshape(d, tm).astype(BF16)


def _qkv(x2d, g_q, g_kv, w_q, w_kt, w_v, k_norm_col):
    t, d = x2d.shape
    tm = ROW_TILE
    const = lambda i: (0, 0)
    return pl.pallas_call(
        functools.partial(_qkv_kernel, tm=tm, d=d),
        grid=(t // tm,),
        in_specs=[
            pl.BlockSpec((tm, d), lambda i: (i, 0)),
            pl.BlockSpec((1, d), const),
            pl.BlockSpec((1, d), const),
            pl.BlockSpec((d, d), const),
            pl.BlockSpec((d, d), const),
            pl.BlockSpec((d, d), const),
            pl.BlockSpec((HEAD_DIM, 1), const),
        ],
        out_specs=[
            pl.BlockSpec((tm, d), lambda i: (i, 0)),
            pl.BlockSpec((d, tm), lambda i: (0, i)),
            pl.BlockSpec((tm, d), lambda i: (i, 0)),
        ],
        out_shape=[
            jax.ShapeDtypeStruct((t, d), BF16),
            jax.ShapeDtypeStruct((d, t), BF16),
            jax.ShapeDtypeStruct((t, d), BF16),
        ],
        compiler_params=_params("parallel"),
        name="qkv_proj",
    )(x2d, g_q, g_kv, w_q, w_kt, w_v, k_norm_col)


def _group_mean_sq(x, ones_bd, group):
    sq = x * x
    hi = sq.astype(BF16)
    lo = (sq - hi.astype(F32)).astype(BF16)
    return (_dot(hi, ones_bd) + _dot(lo, ones_bd)) * (1.0 / group)


def _attn_kernel(bound_ref, q_ref, kt_ref, v_ref, qn_ref, lam_ref, sn_ref, o_ref, v1_ref, qn_s,
                 *, seq, tq, lam_init):
    lp = lam_ref[...]
    lam = (jnp.exp(jnp.sum(lp[0:1] * lp[1:2], axis=-1, keepdims=True))
           - jnp.exp(jnp.sum(lp[2:3] * lp[3:4], axis=-1, keepdims=True)) + lam_init)
    row = lax.broadcasted_iota(jnp.int32, (tq, tq), 0)
    col = lax.broadcasted_iota(jnp.int32, (tq, tq), 1)
    causal = col <= row
    gi = lax.broadcasted_iota(jnp.int32, (V_DIM, V_DIM), 0) // HEAD_DIM
    gj = lax.broadcasted_iota(jnp.int32, (V_DIM, V_DIM), 1) // HEAD_DIM
    ones_bd = jnp.where(gi == gj, 1.0, 0.0).astype(BF16)
    v1_ref[:, 0:V_DIM] = v_ref[...]
    v1_ref[:, V_DIM:] = jnp.ones((seq, V_DIM), BF16)
    q = q_ref[...].astype(F32)
    ms = _group_mean_sq(q, ones_bd, HEAD_DIM)
    qn_s[...] = (q * lax.rsqrt(ms + EPS) * (qn_ref[...] * (HEAD_DIM ** -0.5 * LOG2E))).astype(BF16)
    maps = [slice(c * HEAD_DIM, (c + 1) * HEAD_DIM) for c in range(2)]

    def finish(q0, r1, r2):
        o = r1[:, 0:V_DIM] / r1[:, V_DIM:] - lam * (r2[:, 0:V_DIM] / r2[:, V_DIM:])
        o = _rms_scale(o) * sn_ref[...] * (1.0 - lam_init)
        o_ref[pl.ds(q0, tq), :] = o.astype(BF16)

    @pl.when(bound_ref[0] <= SAFE_SHIFT)
    def _fixed_shift():
        shift = bound_ref[0] * LOG2E
        for qi in range(seq // tq):
            q0 = qi * tq
            res = []
            for sl in maps:
                qc = qn_s[q0:q0 + tq, sl]
                s = _dot(qc, kt_ref[sl, q0:q0 + tq])
                p = jnp.where(causal, jnp.exp2(s - shift), 0.0).astype(BF16)
                r = _dot(p, v1_ref[q0:q0 + tq, :])
                if q0 > 0:
                    s = _dot(qc, kt_ref[sl, 0:q0])
                    r = r + _dot(jnp.exp2(s - shift).astype(BF16), v1_ref[0:q0, :])
                res.append(r)
            finish(q0, *res)

    @pl.when(jnp.logical_not(bound_ref[0] <= SAFE_SHIFT))
    def _running_max():
        def q_body(qi, carry):
            q0 = pl.multiple_of(qi * tq, tq)
            qs = [qn_s[pl.ds(q0, tq), sl] for sl in maps]

            def tile(j, state, masked):
                k0 = pl.multiple_of(j * tq, tq)
                out = []
                for c, sl in enumerate(maps):
                    m, r = state[c]
                    s = _dot(qs[c], kt_ref[sl, pl.ds(k0, tq)])
                    if masked:
                        s = jnp.where(causal, s, -jnp.inf)
                    m_new = jnp.maximum(m, jnp.max(s, axis=-1, keepdims=True))
                    p = jnp.exp2(s - m_new).astype(BF16)
                    r = jnp.exp2(m - m_new) * r + _dot(p, v1_ref[pl.ds(k0, tq), :])
                    out.append((m_new, r))
                return tuple(out)

            one = (jnp.full((tq, 1), -jnp.inf, F32), jnp.zeros((tq, 2 * V_DIM), F32))
            state = lax.fori_loop(0, qi, lambda j, st: tile(j, st, False), (one, one))
            (_, r1), (_, r2) = tile(qi, state, True)
            finish(q0, r1, r2)
            return carry

        lax.fori_loop(0, seq // tq, q_body, 0)


def _attention(score_bound, q, kt, v, q_norm2, lam_params, sub_norm, batch, seq, lam_init):
    t, d = q.shape
    n_heads = d // V_DIM
    kern = functools.partial(_attn_kernel, seq=seq, tq=ATTN_TILE, lam_init=lam_init)
    return pl.pallas_call(
        kern,
        grid=(batch, n_heads),
        in_specs=[
            pl.BlockSpec(memory_space=pltpu.SMEM),
            pl.BlockSpec((seq, V_DIM), lambda b, h: (b, h)),
            pl.BlockSpec((V_DIM, seq), lambda b, h: (h, b)),
            pl.BlockSpec((seq, V_DIM), lambda b, h: (b, h)),
            pl.BlockSpec((1, V_DIM), lambda b, h: (0, 0)),
            pl.BlockSpec((4, HEAD_DIM), lambda b, h: (0, 0)),
            pl.BlockSpec((1, V_DIM), lambda b, h: (0, 0)),
        ],
        out_specs=pl.BlockSpec((seq, V_DIM), lambda b, h: (b, h)),
        out_shape=jax.ShapeDtypeStruct((t, d), BF16),
        scratch_shapes=[pltpu.VMEM((seq, 2 * V_DIM), BF16), pltpu.VMEM((seq, V_DIM), BF16)],
        compiler_params=_params("parallel", "parallel"),
        name="diff_attn",
    )(score_bound, q, kt, v, q_norm2, lam_params, sub_norm)


def _oproj_router_kernel(x_ref, o_ref, wo_ref, g_ref, rw_ref,
                         x3_ref, h8_ref, topi_ref, gate_ref, rank_ref, cnt_ref, carry,
                         *, tm, d, n_experts):
    i = pl.program_id(0)

    @pl.when(i == 0)
    def _():
        carry[...] = jnp.zeros_like(carry)

    x3 = x_ref[...] + _dot(o_ref[...], wo_ref[...])
    x3_ref[...] = x3
    h = _rms_scale(x3) * g_ref[...]
    for c in range(d // LANES):
        h8_ref[pl.ds(c, tm, stride=d // LANES), :] = h[:, c * LANES:(c + 1) * LANES]

    rw = rw_ref[...]
    h_hi = h.astype(BF16)
    h_lo = (h - h_hi.astype(F32)).astype(BF16)
    w_hi = rw.astype(BF16)
    w_lo = (rw - w_hi.astype(F32)).astype(BF16)
    logits = _dot(h_hi, w_hi) + (_dot(h_hi, w_lo) + _dot(h_lo, w_hi))

    lane = lax.broadcasted_iota(jnp.int32, (tm, LANES), 1)
    neg = -jnp.inf
    l1 = jnp.where(lane < n_experts, logits, neg)
    m1 = jnp.max(l1, axis=-1, keepdims=True)
    i1 = jnp.min(jnp.where(l1 == m1, lane, LANES), axis=-1, keepdims=True)
    l2 = jnp.where(lane == i1, neg, l1)
    m2 = jnp.max(l2, axis=-1, keepdims=True)
    i2 = jnp.min(jnp.where(l2 == m2, lane, LANES), axis=-1, keepdims=True)
    e2 = jnp.exp(m2 - m1)
    den = 1.0 + e2
    first = lax.broadcasted_iota(jnp.int32, (tm, TOP_K), 1) == 0
    topi_ref[...] = jnp.where(first, i1, i2)
    gate_ref[...] = jnp.where(first, 1.0 / den, e2 / den)

    oh1 = lane == i1
    oh2 = lane == i2
    cnt = jnp.where(oh1, 1.0, 0.0) + jnp.where(oh2, 1.0, 0.0)
    r = lax.broadcasted_iota(jnp.int32, (tm, tm), 0)
    cidx = lax.broadcasted_iota(jnp.int32, (tm, tm), 1)
    lower = jnp.where(cidx < r, 1.0, 0.0).astype(BF16)
    before = _dot(lower, cnt.astype(BF16)) + carry[...]
    r1 = jnp.sum(jnp.where(oh1, before, 0.0), axis=-1, keepdims=True)
    r2 = jnp.sum(jnp.where(oh2, before, 0.0), axis=-1, keepdims=True)
    rank_ref[...] = jnp.where(first, r1, r2).astype(jnp.int32)
    total = carry[...] + jnp.sum(cnt, axis=0, keepdims=True)
    carry[...] = total
    cnt_ref[...] = total.astype(jnp.int32)


def _oproj_router(x2d, o, w_o, g, rw_pad, n_experts):
    t, d = x2d.shape
    tm = ROW_TILE
    const = lambda i: (0, 0)
    kern = functools.partial(_oproj_router_kernel, tm=tm, d=d, n_experts=n_experts)
    return pl.pallas_call(
        kern,
        grid=(t // tm,),
        in_specs=[
            pl.BlockSpec((tm, d), lambda i: (i, 0)),
            pl.BlockSpec((tm, d), lambda i: (i, 0)),
            pl.BlockSpec((d, d), const),
            pl.BlockSpec((1, d), const),
            pl.BlockSpec((d, LANES), const),
        ],
        out_specs=[
            pl.BlockSpec((tm, d), lambda i: (i, 0)),
            pl.BlockSpec((tm * (d // LANES), LANES), lambda i: (i, 0)),
            pl.BlockSpec((tm, TOP_K), lambda i: (i, 0)),
            pl.BlockSpec((tm, TOP_K), lambda i: (i, 0)),
            pl.BlockSpec((tm, TOP_K), lambda i: (i, 0)),
            pl.BlockSpec((1, LANES), const),
        ],
        out_shape=[
            jax.ShapeDtypeStruct((t, d), F32),
            jax.ShapeDtypeStruct((t * (d // LANES), LANES), F32),
            jax.ShapeDtypeStruct((t, TOP_K), jnp.int32),
            jax.ShapeDtypeStruct((t, TOP_K), F32),
            jax.ShapeDtypeStruct((t, TOP_K), jnp.int32),
            jax.ShapeDtypeStruct((1, LANES), jnp.int32),
        ],
        scratch_shapes=[pltpu.VMEM((1, LANES), F32)],
        compiler_params=_params("arbitrary"),
        name="oproj_router",
    )(x2d, o, w_o, g, rw_pad)


def _pad_bits(tile):
    return [1 << b for b in reversed(range(int(math.log2(tile))))]


def _dispatch_kernel(s0_ref, s1_ref, pad_start_ref, pad_len_ref, h_ref, hs_hbm, zeros, sem, zsem,
                     *, tb, nc, n_experts, tile):
    i = pl.program_id(0)
    base = i * tb

    def rows(ref, first, count):
        start = first * nc
        if not isinstance(start, int):
            start = pl.multiple_of(start, nc)
        return ref.at[pl.ds(start, count * nc), :]

    def row_copy(r, slot):
        return pltpu.make_async_copy(rows(h_ref, r, 1), rows(hs_hbm, slot, 1), sem)

    def issue(g, carry):
        for u in range(DMA_UNROLL):
            r = g * DMA_UNROLL + u
            row_copy(r, s0_ref[base + r]).start()
            row_copy(r, s1_ref[base + r]).start()
        return carry

    lax.fori_loop(0, tb // DMA_UNROLL, issue, 0)

    def pad_copies(do):
        for e in range(n_experts):
            n = pad_len_ref[e]
            pos = pad_start_ref[e]
            for bit in _pad_bits(tile):
                @pl.when((n & bit) != 0)
                def _(pos=pos, bit=bit):
                    do(pltpu.make_async_copy(rows(zeros, 0, bit), rows(hs_hbm, pos, bit), zsem))
                pos = pos + (n & bit)
        for k in range(n_experts):
            @pl.when(k < pad_len_ref[n_experts])
            def _(k=k):
                pos = pad_start_ref[n_experts] + k * tile
                do(pltpu.make_async_copy(zeros, rows(hs_hbm, pos, tile), zsem))

    @pl.when(i == 0)
    def _():
        zeros[...] = jnp.zeros_like(zeros)
        pad_copies(lambda cp: cp.start())
        pad_copies(lambda cp: cp.wait())

    def drain(g, carry):
        for u in range(2 * DMA_UNROLL):
            row_copy(0, 0).wait()
        return carry

    lax.fori_loop(0, tb // DMA_UNROLL, drain, 0)


def _dispatch(slot0, slot1, pad_start, pad_len, h8, nc, n_slots, n_experts):
    t = h8.shape[0] // nc
    tb = DISPATCH_BLOCK
    kern = functools.partial(_dispatch_kernel, tb=tb, nc=nc, n_experts=n_experts, tile=GROUP_TILE)
    grid_spec = pltpu.PrefetchScalarGridSpec(
        num_scalar_prefetch=4,
        grid=(t // tb,),
        in_specs=[pl.BlockSpec((tb * nc, LANES), lambda i, *_: (i, 0))],
        out_specs=pl.BlockSpec(memory_space=pl.ANY),
        scratch_shapes=[
            pltpu.VMEM((GROUP_TILE * nc, LANES), F32),
            pltpu.SemaphoreType.DMA,
            pltpu.SemaphoreType.DMA,
        ],
    )
    return pl.pallas_call(
        kern,
        grid_spec=grid_spec,
        out_shape=jax.ShapeDtypeStruct((n_slots * nc, LANES), F32),
        compiler_params=_params("arbitrary"),
        name="moe_dispatch",
    )(slot0, slot1, pad_start, pad_len, h8)


def _group_chunks(tf):
    step = 512
    return [(lo, min(lo + step, tf)) for lo in range(0, tf, step)]


def _moe_group_kernel(te_ref, nt_ref, hs_ref, wg_ref, wu_ref, wd_ref, ys_ref, hb, acc, *, tm, d, tf, nf):
    j = pl.program_id(0)
    f = pl.program_id(1)
    nc = d // LANES

    @pl.when(j >= nt_ref[0])
    def _():
        ys_ref[...] = jnp.zeros_like(ys_ref)

    @pl.when(j < nt_ref[0])
    def _():
        @pl.when(f == 0)
        def _():
            for c in range(nc):
                hb[:, c * LANES:(c + 1) * LANES] = hs_ref[pl.ds(c, tm, stride=nc), :].astype(BF16)

        h = hb[...]
        part = None
        for lo, hi in _group_chunks(tf):
            gate = _dot(h, wg_ref[:, lo:hi])
            up = _dot(h, wu_ref[:, lo:hi])
            a = (gate * jax.nn.sigmoid(gate) * up).astype(BF16)
            p = _dot(a, wd_ref[lo:hi, :])
            part = p if part is None else part + p

        @pl.when(f == 0)
        def _():
            acc[...] = part

        @pl.when(f != 0)
        def _():
            acc[...] = acc[...] + part

        @pl.when(f == nf - 1)
        def _():
            y = acc[...]
            for c in range(nc):
                ys_ref[pl.ds(c, tm, stride=nc), :] = y[:, c * LANES:(c + 1) * LANES]


def _moe_group(tile_expert, n_tiles, hs2d, w_gu, w_down, max_tiles):
    n_e, d, f2 = w_gu.shape
    f_exp = f2 // 2
    tm, tf = GROUP_TILE, MOE_F_TILE
    nf = f_exp // tf
    nc = d // LANES

    def row_map(j, f, te, nt):
        return (jnp.minimum(j, nt[0] - 1), 0)

    kern = functools.partial(_moe_group_kernel, tm=tm, d=d, tf=tf, nf=nf)
    grid_spec = pltpu.PrefetchScalarGridSpec(
        num_scalar_prefetch=2,
        grid=(max_tiles, nf),
        in_specs=[
            pl.BlockSpec((tm * nc, LANES), row_map),
            pl.BlockSpec((None, d, tf), lambda j, f, te, nt: (te[j], 0, f)),
            pl.BlockSpec((None, d, tf), lambda j, f, te, nt: (te[j], 0, nf + f)),
            pl.BlockSpec((None, tf, d), lambda j, f, te, nt: (te[j], f, 0)),
        ],
        out_specs=pl.BlockSpec((tm * nc, LANES), lambda j, f, te, nt: (j, 0)),
        scratch_shapes=[pltpu.VMEM((tm, d), BF16), pltpu.VMEM((tm, d), F32)],
    )
    return pl.pallas_call(
        kern,
        grid_spec=grid_spec,
        out_shape=jax.ShapeDtypeStruct(hs2d.shape, F32),
        compiler_params=_params("arbitrary", "arbitrary"),
        name="moe_group",
    )(tile_expert, n_tiles, hs2d, w_gu, w_gu, w_down)


def _combine_kernel(s0_ref, s1_ref, x_ref, gate_ref, ys_hbm, o_ref, buf0, buf1, sem, *, tm, d):
    i = pl.program_id(0)
    base = i * tm
    nc = d // LANES

    def row_copy(slot, buf, r):
        src = ys_hbm.at[pl.ds(pl.multiple_of(slot * nc, nc), nc), :]
        dst = buf.at[pl.ds(pl.multiple_of(r * nc, nc), nc), :]
        return pltpu.make_async_copy(src, dst, sem)

    def issue(g, carry):
        for u in range(DMA_UNROLL):
            r = g * DMA_UNROLL + u
            row_copy(s0_ref[base + r], buf0, r).start()
            row_copy(s1_ref[base + r], buf1, r).start()
        return carry

    lax.fori_loop(0, tm // DMA_UNROLL, issue, 0)

    def drain(g, carry):
        for u in range(DMA_UNROLL):
            row_copy(0, buf0, 0).wait()
            row_copy(0, buf1, 0).wait()
        return carry

    lax.fori_loop(0, tm // DMA_UNROLL, drain, 0)

    gates = gate_ref[...]
    g0 = gates[:, 0:1]
    g1 = gates[:, 1:2]
    for c in range(nc):
        y0 = buf0[pl.ds(c, tm, stride=nc), :]
        y1 = buf1[pl.ds(c, tm, stride=nc), :]
        cols = slice(c * LANES, (c + 1) * LANES)
        o_ref[:, cols] = x_ref[:, cols] + (g0 * y0 + g1 * y1)


def _combine(slot0, slot1, x3, gates, ys2d):
    t, d = x3.shape
    tm = ROW_TILE
    nc = d // LANES
    grid_spec = pltpu.PrefetchScalarGridSpec(
        num_scalar_prefetch=2,
        grid=(t // tm,),
        in_specs=[
            pl.BlockSpec((tm, d), lambda i, s0, s1: (i, 0)),
            pl.BlockSpec((tm, TOP_K), lambda i, s0, s1: (i, 0)),
            pl.BlockSpec(memory_space=pl.ANY),
        ],
        out_specs=pl.BlockSpec((tm, d), lambda i, s0, s1: (i, 0)),
        scratch_shapes=[
            pltpu.VMEM((tm * nc, LANES), F32),
            pltpu.VMEM((tm * nc, LANES), F32),
            pltpu.SemaphoreType.DMA,
        ],
    )
    return pl.pallas_call(
        functools.partial(_combine_kernel, tm=tm, d=d),
        grid_spec=grid_spec,
        out_shape=jax.ShapeDtypeStruct((t, d), F32),
        compiler_params=_params("arbitrary"),
        name="moe_combine",
    )(slot0, slot1, x3, gates, ys2d)


def _routing_tables(topi, rank, counts, n_experts, max_tiles):
    tile = GROUP_TILE
    padded = ((counts + tile - 1) // tile) * tile
    ends = jnp.cumsum(padded)
    offs = ends - padded
    onehot = topi[..., None] == jnp.arange(n_experts, dtype=jnp.int32)
    slot = jnp.sum(jnp.where(onehot, offs, 0), axis=-1) + rank
    tile_ends = ends // tile
    n_tiles = tile_ends[-1]
    j = jnp.minimum(jnp.arange(max_tiles, dtype=jnp.int32), n_tiles - 1)
    tile_expert = jnp.sum((j[:, None] >= tile_ends[None, :]).astype(jnp.int32), axis=-1)
    pad_start = jnp.concatenate([offs + counts, ends[-1:]]).astype(jnp.int32)
    pad_len = jnp.concatenate([padded - counts, max_tiles - n_tiles.reshape(1)]).astype(jnp.int32)
    return (slot[:, 0], slot[:, 1], pad_start, pad_len,
            tile_expert.astype(jnp.int32), n_tiles.reshape(1).astype(jnp.int32))


def _lambda_init(layer_idx_1based):
    return 0.8 - 0.6 * math.exp(-0.3 * (layer_idx_1based - 1))


def kernel(x, ln_mix, ln_ffn, conv_w_in, conv_w, conv_w_out, ln_kv, w_kv, k_norm, attn_w_q, q_norm, lam_params,
           sub_norm, attn_w_o, ffn_w_gu, ffn_w_down, router_w, moe_w_gu, moe_w_down):
    batch, seq, d = x.shape
    t = batch * seq
    n_experts = router_w.shape[-1]
    assert ln_mix.shape[0] == 2 and conv_w_in.shape[0] == 1 and attn_w_q.shape[0] == 1
    assert seq % ROW_TILE == 0 and seq % ATTN_TILE == 0 and d % LANES == 0
    nc = d // LANES
    x2d = x.reshape(t, d)

    x1 = _mixer_a(x2d, ln_mix[0:1], conv_w_in[0].astype(BF16), conv_w[0], conv_w_out[0].astype(BF16), seq)
    x2 = _ffn_dense(x1, ln_ffn[0:1], ffn_w_gu[0].astype(BF16), ffn_w_down[0].astype(BF16))

    n_k = d
    w_kt = w_kv[:, :n_k].T.astype(BF16)
    w_v = w_kv[:, n_k:].astype(BF16)
    q, kt, v = _qkv(x2, ln_mix[1:2], ln_kv.reshape(1, d), attn_w_q[0].astype(BF16), w_kt, w_v,
                    k_norm.reshape(HEAD_DIM, 1))

    score_bound = (math.sqrt(HEAD_DIM) * jnp.max(jnp.abs(q_norm[0])) * jnp.max(jnp.abs(k_norm))).reshape(1)
    o = _attention(score_bound, q, kt, v, jnp.tile(q_norm[0:1], (1, 2)), lam_params[0], sub_norm[0:1],
                   batch, seq, _lambda_init(2))

    rw_pad = jnp.pad(router_w[0], ((0, 0), (0, LANES - n_experts)))
    x3, h8, topi, gates, rank, counts = _oproj_router(x2, o, attn_w_o[0].astype(BF16), ln_ffn[1:2], rw_pad,
                                                      n_experts)
    max_tiles = (TOP_K * t) // GROUP_TILE + n_experts
    n_slots = max_tiles * GROUP_TILE
    slot0, slot1, pad_start, pad_len, tile_expert, n_tiles = _routing_tables(
        topi, rank, counts[0, :n_experts], n_experts, max_tiles)
    hs = _dispatch(slot0, slot1, pad_start, pad_len, h8, nc, n_slots, n_experts)
    ys = _moe_group(tile_expert, n_tiles, hs, moe_w_gu[0].astype(BF16), moe_w_down[0].astype(BF16), max_tiles)
    out = _combine(slot0, slot1, x3, gates, ys)
    return out.reshape(batch, seq, d)
```

```python
import functools
import math

import jax
import jax.numpy as jnp
from jax import lax
from jax.experimental import pallas as pl
from jax.experimental.pallas import tpu as pltpu

F32 = jnp.float32
BF16 = jnp.bfloat16

EPS = 1e-6
HEAD_DIM = 64
V_DIM = 2 * HEAD_DIM
CONV_WIDTH = 3
TOP_K = 2

LANES = 128
SUBLANES = 8
VMEM_LIMIT = 56 * 1024 * 1024

ROW_TILE = 512
ATTN_TILE = 256
GROUP_TILE = 512
MOE_F_TILE = 1792
DISPATCH_BLOCK = 1024
DMA_UNROLL = 16

LOG2E = 1.4426950408889634
SAFE_SHIFT = 40.0


def _rms_scale(x):
    return x * lax.rsqrt(jnp.mean(x * x, axis=-1, keepdims=True) + EPS)


def _dot(a, b):
    return jnp.dot(a, b, preferred_element_type=F32)


def _params(*sem):
    return pltpu.CompilerParams(dimension_semantics=sem, vmem_limit_bytes=VMEM_LIMIT)


def _mixer_a_kernel(x_ref, g_ref, win_ref, cw_ref, wout_ref, o_ref, ubuf, *, tm, d, tiles_per_seq):
    i = pl.program_id(0)
    x = x_ref[...]
    h = (_rms_scale(x) * g_ref[...]).astype(BF16)
    c = _dot(h, win_ref[:, d:2 * d])
    v = _dot(h, win_ref[:, 2 * d:])
    u = c * v

    @pl.when(i % tiles_per_seq == 0)
    def _():
        ubuf[0:SUBLANES, :] = jnp.zeros((SUBLANES, d), F32)

    ubuf[SUBLANES:tm + SUBLANES, :] = u
    u1 = ubuf[SUBLANES - 1:tm + SUBLANES - 1, :]
    u2 = ubuf[SUBLANES - 2:tm + SUBLANES - 2, :]
    cw = cw_ref[...]
    z = u2 * cw[0:1] + u1 * cw[1:2] + u * cw[2:3]
    b = _dot(h, win_ref[:, 0:d])
    y = (b * z).astype(BF16)
    o_ref[...] = x + _dot(y, wout_ref[...])
    ubuf[0:SUBLANES, :] = ubuf[tm:tm + SUBLANES, :]


def _mixer_a(x2d, g, w_in, conv_w, w_out, seq):
    t, d = x2d.shape
    tm = ROW_TILE
    kern = functools.partial(_mixer_a_kernel, tm=tm, d=d, tiles_per_seq=seq // tm)
    return pl.pallas_call(
        kern,
        grid=(t // tm,),
        in_specs=[
            pl.BlockSpec((tm, d), lambda i: (i, 0)),
            pl.BlockSpec((1, d), lambda i: (0, 0)),
            pl.BlockSpec((d, 3 * d), lambda i: (0, 0)),
            pl.BlockSpec((CONV_WIDTH, d), lambda i: (0, 0)),
            pl.BlockSpec((d, d), lambda i: (0, 0)),
        ],
        out_specs=pl.BlockSpec((tm, d), lambda i: (i, 0)),
        out_shape=jax.ShapeDtypeStruct((t, d), F32),
        scratch_shapes=[pltpu.VMEM((tm + SUBLANES, d), F32)],
        compiler_params=_params("arbitrary"),
        name="mixer_a",
    )(x2d, g, w_in, conv_w, w_out)


def _ffn_chunks(f):
    step = 1024
    return [(lo, min(lo + step, f)) for lo in range(0, f, step)]


def _ffn_kernel(x_ref, g_ref, wgu_ref, wd_ref, o_ref, *, f):
    x = x_ref[...]
    h = (_rms_scale(x) * g_ref[...]).astype(BF16)
    acc = x
    for lo, hi in _ffn_chunks(f):
        gate = _dot(h, wgu_ref[:, lo:hi])
        up = _dot(h, wgu_ref[:, f + lo:f + hi])
        a = (gate * jax.nn.sigmoid(gate) * up).astype(BF16)
        acc = acc + _dot(a, wd_ref[lo:hi, :])
    o_ref[...] = acc


def _ffn_dense(x2d, g, w_gu, w_down):
    t, d = x2d.shape
    f = w_down.shape[0]
    tm = ROW_TILE
    return pl.pallas_call(
        functools.partial(_ffn_kernel, f=f),
        grid=(t // tm,),
        in_specs=[
            pl.BlockSpec((tm, d), lambda i: (i, 0)),
            pl.BlockSpec((1, d), lambda i: (0, 0)),
            pl.BlockSpec((d, 2 * f), lambda i: (0, 0)),
            pl.BlockSpec((f, d), lambda i: (0, 0)),
        ],
        out_specs=pl.BlockSpec((tm, d), lambda i: (i, 0)),
        out_shape=jax.ShapeDtypeStruct((t, d), F32),
        compiler_params=_params("parallel"),
        name="ffn_dense",
    )(x2d, g, w_gu, w_down)


def _qkv_kernel(x_ref, gq_ref, gkv_ref, wq_ref, wkt_ref, wv_ref, kn_ref, q_ref, kt_ref, v_ref, *, tm, d):
    y = _rms_scale(x_ref[...])
    hq = (y * gq_ref[...]).astype(BF16)
    hkv = (y * gkv_ref[...]).astype(BF16)
    q_ref[...] = _dot(hq, wq_ref[...]).astype(BF16)
    v_ref[...] = _dot(hkv, wv_ref[...]).astype(BF16)
    kt = lax.dot_general(wkt_ref[...], hkv, (((1,), (1,)), ((), ())), preferred_element_type=F32)
    k3 = kt.reshape(d // HEAD_DIM, HEAD_DIM, tm)
    k3 = k3 * lax.rsqrt(jnp.mean(k3 * k3, axis=1, keepdims=True) + EPS) * kn_ref[...][None]
    kt_ref[...] = k3.reshape(d, tm).astype(BF16)


def _qkv(x2d, g_q, g_kv, w_q, w_kt, w_v, k_norm_col):
    t, d = x2d.shape
    tm = ROW_TILE
    const = lambda i: (0, 0)
    return pl.pallas_call(
        functools.partial(_qkv_kernel, tm=tm, d=d),
        grid=(t // tm,),
        in_specs=[
            pl.BlockSpec((tm, d), lambda i: (i, 0)),
            pl.BlockSpec((1, d), const),
            pl.BlockSpec((1, d), const),
            pl.BlockSpec((d, d), const),
            pl.BlockSpec((d, d), const),
            pl.BlockSpec((d, d), const),
            pl.BlockSpec((HEAD_DIM, 1), const),
        ],
        out_specs=[
            pl.BlockSpec((tm, d), lambda i: (i, 0)),
            pl.BlockSpec((d, tm), lambda i: (0, i)),
            pl.BlockSpec((tm, d), lambda i: (i, 0)),
        ],
        out_shape=[
            jax.ShapeDtypeStruct((t, d), BF16),
            jax.ShapeDtypeStruct((d, t), BF16),
            jax.ShapeDtypeStruct((t, d), BF16),
        ],
        compiler_params=_params("parallel"),
        name="qkv_proj",
    )(x2d, g_q, g_kv, w_q, w_kt, w_v, k_norm_col)


def _group_mean_sq(x, ones_bd, group):
    sq = x * x
    hi = sq.astype(BF16)
    lo = (sq - hi.astype(F32)).astype(BF16)
    return (_dot(hi, ones_bd) + _dot(lo, ones_bd)) * (1.0 / group)


def _attn_kernel(bound_ref, q_ref, kt_ref, v_ref, qn_ref, lam_ref, sn_ref, wgu_ref, wdn_ref,
                 o_ref, wgu_bf_ref, wdn_bf_ref, v1_ref, qn_s, *, seq, tq, lam_init):
    wgu_bf_ref[...] = wgu_ref[...].astype(BF16)
    wdn_bf_ref[...] = wdn_ref[...].astype(BF16)
    lp = lam_ref[...]
    lam = (jnp.exp(jnp.sum(lp[0:1] * lp[1:2], axis=-1, keepdims=True))
           - jnp.exp(jnp.sum(lp[2:3] * lp[3:4], axis=-1, keepdims=True)) + lam_init)
    row = lax.broadcasted_iota(jnp.int32, (tq, tq), 0)
    col = lax.broadcasted_iota(jnp.int32, (tq, tq), 1)
    causal = col <= row
    gi = lax.broadcasted_iota(jnp.int32, (V_DIM, V_DIM), 0) // HEAD_DIM
    gj = lax.broadcasted_iota(jnp.int32, (V_DIM, V_DIM), 1) // HEAD_DIM
    ones_bd = jnp.where(gi == gj, 1.0, 0.0).astype(BF16)
    v1_ref[:, 0:V_DIM] = v_ref[...]
    v1_ref[:, V_DIM:] = jnp.ones((seq, V_DIM), BF16)
    q = q_ref[...].astype(F32)
    ms = _group_mean_sq(q, ones_bd, HEAD_DIM)
    qn_s[...] = (q * lax.rsqrt(ms + EPS) * (qn_ref[...] * (HEAD_DIM ** -0.5 * LOG2E))).astype(BF16)
    maps = [slice(c * HEAD_DIM, (c + 1) * HEAD_DIM) for c in range(2)]

    def finish(q0, r1, r2):
        o = r1[:, 0:V_DIM] / r1[:, V_DIM:] - lam * (r2[:, 0:V_DIM] / r2[:, V_DIM:])
        o = _rms_scale(o) * sn_ref[...] * (1.0 - lam_init)
        o_ref[pl.ds(q0, tq), :] = o.astype(BF16)

    @pl.when(bound_ref[0] <= SAFE_SHIFT)
    def _fixed_shift():
        shift = bound_ref[0] * LOG2E
        for qi in range(seq // tq):
            q0 = qi * tq
            res = []
            for sl in maps:
                qc = qn_s[q0:q0 + tq, sl]
                s = _dot(qc, kt_ref[sl, q0:q0 + tq])
                p = jnp.where(causal, jnp.exp2(s - shift), 0.0).astype(BF16)
                r = _dot(p, v1_ref[q0:q0 + tq, :])
                if q0 > 0:
                    s = _dot(qc, kt_ref[sl, 0:q0])
                    r = r + _dot(jnp.exp2(s - shift).astype(BF16), v1_ref[0:q0, :])
                res.append(r)
            finish(q0, *res)

    @pl.when(jnp.logical_not(bound_ref[0] <= SAFE_SHIFT))
    def _running_max():
        def q_body(qi, carry):
            q0 = pl.multiple_of(qi * tq, tq)
            qs = [qn_s[pl.ds(q0, tq), sl] for sl in maps]

            def tile(j, state, masked):
                k0 = pl.multiple_of(j * tq, tq)
                out = []
                for c, sl in enumerate(maps):
                    m, r = state[c]
                    s = _dot(qs[c], kt_ref[sl, pl.ds(k0, tq)])
                    if masked:
                        s = jnp.where(causal, s, -jnp.inf)
                    m_new = jnp.maximum(m, jnp.max(s, axis=-1, keepdims=True))
                    p = jnp.exp2(s - m_new).astype(BF16)
                    r = jnp.exp2(m - m_new) * r + _dot(p, v1_ref[pl.ds(k0, tq), :])
                    out.append((m_new, r))
                return tuple(out)

            one = (jnp.full((tq, 1), -jnp.inf, F32), jnp.zeros((tq, 2 * V_DIM), F32))
            state = lax.fori_loop(0, qi, lambda j, st: tile(j, st, False), (one, one))
            (_, r1), (_, r2) = tile(qi, state, True)
            finish(q0, r1, r2)
            return carry

        lax.fori_loop(0, seq // tq, q_body, 0)


def _attention(score_bound, q, kt, v, q_norm2, lam_params, sub_norm, w_gu, w_down, batch, seq, lam_init):
    t, d = q.shape
    n_heads = d // V_DIM
    steps = batch * n_heads
    wgu2 = w_gu.reshape(-1, w_gu.shape[-1])
    wdn2 = w_down.reshape(-1, w_down.shape[-1])
    gu_rows, dn_rows = wgu2.shape[0] // steps, wdn2.shape[0] // steps
    assert wgu2.shape[0] % steps == 0 and wdn2.shape[0] % steps == 0 and gu_rows % 16 == 0 and dn_rows % 16 == 0
    slab = lambda b, h: (b * n_heads + h, 0)
    kern = functools.partial(_attn_kernel, seq=seq, tq=ATTN_TILE, lam_init=lam_init)
    o, wgu_bf, wdn_bf = pl.pallas_call(
        kern,
        grid=(batch, n_heads),
        in_specs=[
            pl.BlockSpec(memory_space=pltpu.SMEM),
            pl.BlockSpec((seq, V_DIM), lambda b, h: (b, h)),
            pl.BlockSpec((V_DIM, seq), lambda b, h: (h, b)),
            pl.BlockSpec((seq, V_DIM), lambda b, h: (b, h)),
            pl.BlockSpec((1, V_DIM), lambda b, h: (0, 0)),
            pl.BlockSpec((4, HEAD_DIM), lambda b, h: (0, 0)),
            pl.BlockSpec((1, V_DIM), lambda b, h: (0, 0)),
            pl.BlockSpec((gu_rows, wgu2.shape[1]), slab),
            pl.BlockSpec((dn_rows, wdn2.shape[1]), slab),
        ],
        out_specs=[
            pl.BlockSpec((seq, V_DIM), lambda b, h: (b, h)),
            pl.BlockSpec((gu_rows, wgu2.shape[1]), slab),
            pl.BlockSpec((dn_rows, wdn2.shape[1]), slab),
        ],
        out_shape=[
            jax.ShapeDtypeStruct((t, d), BF16),
            jax.ShapeDtypeStruct(wgu2.shape, BF16),
            jax.ShapeDtypeStruct(wdn2.shape, BF16),
        ],
        scratch_shapes=[pltpu.VMEM((seq, 2 * V_DIM), BF16), pltpu.VMEM((seq, V_DIM), BF16)],
        compiler_params=_params("parallel", "parallel"),
        name="diff_attn",
    )(score_bound, q, kt, v, q_norm2, lam_params, sub_norm, wgu2, wdn2)
    return o, wgu_bf.reshape(w_gu.shape), wdn_bf.reshape(w_down.shape)


def _oproj_router_kernel(x_ref, o_ref, wo_ref, g_ref, rw_ref,
                         x3_ref, h8_ref, topi_ref, gate_ref, rank_ref, cnt_ref, carry, lower,
                         *, tm, d, n_experts):
    i = pl.program_id(0)

    @pl.when(i == 0)
    def _():
        carry[...] = jnp.zeros_like(carry)
        r = lax.broadcasted_iota(jnp.int32, (tm, tm), 0)
        cidx = lax.broadcasted_iota(jnp.int32, (tm, tm), 1)
        lower[...] = jnp.where(cidx < r, 1.0, 0.0).astype(BF16)

    x3 = x_ref[...] + _dot(o_ref[...], wo_ref[...])
    x3_ref[...] = x3
    h = _rms_scale(x3) * g_ref[...]
    for c in range(d // LANES):
        h8_ref[pl.ds(c, tm, stride=d // LANES), :] = h[:, c * LANES:(c + 1) * LANES]

    rw = rw_ref[...]
    h_hi = h.astype(BF16)
    h_lo = (h - h_hi.astype(F32)).astype(BF16)
    w_hi = rw.astype(BF16)
    w_lo = (rw - w_hi.astype(F32)).astype(BF16)
    both = _dot(h_hi, jnp.concatenate([w_hi, w_lo], axis=1))
    logits = both[:, 0:LANES] + (both[:, LANES:] + _dot(h_lo, w_hi))

    lane = lax.broadcasted_iota(jnp.int32, (tm, LANES), 1)
    neg = -jnp.inf
    l1 = jnp.where(lane < n_experts, logits, neg)
    m1 = jnp.max(l1, axis=-1, keepdims=True)
    i1 = jnp.min(jnp.where(l1 == m1, lane, LANES), axis=-1, keepdims=True)
    l2 = jnp.where(lane == i1, neg, l1)
    m2 = jnp.max(l2, axis=-1, keepdims=True)
    i2 = jnp.min(jnp.where(l2 == m2, lane, LANES), axis=-1, keepdims=True)
    e2 = jnp.exp(m2 - m1)
    den = 1.0 + e2
    first = lax.broadcasted_iota(jnp.int32, (tm, TOP_K), 1) == 0
    topi_ref[...] = jnp.where(first, i1, i2)
    gate_ref[...] = jnp.where(first, 1.0 / den, e2 / den)

    oh1 = lane == i1
    oh2 = lane == i2
    cnt = jnp.where(oh1, 1.0, 0.0) + jnp.where(oh2, 1.0, 0.0)
    before = _dot(lower[...], cnt.astype(BF16)) + carry[...]
    r1 = jnp.sum(jnp.where(oh1, before, 0.0), axis=-1, keepdims=True)
    r2 = jnp.sum(jnp.where(oh2, before, 0.0), axis=-1, keepdims=True)
    rank_ref[...] = jnp.where(first, r1, r2).astype(jnp.int32)
    total = carry[...] + jnp.sum(cnt, axis=0, keepdims=True)
    carry[...] = total
    cnt_ref[...] = total.astype(jnp.int32)


def _oproj_router(x2d, o, w_o, g, rw_pad, n_experts):
    t, d = x2d.shape
    tm = ROW_TILE
    const = lambda i: (0, 0)
    kern = functools.partial(_oproj_router_kernel, tm=tm, d=d, n_experts=n_experts)
    return pl.pallas_call(
        kern,
        grid=(t // tm,),
        in_specs=[
            pl.BlockSpec((tm, d), lambda i: (i, 0)),
            pl.BlockSpec((tm, d), lambda i: (i, 0)),
            pl.BlockSpec((d, d), const),
            pl.BlockSpec((1, d), const),
            pl.BlockSpec((d, LANES), const),
        ],
        out_specs=[
            pl.BlockSpec((tm, d), lambda i: (i, 0)),
            pl.BlockSpec((tm * (d // LANES), LANES), lambda i: (i, 0)),
            pl.BlockSpec((tm, TOP_K), lambda i: (i, 0)),
            pl.BlockSpec((tm, TOP_K), lambda i: (i, 0)),
            pl.BlockSpec((tm, TOP_K), lambda i: (i, 0)),
            pl.BlockSpec((1, LANES), const),
        ],
        out_shape=[
            jax.ShapeDtypeStruct((t, d), F32),
            jax.ShapeDtypeStruct((t * (d // LANES), LANES), F32),
            jax.ShapeDtypeStruct((t, TOP_K), jnp.int32),
            jax.ShapeDtypeStruct((t, TOP_K), F32),
            jax.ShapeDtypeStruct((t, TOP_K), jnp.int32),
            jax.ShapeDtypeStruct((1, LANES), jnp.int32),
        ],
        scratch_shapes=[pltpu.VMEM((1, LANES), F32), pltpu.VMEM((tm, tm), BF16)],
        compiler_params=_params("arbitrary"),
        name="oproj_router",
    )(x2d, o, w_o, g, rw_pad)


def _pad_bits(tile):
    return [1 << b for b in reversed(range(int(math.log2(tile))))]


def _dispatch_kernel(s0_ref, s1_ref, pad_start_ref, pad_len_ref, h_ref, hs_hbm, zeros, sem, zsem,
                     *, tb, nc, n_experts, tile):
    i = pl.program_id(0)
    base = i * tb

    def rows(ref, first, count):
        start = first * nc
        if not isinstance(start, int):
            start = pl.multiple_of(start, nc)
        return ref.at[pl.ds(start, count * nc), :]

    def row_copy(r, slot):
        return pltpu.make_async_copy(rows(h_ref, r, 1), rows(hs_hbm, slot, 1), sem)

    def issue(g, carry):
        for u in range(DMA_UNROLL):
            r = g * DMA_UNROLL + u
            row_copy(r, s0_ref[base + r]).start()
            row_copy(r, s1_ref[base + r]).start()
        return carry

    lax.fori_loop(0, tb // DMA_UNROLL, issue, 0)

    def pad_copies(do):
        for e in range(n_experts):
            n = pad_len_ref[e]
            pos = pad_start_ref[e]
            for bit in _pad_bits(tile):
                @pl.when((n & bit) != 0)
                def _(pos=pos, bit=bit):
                    do(pltpu.make_async_copy(rows(zeros, 0, bit), rows(hs_hbm, pos, bit), zsem))
                pos = pos + (n & bit)
        for k in range(n_experts):
            @pl.when(k < pad_len_ref[n_experts])
            def _(k=k):
                pos = pad_start_ref[n_experts] + k * tile
                do(pltpu.make_async_copy(zeros, rows(hs_hbm, pos, tile), zsem))

    @pl.when(i == 0)
    def _():
        zeros[...] = jnp.zeros_like(zeros)
        pad_copies(lambda cp: cp.start())
        pad_copies(lambda cp: cp.wait())

    def drain(g, carry):
        for u in range(2 * DMA_UNROLL):
            row_copy(0, 0).wait()
        return carry

    lax.fori_loop(0, tb // DMA_UNROLL, drain, 0)


def _dispatch(slot0, slot1, pad_start, pad_len, h8, nc, n_slots, n_experts):
    t = h8.shape[0] // nc
    tb = DISPATCH_BLOCK
    kern = functools.partial(_dispatch_kernel, tb=tb, nc=nc, n_experts=n_experts, tile=GROUP_TILE)
    grid_spec = pltpu.PrefetchScalarGridSpec(
        num_scalar_prefetch=4,
        grid=(t // tb,),
        in_specs=[pl.BlockSpec((tb * nc, LANES), lambda i, *_: (i, 0))],
        out_specs=pl.BlockSpec(memory_space=pl.ANY),
        scratch_shapes=[
            pltpu.VMEM((GROUP_TILE * nc, LANES), F32),
            pltpu.SemaphoreType.DMA,
            pltpu.SemaphoreType.DMA,
        ],
    )
    return pl.pallas_call(
        kern,
        grid_spec=grid_spec,
        out_shape=jax.ShapeDtypeStruct((n_slots * nc, LANES), F32),
        compiler_params=_params("arbitrary"),
        name="moe_dispatch",
    )(slot0, slot1, pad_start, pad_len, h8)


def _group_chunks(tf):
    step = 512
    return [(lo, min(lo + step, tf)) for lo in range(0, tf, step)]


def _moe_group_kernel(te_ref, nt_ref, hs_ref, wg_ref, wu_ref, wd_ref, ys_ref, hb, acc, *, tm, d, tf, nf):
    j = pl.program_id(0)
    f = pl.program_id(1)
    nc = d // LANES

    @pl.when(j >= nt_ref[0])
    def _():
        ys_ref[...] = jnp.zeros_like(ys_ref)

    @pl.when(j < nt_ref[0])
    def _():
        @pl.when(f == 0)
        def _():
            for c in range(nc):
                hb[:, c * LANES:(c + 1) * LANES] = hs_ref[pl.ds(c, tm, stride=nc), :].astype(BF16)

        h = hb[...]
        part = None
        for lo, hi in _group_chunks(tf):
            gate = _dot(h, wg_ref[:, lo:hi])
            up = _dot(h, wu_ref[:, lo:hi])
            a = (gate * jax.nn.sigmoid(gate) * up).astype(BF16)
            p = _dot(a, wd_ref[lo:hi, :])
            part = p if part is None else part + p

        @pl.when(f == 0)
        def _():
            acc[...] = part

        @pl.when(f != 0)
        def _():
            acc[...] = acc[...] + part

        @pl.when(f == nf - 1)
        def _():
            y = acc[...]
            for c in range(nc):
                ys_ref[pl.ds(c, tm, stride=nc), :] = y[:, c * LANES:(c + 1) * LANES]


def _moe_group(tile_expert, n_tiles, hs2d, w_gu, w_down, max_tiles):
    n_e, d, f2 = w_gu.shape
    f_exp = f2 // 2
    tm, tf = GROUP_TILE, MOE_F_TILE
    nf = f_exp // tf
    nc = d // LANES

    def row_map(j, f, te, nt):
        return (jnp.minimum(j, nt[0] - 1), 0)

    kern = functools.partial(_moe_group_kernel, tm=tm, d=d, tf=tf, nf=nf)
    grid_spec = pltpu.PrefetchScalarGridSpec(
        num_scalar_prefetch=2,
        grid=(max_tiles, nf),
        in_specs=[
            pl.BlockSpec((tm * nc, LANES), row_map),
            pl.BlockSpec((None, d, tf), lambda j, f, te, nt: (te[j], 0, f)),
            pl.BlockSpec((None, d, tf), lambda j, f, te, nt: (te[j], 0, nf + f)),
            pl.BlockSpec((None, tf, d), lambda j, f, te, nt: (te[j], f, 0)),
        ],
        out_specs=pl.BlockSpec((tm * nc, LANES), lambda j, f, te, nt: (j, 0)),
        scratch_shapes=[pltpu.VMEM((tm, d), BF16), pltpu.VMEM((tm, d), F32)],
    )
    return pl.pallas_call(
        kern,
        grid_spec=grid_spec,
        out_shape=jax.ShapeDtypeStruct(hs2d.shape, F32),
        compiler_params=_params("arbitrary", "arbitrary"),
        name="moe_group",
    )(tile_expert, n_tiles, hs2d, w_gu, w_gu, w_down)


def _combine_kernel(s0_ref, s1_ref, x_ref, gate_ref, ys_hbm, o_ref, bufs, sems, *, tm, d):
    i = pl.program_id(0)
    n = pl.num_programs(0)
    nc = d // LANES

    def row_copy(slot, par, k, r):
        start = r * nc
        if not isinstance(start, int):
            start = pl.multiple_of(start, nc)
        src = ys_hbm.at[pl.ds(pl.multiple_of(slot * nc, nc), nc), :]
        return pltpu.make_async_copy(src, bufs.at[par, k, pl.ds(start, nc), :], sems.at[par])

    def issue(block, par):
        base = block * tm

        def body(g, carry):
            for u in range(DMA_UNROLL):
                r = g * DMA_UNROLL + u
                row_copy(s0_ref[base + r], par, 0, r).start()
                row_copy(s1_ref[base + r], par, 1, r).start()
            return carry

        lax.fori_loop(0, tm // DMA_UNROLL, body, 0)

    def drain(par):
        def body(g, carry):
            for u in range(DMA_UNROLL):
                row_copy(0, par, 0, 0).wait()
                row_copy(0, par, 1, 0).wait()
            return carry

        lax.fori_loop(0, tm // DMA_UNROLL, body, 0)

    @pl.when(i == 0)
    def _():
        issue(0, 0)

    for par in range(2):
        @pl.when(i % 2 == par)
        def _(par=par):
            @pl.when(i + 1 < n)
            def _():
                issue(i + 1, 1 - par)

            drain(par)
            gates = gate_ref[...]
            g0 = gates[:, 0:1]
            g1 = gates[:, 1:2]
            for c in range(nc):
                y0 = bufs[par, 0, pl.ds(c, tm, stride=nc), :]
                y1 = bufs[par, 1, pl.ds(c, tm, stride=nc), :]
                cols = slice(c * LANES, (c + 1) * LANES)
                o_ref[:, cols] = x_ref[:, cols] + (g0 * y0 + g1 * y1)


def _combine(slot0, slot1, x3, gates, ys2d):
    t, d = x3.shape
    tm = ROW_TILE
    nc = d // LANES
    grid_spec = pltpu.PrefetchScalarGridSpec(
        num_scalar_prefetch=2,
        grid=(t // tm,),
        in_specs=[
            pl.BlockSpec((tm, d), lambda i, s0, s1: (i, 0)),
            pl.BlockSpec((tm, TOP_K), lambda i, s0, s1: (i, 0)),
            pl.BlockSpec(memory_space=pl.ANY),
        ],
        out_specs=pl.BlockSpec((tm, d), lambda i, s0, s1: (i, 0)),
        scratch_shapes=[
            pltpu.VMEM((2, TOP_K, tm * nc, LANES), F32),
            pltpu.SemaphoreType.DMA((2,)),
        ],
    )
    return pl.pallas_call(
        functools.partial(_combine_kernel, tm=tm, d=d),
        grid_spec=grid_spec,
        out_shape=jax.ShapeDtypeStruct((t, d), F32),
        compiler_params=_params("arbitrary"),
        name="moe_combine",
    )(slot0, slot1, x3, gates, ys2d)


def _routing_tables(topi, rank, counts, n_experts, max_tiles):
    tile = GROUP_TILE
    padded = ((counts + tile - 1) // tile) * tile
    ends = jnp.cumsum(padded)
    offs = ends - padded
    onehot = topi[..., None] == jnp.arange(n_experts, dtype=jnp.int32)
    slot = jnp.sum(jnp.where(onehot, offs, 0), axis=-1) + rank
    tile_ends = ends // tile
    n_tiles = tile_ends[-1]
    j = jnp.minimum(jnp.arange(max_tiles, dtype=jnp.int32), n_tiles - 1)
    tile_expert = jnp.sum((j[:, None] >= tile_ends[None, :]).astype(jnp.int32), axis=-1)
    pad_start = jnp.concatenate([offs + counts, ends[-1:]]).astype(jnp.int32)
    pad_len = jnp.concatenate([padded - counts, max_tiles - n_tiles.reshape(1)]).astype(jnp.int32)
    return (slot[:, 0], slot[:, 1], pad_start, pad_len,
            tile_expert.astype(jnp.int32), n_tiles.reshape(1).astype(jnp.int32))


def _lambda_init(layer_idx_1based):
    return 0.8 - 0.6 * math.exp(-0.3 * (layer_idx_1based - 1))


def kernel(x, ln_mix, ln_ffn, conv_w_in, conv_w, conv_w_out, ln_kv, w_kv, k_norm, attn_w_q, q_norm, lam_params,
           sub_norm, attn_w_o, ffn_w_gu, ffn_w_down, router_w, moe_w_gu, moe_w_down):
    batch, seq, d = x.shape
    t = batch * seq
    n_experts = router_w.shape[-1]
    assert ln_mix.shape[0] == 2 and conv_w_in.shape[0] == 1 and attn_w_q.shape[0] == 1
    assert seq % ROW_TILE == 0 and seq % ATTN_TILE == 0 and d % LANES == 0
    nc = d // LANES
    x2d = x.reshape(t, d)

    x1 = _mixer_a(x2d, ln_mix[0:1], conv_w_in[0].astype(BF16), conv_w[0], conv_w_out[0].astype(BF16), seq)
    x2 = _ffn_dense(x1, ln_ffn[0:1], ffn_w_gu[0].astype(BF16), ffn_w_down[0].astype(BF16))

    n_k = d
    w_kt = w_kv[:, :n_k].T.astype(BF16)
    w_v = w_kv[:, n_k:].astype(BF16)
    q, kt, v = _qkv(x2, ln_mix[1:2], ln_kv.reshape(1, d), attn_w_q[0].astype(BF16), w_kt, w_v,
                    k_norm.reshape(HEAD_DIM, 1))

    score_bound = (math.sqrt(HEAD_DIM) * jnp.max(jnp.abs(q_norm[0])) * jnp.max(jnp.abs(k_norm))).reshape(1)
    o, moe_gu_bf, moe_down_bf = _attention(score_bound, q, kt, v, jnp.tile(q_norm[0:1], (1, 2)), lam_params[0],
                                           sub_norm[0:1], moe_w_gu[0], moe_w_down[0], batch, seq, _lambda_init(2))

    rw_pad = jnp.pad(router_w[0], ((0, 0), (0, LANES - n_experts)))
    x3, h8, topi, gates, rank, counts = _oproj_router(x2, o, attn_w_o[0].astype(BF16), ln_ffn[1:2], rw_pad,
                                                      n_experts)
    max_tiles = (TOP_K * t) // GROUP_TILE + n_experts
    n_slots = max_tiles * GROUP_TILE
    slot0, slot1, pad_start, pad_len, tile_expert, n_tiles = _routing_tables(
        topi, rank, counts[0, :n_experts], n_experts, max_tiles)
    hs = _dispatch(slot0, slot1, pad_start, pad_len, h8, nc, n_slots, n_experts)
    ys = _moe_group(tile_expert, n_tiles, hs, moe_gu_bf, moe_down_bf, max_tiles)
    out = _combine(slot0, slot1, x3, gates, ys)
    return out.reshape(batch, seq, d)
```

```python
import functools
import math

import jax
import jax.numpy as jnp
from jax import lax
from jax.experimental import pallas as pl
from jax.experimental.pallas import tpu as pltpu

F32 = jnp.float32
BF16 = jnp.bfloat16

EPS = 1e-6
HEAD_DIM = 64
V_DIM = 2 * HEAD_DIM
CONV_WIDTH = 3
TOP_K = 2

LANES = 128
SUBLANES = 8
VMEM_LIMIT = 56 * 1024 * 1024

ROW_TILE = 512
ATTN_TILE = 256
GROUP_TILE = 512
MOE_F_CHUNK = 512
MOE_VMEM_LIMIT = 62 * 1024 * 1024
DISPATCH_BLOCK = 1024
DMA_UNROLL = 16

LOG2E = 1.4426950408889634
SAFE_SHIFT = 40.0


def _rms_scale(x):
    return x * lax.rsqrt(jnp.mean(x * x, axis=-1, keepdims=True) + EPS)


def _dot(a, b):
    return jnp.dot(a, b, preferred_element_type=F32)


def _params(*sem):
    return pltpu.CompilerParams(dimension_semantics=sem, vmem_limit_bytes=VMEM_LIMIT)


def _mixer_a_kernel(x_ref, g_ref, win_ref, cw_ref, wout_ref, o_ref, ubuf, *, tm, d, tiles_per_seq):
    i = pl.program_id(0)
    x = x_ref[...]
    h = (_rms_scale(x) * g_ref[...]).astype(BF16)
    c = _dot(h, win_ref[:, d:2 * d])
    v = _dot(h, win_ref[:, 2 * d:])
    u = c * v

    @pl.when(i % tiles_per_seq == 0)
    def _():
        ubuf[0:SUBLANES, :] = jnp.zeros((SUBLANES, d), F32)

    ubuf[SUBLANES:tm + SUBLANES, :] = u
    u1 = ubuf[SUBLANES - 1:tm + SUBLANES - 1, :]
    u2 = ubuf[SUBLANES - 2:tm + SUBLANES - 2, :]
    cw = cw_ref[...]
    z = u2 * cw[0:1] + u1 * cw[1:2] + u * cw[2:3]
    b = _dot(h, win_ref[:, 0:d])
    y = (b * z).astype(BF16)
    o_ref[...] = x + _dot(y, wout_ref[...])
    ubuf[0:SUBLANES, :] = ubuf[tm:tm + SUBLANES, :]


def _mixer_a(x2d, g, w_in, conv_w, w_out, seq):
    t, d = x2d.shape
    tm = ROW_TILE
    kern = functools.partial(_mixer_a_kernel, tm=tm, d=d, tiles_per_seq=seq // tm)
    return pl.pallas_call(
        kern,
        grid=(t // tm,),
        in_specs=[
            pl.BlockSpec((tm, d), lambda i: (i, 0)),
            pl.BlockSpec((1, d), lambda i: (0, 0)),
            pl.BlockSpec((d, 3 * d), lambda i: (0, 0)),
            pl.BlockSpec((CONV_WIDTH, d), lambda i: (0, 0)),
            pl.BlockSpec((d, d), lambda i: (0, 0)),
        ],
        out_specs=pl.BlockSpec((tm, d), lambda i: (i, 0)),
        out_shape=jax.ShapeDtypeStruct((t, d), F32),
        scratch_shapes=[pltpu.VMEM((tm + SUBLANES, d), F32)],
        compiler_params=_params("arbitrary"),
        name="mixer_a",
    )(x2d, g, w_in, conv_w, w_out)


def _ffn_chunks(f):
    step = 1024
    return [(lo, min(lo + step, f)) for lo in range(0, f, step)]


def _ffn_kernel(x_ref, g_ref, wgu_ref, wd_ref, o_ref, *, f):
    x = x_ref[...]
    h = (_rms_scale(x) * g_ref[...]).astype(BF16)
    acc = x
    for lo, hi in _ffn_chunks(f):
        gate = _dot(h, wgu_ref[:, lo:hi])
        up = _dot(h, wgu_ref[:, f + lo:f + hi])
        a = (gate * jax.nn.sigmoid(gate) * up).astype(BF16)
        acc = acc + _dot(a, wd_ref[lo:hi, :])
    o_ref[...] = acc


def _ffn_dense(x2d, g, w_gu, w_down):
    t, d = x2d.shape
    f = w_down.shape[0]
    tm = ROW_TILE
    return pl.pallas_call(
        functools.partial(_ffn_kernel, f=f),
        grid=(t // tm,),
        in_specs=[
            pl.BlockSpec((tm, d), lambda i: (i, 0)),
            pl.BlockSpec((1, d), lambda i: (0, 0)),
            pl.BlockSpec((d, 2 * f), lambda i: (0, 0)),
            pl.BlockSpec((f, d), lambda i: (0, 0)),
        ],
        out_specs=pl.BlockSpec((tm, d), lambda i: (i, 0)),
        out_shape=jax.ShapeDtypeStruct((t, d), F32),
        compiler_params=_params("parallel"),
        name="ffn_dense",
    )(x2d, g, w_gu, w_down)


def _qkv_kernel(x_ref, gq_ref, gkv_ref, wq_ref, wkt_ref, wv_ref, kn_ref, q_ref, kt_ref, v_ref, *, tm, d):
    y = _rms_scale(x_ref[...])
    hq = (y * gq_ref[...]).astype(BF16)
    hkv = (y * gkv_ref[...]).astype(BF16)
    q_ref[...] = _dot(hq, wq_ref[...]).astype(BF16)
    v_ref[...] = _dot(hkv, wv_ref[...]).astype(BF16)
    kt = lax.dot_general(wkt_ref[...], hkv, (((1,), (1,)), ((), ())), preferred_element_type=F32)
    k3 = kt.reshape(d // HEAD_DIM, HEAD_DIM, tm)
    k3 = k3 * lax.rsqrt(jnp.mean(k3 * k3, axis=1, keepdims=True) + EPS) * kn_ref[...][None]
    kt_ref[...] = k3.reshape(d, tm).astype(BF16)


def _qkv(x2d, g_q, g_kv, w_q, w_kt, w_v, k_norm_col):
    t, d = x2d.shape
    tm = ROW_TILE
    const = lambda i: (0, 0)
    return pl.pallas_call(
        functools.partial(_qkv_kernel, tm=tm, d=d),
        grid=(t // tm,),
        in_specs=[
            pl.BlockSpec((tm, d), lambda i: (i, 0)),
            pl.BlockSpec((1, d), const),
            pl.BlockSpec((1, d), const),
            pl.BlockSpec((d, d), const),
            pl.BlockSpec((d, d), const),
            pl.BlockSpec((d, d), const),
            pl.BlockSpec((HEAD_DIM, 1), const),
        ],
        out_specs=[
            pl.BlockSpec((tm, d), lambda i: (i, 0)),
            pl.BlockSpec((d, tm), lambda i: (0, i)),
            pl.BlockSpec((tm, d), lambda i: (i, 0)),
        ],
        out_shape=[
            jax.ShapeDtypeStruct((t, d), BF16),
            jax.ShapeDtypeStruct((d, t), BF16),
            jax.ShapeDtypeStruct((t, d), BF16),
        ],
        compiler_params=_params("parallel"),
        name="qkv_proj",
    )(x2d, g_q, g_kv, w_q, w_kt, w_v, k_norm_col)


def _group_mean_sq(x, ones_bd, group):
    sq = x * x
    hi = sq.astype(BF16)
    lo = (sq - hi.astype(F32)).astype(BF16)
    return (_dot(hi, ones_bd) + _dot(lo, ones_bd)) * (1.0 / group)


def _attn_kernel(bound_ref, q_ref, kt_ref, v_ref, qn_ref, lam_ref, sn_ref, wgu_ref, wdn_ref,
                 o_ref, wgu_bf_ref, wdn_bf_ref, v1_ref, qn_s, *, seq, tq, lam_init):
    wgu_bf_ref[...] = wgu_ref[...].astype(BF16)
    wdn_bf_ref[...] = wdn_ref[...].astype(BF16)
    lp = lam_ref[...]
    lam = (jnp.exp(jnp.sum(lp[0:1] * lp[1:2], axis=-1, keepdims=True))
           - jnp.exp(jnp.sum(lp[2:3] * lp[3:4], axis=-1, keepdims=True)) + lam_init)
    row = lax.broadcasted_iota(jnp.int32, (tq, tq), 0)
    col = lax.broadcasted_iota(jnp.int32, (tq, tq), 1)
    causal = col <= row
    gi = lax.broadcasted_iota(jnp.int32, (V_DIM, V_DIM), 0) // HEAD_DIM
    gj = lax.broadcasted_iota(jnp.int32, (V_DIM, V_DIM), 1) // HEAD_DIM
    ones_bd = jnp.where(gi == gj, 1.0, 0.0).astype(BF16)
    v1_ref[:, 0:V_DIM] = v_ref[...]
    v1_ref[:, V_DIM:] = jnp.ones((seq, V_DIM), BF16)
    q = q_ref[...].astype(F32)
    ms = _group_mean_sq(q, ones_bd, HEAD_DIM)
    qn_s[...] = (q * lax.rsqrt(ms + EPS) * (qn_ref[...] * (HEAD_DIM ** -0.5 * LOG2E))).astype(BF16)
    maps = [slice(c * HEAD_DIM, (c + 1) * HEAD_DIM) for c in range(2)]

    def finish(q0, r1, r2):
        o = r1[:, 0:V_DIM] / r1[:, V_DIM:] - lam * (r2[:, 0:V_DIM] / r2[:, V_DIM:])
        o = _rms_scale(o) * sn_ref[...] * (1.0 - lam_init)
        o_ref[pl.ds(q0, tq), :] = o.astype(BF16)

    @pl.when(bound_ref[0] <= SAFE_SHIFT)
    def _fixed_shift():
        shift = bound_ref[0] * LOG2E
        for qi in range(seq // tq):
            q0 = qi * tq
            res = []
            for sl in maps:
                qc = qn_s[q0:q0 + tq, sl]
                s = _dot(qc, kt_ref[sl, q0:q0 + tq])
                p = jnp.where(causal, jnp.exp2(s - shift), 0.0).astype(BF16)
                r = _dot(p, v1_ref[q0:q0 + tq, :])
                if q0 > 0:
                    s = _dot(qc, kt_ref[sl, 0:q0])
                    r = r + _dot(jnp.exp2(s - shift).astype(BF16), v1_ref[0:q0, :])
                res.append(r)
            finish(q0, *res)

    @pl.when(jnp.logical_not(bound_ref[0] <= SAFE_SHIFT))
    def _running_max():
        def q_body(qi, carry):
            q0 = pl.multiple_of(qi * tq, tq)
            qs = [qn_s[pl.ds(q0, tq), sl] for sl in maps]

            def tile(j, state, masked):
                k0 = pl.multiple_of(j * tq, tq)
                out = []
                for c, sl in enumerate(maps):
                    m, r = state[c]
                    s = _dot(qs[c], kt_ref[sl, pl.ds(k0, tq)])
                    if masked:
                        s = jnp.where(causal, s, -jnp.inf)
                    m_new = jnp.maximum(m, jnp.max(s, axis=-1, keepdims=True))
                    p = jnp.exp2(s - m_new).astype(BF16)
                    r = jnp.exp2(m - m_new) * r + _dot(p, v1_ref[pl.ds(k0, tq), :])
                    out.append((m_new, r))
                return tuple(out)

            one = (jnp.full((tq, 1), -jnp.inf, F32), jnp.zeros((tq, 2 * V_DIM), F32))
            state = lax.fori_loop(0, qi, lambda j, st: tile(j, st, False), (one, one))
            (_, r1), (_, r2) = tile(qi, state, True)
            finish(q0, r1, r2)
            return carry

        lax.fori_loop(0, seq // tq, q_body, 0)


def _attention(score_bound, q, kt, v, q_norm2, lam_params, sub_norm, w_gu, w_down, batch, seq, lam_init):
    t, d = q.shape
    n_heads = d // V_DIM
    steps = batch * n_heads
    wgu2 = w_gu.reshape(-1, w_gu.shape[-1])
    wdn2 = w_down.reshape(-1, w_down.shape[-1])
    gu_rows, dn_rows = wgu2.shape[0] // steps, wdn2.shape[0] // steps
    assert wgu2.shape[0] % steps == 0 and wdn2.shape[0] % steps == 0 and gu_rows % 16 == 0 and dn_rows % 16 == 0
    slab = lambda b, h: (b * n_heads + h, 0)
    kern = functools.partial(_attn_kernel, seq=seq, tq=ATTN_TILE, lam_init=lam_init)
    o, wgu_bf, wdn_bf = pl.pallas_call(
        kern,
        grid=(batch, n_heads),
        in_specs=[
            pl.BlockSpec(memory_space=pltpu.SMEM),
            pl.BlockSpec((seq, V_DIM), lambda b, h: (b, h)),
            pl.BlockSpec((V_DIM, seq), lambda b, h: (h, b)),
            pl.BlockSpec((seq, V_DIM), lambda b, h: (b, h)),
            pl.BlockSpec((1, V_DIM), lambda b, h: (0, 0)),
            pl.BlockSpec((4, HEAD_DIM), lambda b, h: (0, 0)),
            pl.BlockSpec((1, V_DIM), lambda b, h: (0, 0)),
            pl.BlockSpec((gu_rows, wgu2.shape[1]), slab),
            pl.BlockSpec((dn_rows, wdn2.shape[1]), slab),
        ],
        out_specs=[
            pl.BlockSpec((seq, V_DIM), lambda b, h: (b, h)),
            pl.BlockSpec((gu_rows, wgu2.shape[1]), slab),
            pl.BlockSpec((dn_rows, wdn2.shape[1]), slab),
        ],
        out_shape=[
            jax.ShapeDtypeStruct((t, d), BF16),
            jax.ShapeDtypeStruct(wgu2.shape, BF16),
            jax.ShapeDtypeStruct(wdn2.shape, BF16),
        ],
        scratch_shapes=[pltpu.VMEM((seq, 2 * V_DIM), BF16), pltpu.VMEM((seq, V_DIM), BF16)],
        compiler_params=_params("parallel", "parallel"),
        name="diff_attn",
    )(score_bound, q, kt, v, q_norm2, lam_params, sub_norm, wgu2, wdn2)
    return o, wgu_bf.reshape(w_gu.shape), wdn_bf.reshape(w_down.shape)


def _oproj_router_kernel(x_ref, o_ref, wo_ref, g_ref, rw_ref,
                         x3_ref, h8_ref, topi_ref, gate_ref, rank_ref, cnt_ref, carry, lower,
                         *, tm, d, n_experts):
    i = pl.program_id(0)

    @pl.when(i == 0)
    def _():
        carry[...] = jnp.zeros_like(carry)
        r = lax.broadcasted_iota(jnp.int32, (tm, tm), 0)
        cidx = lax.broadcasted_iota(jnp.int32, (tm, tm), 1)
        lower[...] = jnp.where(cidx < r, 1.0, 0.0).astype(BF16)

    x3 = x_ref[...] + _dot(o_ref[...], wo_ref[...])
    x3_ref[...] = x3
    h = _rms_scale(x3) * g_ref[...]
    for c in range(d // LANES):
        h8_ref[pl.ds(c, tm, stride=d // LANES), :] = h[:, c * LANES:(c + 1) * LANES]

    rw = rw_ref[...]
    h_hi = h.astype(BF16)
    h_lo = (h - h_hi.astype(F32)).astype(BF16)
    w_hi = rw.astype(BF16)
    w_lo = (rw - w_hi.astype(F32)).astype(BF16)
    both = _dot(h_hi, jnp.concatenate([w_hi, w_lo], axis=1))
    logits = both[:, 0:LANES] + (both[:, LANES:] + _dot(h_lo, w_hi))

    lane = lax.broadcasted_iota(jnp.int32, (tm, LANES), 1)
    neg = -jnp.inf
    l1 = jnp.where(lane < n_experts, logits, neg)
    m1 = jnp.max(l1, axis=-1, keepdims=True)
    i1 = jnp.min(jnp.where(l1 == m1, lane, LANES), axis=-1, keepdims=True)
    l2 = jnp.where(lane == i1, neg, l1)
    m2 = jnp.max(l2, axis=-1, keepdims=True)
    i2 = jnp.min(jnp.where(l2 == m2, lane, LANES), axis=-1, keepdims=True)
    e2 = jnp.exp(m2 - m1)
    den = 1.0 + e2
    first = lax.broadcasted_iota(jnp.int32, (tm, TOP_K), 1) == 0
    topi_ref[...] = jnp.where(first, i1, i2)
    gate_ref[...] = jnp.where(first, 1.0 / den, e2 / den)

    oh1 = lane == i1
    oh2 = lane == i2
    cnt = jnp.where(oh1, 1.0, 0.0) + jnp.where(oh2, 1.0, 0.0)
    before = _dot(lower[...], cnt.astype(BF16)) + carry[...]
    r1 = jnp.sum(jnp.where(oh1, before, 0.0), axis=-1, keepdims=True)
    r2 = jnp.sum(jnp.where(oh2, before, 0.0), axis=-1, keepdims=True)
    rank_ref[...] = jnp.where(first, r1, r2).astype(jnp.int32)
    total = carry[...] + jnp.sum(cnt, axis=0, keepdims=True)
    carry[...] = total
    cnt_ref[...] = total.astype(jnp.int32)


def _oproj_router(x2d, o, w_o, g, rw_pad, n_experts):
    t, d = x2d.shape
    tm = ROW_TILE
    const = lambda i: (0, 0)
    kern = functools.partial(_oproj_router_kernel, tm=tm, d=d, n_experts=n_experts)
    return pl.pallas_call(
        kern,
        grid=(t // tm,),
        in_specs=[
            pl.BlockSpec((tm, d), lambda i: (i, 0)),
            pl.BlockSpec((tm, d), lambda i: (i, 0)),
            pl.BlockSpec((d, d), const),
            pl.BlockSpec((1, d), const),
            pl.BlockSpec((d, LANES), const),
        ],
        out_specs=[
            pl.BlockSpec((tm, d), lambda i: (i, 0)),
            pl.BlockSpec((tm * (d // LANES), LANES), lambda i: (i, 0)),
            pl.BlockSpec((tm, TOP_K), lambda i: (i, 0)),
            pl.BlockSpec((tm, TOP_K), lambda i: (i, 0)),
            pl.BlockSpec((tm, TOP_K), lambda i: (i, 0)),
            pl.BlockSpec((1, LANES), const),
        ],
        out_shape=[
            jax.ShapeDtypeStruct((t, d), F32),
            jax.ShapeDtypeStruct((t * (d // LANES), LANES), F32),
            jax.ShapeDtypeStruct((t, TOP_K), jnp.int32),
            jax.ShapeDtypeStruct((t, TOP_K), F32),
            jax.ShapeDtypeStruct((t, TOP_K), jnp.int32),
            jax.ShapeDtypeStruct((1, LANES), jnp.int32),
        ],
        scratch_shapes=[pltpu.VMEM((1, LANES), F32), pltpu.VMEM((tm, tm), BF16)],
        compiler_params=_params("arbitrary"),
        name="oproj_router",
    )(x2d, o, w_o, g, rw_pad)


def _pad_bits(tile):
    return [1 << b for b in reversed(range(int(math.log2(tile))))]


def _dispatch_kernel(s0_ref, s1_ref, pad_start_ref, pad_len_ref, h_ref, hs_hbm, zeros, sem, zsem,
                     *, tb, nc, n_experts, tile):
    i = pl.program_id(0)
    base = i * tb

    def rows(ref, first, count):
        start = first * nc
        if not isinstance(start, int):
            start = pl.multiple_of(start, nc)
        return ref.at[pl.ds(start, count * nc), :]

    def row_copy(r, slot):
        return pltpu.make_async_copy(rows(h_ref, r, 1), rows(hs_hbm, slot, 1), sem)

    def issue(g, carry):
        for u in range(DMA_UNROLL):
            r = g * DMA_UNROLL + u
            row_copy(r, s0_ref[base + r]).start(priority=0)
            row_copy(r, s1_ref[base + r]).start(priority=1)
        return carry

    lax.fori_loop(0, tb // DMA_UNROLL, issue, 0)

    def pad_copies(do):
        for e in range(n_experts):
            n = pad_len_ref[e]
            pos = pad_start_ref[e]
            for bit in _pad_bits(tile):
                @pl.when((n & bit) != 0)
                def _(pos=pos, bit=bit):
                    do(pltpu.make_async_copy(rows(zeros, 0, bit), rows(hs_hbm, pos, bit), zsem))
                pos = pos + (n & bit)
        for k in range(n_experts):
            @pl.when(k < pad_len_ref[n_experts])
            def _(k=k):
                pos = pad_start_ref[n_experts] + k * tile
                do(pltpu.make_async_copy(zeros, rows(hs_hbm, pos, tile), zsem))

    @pl.when(i == 0)
    def _():
        zeros[...] = jnp.zeros_like(zeros)
        pad_copies(lambda cp: cp.start())
        pad_copies(lambda cp: cp.wait())

    def drain(g, carry):
        for u in range(2 * DMA_UNROLL):
            row_copy(0, 0).wait()
        return carry

    lax.fori_loop(0, tb // DMA_UNROLL, drain, 0)


def _dispatch(slot0, slot1, pad_start, pad_len, h8, nc, n_slots, n_experts):
    t = h8.shape[0] // nc
    tb = DISPATCH_BLOCK
    kern = functools.partial(_dispatch_kernel, tb=tb, nc=nc, n_experts=n_experts, tile=GROUP_TILE)
    grid_spec = pltpu.PrefetchScalarGridSpec(
        num_scalar_prefetch=4,
        grid=(t // tb,),
        in_specs=[pl.BlockSpec((tb * nc, LANES), lambda i, *_: (i, 0))],
        out_specs=pl.BlockSpec(memory_space=pl.ANY),
        scratch_shapes=[
            pltpu.VMEM((GROUP_TILE * nc, LANES), F32),
            pltpu.SemaphoreType.DMA,
            pltpu.SemaphoreType.DMA,
        ],
    )
    return pl.pallas_call(
        kern,
        grid_spec=grid_spec,
        out_shape=jax.ShapeDtypeStruct((n_slots * nc, LANES), F32),
        compiler_params=_params("arbitrary"),
        name="moe_dispatch",
    )(slot0, slot1, pad_start, pad_len, h8)


def _moe_group_kernel(te_ref, nt_ref, hs_ref, wg_ref, wu_ref, wd_ref, ys_ref, *, tm, d, f_exp):
    j = pl.program_id(0)
    nc = d // LANES

    @pl.when(j >= nt_ref[0])
    def _():
        ys_ref[...] = jnp.zeros_like(ys_ref)

    @pl.when(j < nt_ref[0])
    def _():
        h = jnp.concatenate([hs_ref[pl.ds(c, tm, stride=nc), :].astype(BF16) for c in range(nc)], axis=1)
        y = None
        for lo in range(0, f_exp, MOE_F_CHUNK):
            hi = min(lo + MOE_F_CHUNK, f_exp)
            gate = _dot(h, wg_ref[:, lo:hi])
            up = _dot(h, wu_ref[:, lo:hi])
            a = (gate * jax.nn.sigmoid(gate) * up).astype(BF16)
            p = _dot(a, wd_ref[lo:hi, :])
            y = p if y is None else y + p
        for c in range(nc):
            ys_ref[pl.ds(c, tm, stride=nc), :] = y[:, c * LANES:(c + 1) * LANES]


def _moe_group(tile_expert, n_tiles, hs2d, w_gu, w_down, max_tiles):
    n_e, d, f2 = w_gu.shape
    f_exp = f2 // 2
    tm = GROUP_TILE
    nc = d // LANES

    def row_map(j, te, nt):
        return (jnp.minimum(j, nt[0] - 1), 0)

    kern = functools.partial(_moe_group_kernel, tm=tm, d=d, f_exp=f_exp)
    grid_spec = pltpu.PrefetchScalarGridSpec(
        num_scalar_prefetch=2,
        grid=(max_tiles,),
        in_specs=[
            pl.BlockSpec((tm * nc, LANES), row_map),
            pl.BlockSpec((None, d, f_exp), lambda j, te, nt: (te[j], 0, 0)),
            pl.BlockSpec((None, d, f_exp), lambda j, te, nt: (te[j], 0, 1)),
            pl.BlockSpec((None, f_exp, d), lambda j, te, nt: (te[j], 0, 0)),
        ],
        out_specs=pl.BlockSpec((tm * nc, LANES), lambda j, te, nt: (j, 0)),
    )
    return pl.pallas_call(
        kern,
        grid_spec=grid_spec,
        out_shape=jax.ShapeDtypeStruct(hs2d.shape, F32),
        compiler_params=pltpu.CompilerParams(dimension_semantics=("arbitrary",),
                                             vmem_limit_bytes=MOE_VMEM_LIMIT),
        name="moe_group",
    )(tile_expert, n_tiles, hs2d, w_gu, w_gu, w_down)


def _combine_kernel(s0_ref, s1_ref, x_ref, gate_ref, ys_hbm, o_ref, bufs, sems, *, tm, d):
    i = pl.program_id(0)
    n = pl.num_programs(0)
    nc = d // LANES

    def row_copy(slot, par, k, r):
        start = r * nc
        if not isinstance(start, int):
            start = pl.multiple_of(start, nc)
        src = ys_hbm.at[pl.ds(pl.multiple_of(slot * nc, nc), nc), :]
        return pltpu.make_async_copy(src, bufs.at[par, k, pl.ds(start, nc), :], sems.at[par])

    def issue(block, par):
        base = block * tm

        def body(g, carry):
            for u in range(DMA_UNROLL):
                r = g * DMA_UNROLL + u
                row_copy(s0_ref[base + r], par, 0, r).start(priority=0)
                row_copy(s1_ref[base + r], par, 1, r).start(priority=1)
            return carry

        lax.fori_loop(0, tm // DMA_UNROLL, body, 0)

    def drain(par):
        def body(g, carry):
            for u in range(DMA_UNROLL):
                row_copy(0, par, 0, 0).wait()
                row_copy(0, par, 1, 0).wait()
            return carry

        lax.fori_loop(0, tm // DMA_UNROLL, body, 0)

    @pl.when(i == 0)
    def _():
        issue(0, 0)

    for par in range(2):
        @pl.when(i % 2 == par)
        def _(par=par):
            @pl.when(i + 1 < n)
            def _():
                issue(i + 1, 1 - par)

            drain(par)
            gates = gate_ref[...]
            g0 = gates[:, 0:1]
            g1 = gates[:, 1:2]
            for c in range(nc):
                y0 = bufs[par, 0, pl.ds(c, tm, stride=nc), :]
                y1 = bufs[par, 1, pl.ds(c, tm, stride=nc), :]
                cols = slice(c * LANES, (c + 1) * LANES)
                o_ref[:, cols] = x_ref[:, cols] + (g0 * y0 + g1 * y1)


def _combine(slot0, slot1, x3, gates, ys2d):
    t, d = x3.shape
    tm = ROW_TILE
    nc = d // LANES
    grid_spec = pltpu.PrefetchScalarGridSpec(
        num_scalar_prefetch=2,
        grid=(t // tm,),
        in_specs=[
            pl.BlockSpec((tm, d), lambda i, s0, s1: (i, 0)),
            pl.BlockSpec((tm, TOP_K), lambda i, s0, s1: (i, 0)),
            pl.BlockSpec(memory_space=pl.ANY),
        ],
        out_specs=pl.BlockSpec((tm, d), lambda i, s0, s1: (i, 0)),
        scratch_shapes=[
            pltpu.VMEM((2, TOP_K, tm * nc, LANES), F32),
            pltpu.SemaphoreType.DMA((2,)),
        ],
    )
    return pl.pallas_call(
        functools.partial(_combine_kernel, tm=tm, d=d),
        grid_spec=grid_spec,
        out_shape=jax.ShapeDtypeStruct((t, d), F32),
        compiler_params=_params("arbitrary"),
        name="moe_combine",
    )(slot0, slot1, x3, gates, ys2d)


def _routing_tables(topi, rank, counts, n_experts, max_tiles):
    tile = GROUP_TILE
    padded = ((counts + tile - 1) // tile) * tile
    ends = jnp.cumsum(padded)
    offs = ends - padded
    onehot = topi[..., None] == jnp.arange(n_experts, dtype=jnp.int32)
    slot = jnp.sum(jnp.where(onehot, offs, 0), axis=-1) + rank
    tile_ends = ends // tile
    n_tiles = tile_ends[-1]
    j = jnp.minimum(jnp.arange(max_tiles, dtype=jnp.int32), n_tiles - 1)
    tile_expert = jnp.sum((j[:, None] >= tile_ends[None, :]).astype(jnp.int32), axis=-1)
    pad_start = jnp.concatenate([offs + counts, ends[-1:]]).astype(jnp.int32)
    pad_len = jnp.concatenate([padded - counts, max_tiles - n_tiles.reshape(1)]).astype(jnp.int32)
    return (slot[:, 0], slot[:, 1], pad_start, pad_len,
            tile_expert.astype(jnp.int32), n_tiles.reshape(1).astype(jnp.int32))


def _lambda_init(layer_idx_1based):
    return 0.8 - 0.6 * math.exp(-0.3 * (layer_idx_1based - 1))


def kernel(x, ln_mix, ln_ffn, conv_w_in, conv_w, conv_w_out, ln_kv, w_kv, k_norm, attn_w_q, q_norm, lam_params,
           sub_norm, attn_w_o, ffn_w_gu, ffn_w_down, router_w, moe_w_gu, moe_w_down):
    batch, seq, d = x.shape
    t = batch * seq
    n_experts = router_w.shape[-1]
    assert ln_mix.shape[0] == 2 and conv_w_in.shape[0] == 1 and attn_w_q.shape[0] == 1
    assert seq % ROW_TILE == 0 and seq % ATTN_TILE == 0 and d % LANES == 0
    nc = d // LANES
    x2d = x.reshape(t, d)

    x1 = _mixer_a(x2d, ln_mix[0:1], conv_w_in[0].astype(BF16), conv_w[0], conv_w_out[0].astype(BF16), seq)
    x2 = _ffn_dense(x1, ln_ffn[0:1], ffn_w_gu[0].astype(BF16), ffn_w_down[0].astype(BF16))

    n_k = d
    w_kt = w_kv[:, :n_k].T.astype(BF16)
    w_v = w_kv[:, n_k:].astype(BF16)
    q, kt, v = _qkv(x2, ln_mix[1:2], ln_kv.reshape(1, d), attn_w_q[0].astype(BF16), w_kt, w_v,
                    k_norm.reshape(HEAD_DIM, 1))

    score_bound = (math.sqrt(HEAD_DIM) * jnp.max(jnp.abs(q_norm[0])) * jnp.max(jnp.abs(k_norm))).reshape(1)
    o, moe_gu_bf, moe_down_bf = _attention(score_bound, q, kt, v, jnp.tile(q_norm[0:1], (1, 2)), lam_params[0],
                                           sub_norm[0:1], moe_w_gu[0], moe_w_down[0], batch, seq, _lambda_init(2))

    rw_pad = jnp.pad(router_w[0], ((0, 0), (0, LANES - n_experts)))
    x3, h8, topi, gates, rank, counts = _oproj_router(x2, o, attn_w_o[0].astype(BF16), ln_ffn[1:2], rw_pad,
                                                      n_experts)
    max_tiles = (TOP_K * t) // GROUP_TILE + n_experts
    n_slots = max_tiles * GROUP_TILE
    slot0, slot1, pad_start, pad_len, tile_expert, n_tiles = _routing_tables(
        topi, rank, counts[0, :n_experts], n_experts, max_tiles)
    hs = _dispatch(slot0, slot1, pad_start, pad_len, h8, nc, n_slots, n_experts)
    ys = _moe_group(tile_expert, n_tiles, hs, moe_gu_bf, moe_down_bf, max_tiles)
    out = _combine(slot0, slot1, x3, gates, ys)
    return out.reshape(batch, seq, d)
```

```python
import functools
import math

import jax
import jax.numpy as jnp
from jax import lax
from jax.experimental import pallas as pl
from jax.experimental.pallas import tpu as pltpu

F32 = jnp.float32
BF16 = jnp.bfloat16

EPS = 1e-6
HEAD_DIM = 64
V_DIM = 2 * HEAD_DIM
CONV_WIDTH = 3
TOP_K = 2

LANES = 128
SUBLANES = 8
VMEM_LIMIT = 56 * 1024 * 1024

ROW_TILE = 512
ATTN_TILE = 256
ATTN_HEADS_PER_STEP = 2
GROUP_TILE = 512
MOE_F_CHUNK = 512
MOE_VMEM_LIMIT = 62 * 1024 * 1024
DISPATCH_BLOCK = 1024
DMA_UNROLL = 16

LOG2E = 1.4426950408889634
SAFE_SHIFT = 40.0


def _rms_scale(x):
    return x * lax.rsqrt(jnp.mean(x * x, axis=-1, keepdims=True) + EPS)


def _dot(a, b):
    return jnp.dot(a, b, preferred_element_type=F32)


def _params(*sem):
    return pltpu.CompilerParams(dimension_semantics=sem, vmem_limit_bytes=VMEM_LIMIT)


def _mixer_a_kernel(x_ref, g_ref, win_ref, cw_ref, wout_ref, o_ref, ubuf, *, tm, d, tiles_per_seq):
    i = pl.program_id(0)
    x = x_ref[...]
    h = (_rms_scale(x) * g_ref[...]).astype(BF16)
    c = _dot(h, win_ref[:, d:2 * d])
    v = _dot(h, win_ref[:, 2 * d:])
    u = c * v

    @pl.when(i % tiles_per_seq == 0)
    def _():
        ubuf[0:SUBLANES, :] = jnp.zeros((SUBLANES, d), F32)

    ubuf[SUBLANES:tm + SUBLANES, :] = u
    u1 = ubuf[SUBLANES - 1:tm + SUBLANES - 1, :]
    u2 = ubuf[SUBLANES - 2:tm + SUBLANES - 2, :]
    cw = cw_ref[...]
    z = u2 * cw[0:1] + u1 * cw[1:2] + u * cw[2:3]
    b = _dot(h, win_ref[:, 0:d])
    y = (b * z).astype(BF16)
    o_ref[...] = x + _dot(y, wout_ref[...])
    ubuf[0:SUBLANES, :] = ubuf[tm:tm + SUBLANES, :]


def _mixer_a(x2d, g, w_in, conv_w, w_out, seq):
    t, d = x2d.shape
    tm = ROW_TILE
    kern = functools.partial(_mixer_a_kernel, tm=tm, d=d, tiles_per_seq=seq // tm)
    return pl.pallas_call(
        kern,
        grid=(t // tm,),
        in_specs=[
            pl.BlockSpec((tm, d), lambda i: (i, 0)),
            pl.BlockSpec((1, d), lambda i: (0, 0)),
            pl.BlockSpec((d, 3 * d), lambda i: (0, 0)),
            pl.BlockSpec((CONV_WIDTH, d), lambda i: (0, 0)),
            pl.BlockSpec((d, d), lambda i: (0, 0)),
        ],
        out_specs=pl.BlockSpec((tm, d), lambda i: (i, 0)),
        out_shape=jax.ShapeDtypeStruct((t, d), F32),
        scratch_shapes=[pltpu.VMEM((tm + SUBLANES, d), F32)],
        compiler_params=_params("arbitrary"),
        name="mixer_a",
    )(x2d, g, w_in, conv_w, w_out)


def _ffn_chunks(f):
    step = 1024
    return [(lo, min(lo + step, f)) for lo in range(0, f, step)]


def _ffn_kernel(x_ref, g_ref, wgu_ref, wd_ref, o_ref, *, f):
    x = x_ref[...]
    h = (_rms_scale(x) * g_ref[...]).astype(BF16)
    acc = x
    for lo, hi in _ffn_chunks(f):
        gate = _dot(h, wgu_ref[:, lo:hi])
        up = _dot(h, wgu_ref[:, f + lo:f + hi])
        a = (gate * jax.nn.sigmoid(gate) * up).astype(BF16)
        acc = acc + _dot(a, wd_ref[lo:hi, :])
    o_ref[...] = acc


def _ffn_dense(x2d, g, w_gu, w_down):
    t, d = x2d.shape
    f = w_down.shape[0]
    tm = ROW_TILE
    return pl.pallas_call(
        functools.partial(_ffn_kernel, f=f),
        grid=(t // tm,),
        in_specs=[
            pl.BlockSpec((tm, d), lambda i: (i, 0)),
            pl.BlockSpec((1, d), lambda i: (0, 0)),
            pl.BlockSpec((d, 2 * f), lambda i: (0, 0)),
            pl.BlockSpec((f, d), lambda i: (0, 0)),
        ],
        out_specs=pl.BlockSpec((tm, d), lambda i: (i, 0)),
        out_shape=jax.ShapeDtypeStruct((t, d), F32),
        compiler_params=_params("parallel"),
        name="ffn_dense",
    )(x2d, g, w_gu, w_down)


def _qkv_kernel(x_ref, gq_ref, gkv_ref, wq_ref, wkt_ref, wv_ref, kn_ref, q_ref, kt_ref, v_ref, *, tm, d):
    y = _rms_scale(x_ref[...])
    hq = (y * gq_ref[...]).astype(BF16)
    hkv = (y * gkv_ref[...]).astype(BF16)
    q_ref[...] = _dot(hq, wq_ref[...]).astype(BF16)
    v_ref[...] = _dot(hkv, wv_ref[...]).astype(BF16)
    kt = lax.dot_general(wkt_ref[...], hkv, (((1,), (1,)), ((), ())), preferred_element_type=F32)
    k3 = kt.reshape(d // HEAD_DIM, HEAD_DIM, tm)
    k3 = k3 * lax.rsqrt(jnp.mean(k3 * k3, axis=1, keepdims=True) + EPS) * kn_ref[...][None]
    kt_ref[...] = k3.reshape(d, tm).astype(BF16)


def _qkv(x2d, g_q, g_kv, w_q, w_kt, w_v, k_norm_col):
    t, d = x2d.shape
    tm = ROW_TILE
    const = lambda i: (0, 0)
    return pl.pallas_call(
        functools.partial(_qkv_kernel, tm=tm, d=d),
        grid=(t // tm,),
        in_specs=[
            pl.BlockSpec((tm, d), lambda i: (i, 0)),
            pl.BlockSpec((1, d), const),
            pl.BlockSpec((1, d), const),
            pl.BlockSpec((d, d), const),
            pl.BlockSpec((d, d), const),
            pl.BlockSpec((d, d), const),
            pl.BlockSpec((HEAD_DIM, 1), const),
        ],
        out_specs=[
            pl.BlockSpec((tm, d), lambda i: (i, 0)),
            pl.BlockSpec((d, tm), lambda i: (0, i)),
            pl.BlockSpec((tm, d), lambda i: (i, 0)),
        ],
        out_shape=[
            jax.ShapeDtypeStruct((t, d), BF16),
            jax.ShapeDtypeStruct((d, t), BF16),
            jax.ShapeDtypeStruct((t, d), BF16),
        ],
        compiler_params=_params("parallel"),
        name="qkv_proj",
    )(x2d, g_q, g_kv, w_q, w_kt, w_v, k_norm_col)


def _group_mean_sq(x, ones_bd, group):
    sq = x * x
    hi = sq.astype(BF16)
    lo = (sq - hi.astype(F32)).astype(BF16)
    return (_dot(hi, ones_bd) + _dot(lo, ones_bd)) * (1.0 / group)


def _attn_kernel(bound_ref, q_ref, kt_ref, v_ref, qn_ref, lam_ref, sn_ref, wgu_ref, wdn_ref,
                 o_ref, wgu_bf_ref, wdn_bf_ref, v1_ref, qn_s, *, seq, tq, heads, lam_init):
    wgu_bf_ref[...] = wgu_ref[...].astype(BF16)
    wdn_bf_ref[...] = wdn_ref[...].astype(BF16)
    lp = lam_ref[...]
    lam = (jnp.exp(jnp.sum(lp[0:1] * lp[1:2], axis=-1, keepdims=True))
           - jnp.exp(jnp.sum(lp[2:3] * lp[3:4], axis=-1, keepdims=True)) + lam_init)
    row = lax.broadcasted_iota(jnp.int32, (tq, tq), 0)
    col = lax.broadcasted_iota(jnp.int32, (tq, tq), 1)
    causal = col <= row
    gi = lax.broadcasted_iota(jnp.int32, (V_DIM, V_DIM), 0) // HEAD_DIM
    gj = lax.broadcasted_iota(jnp.int32, (V_DIM, V_DIM), 1) // HEAD_DIM
    ones_bd = jnp.where(gi == gj, 1.0, 0.0).astype(BF16)
    qgain = qn_ref[...] * (HEAD_DIM ** -0.5 * LOG2E)
    maps = [slice(c * HEAD_DIM, (c + 1) * HEAD_DIM) for c in range(2)]
    n_q = seq // tq

    for hh in range(heads):
        hcols = slice(hh * V_DIM, (hh + 1) * V_DIM)
        hrow = hh * V_DIM
        v1_ref[:, 0:V_DIM] = v_ref[:, hcols]
        v1_ref[:, V_DIM:] = jnp.ones((seq, V_DIM), BF16)
        q = q_ref[:, hcols].astype(F32)
        ms = _group_mean_sq(q, ones_bd, HEAD_DIM)
        qn_s[...] = (q * lax.rsqrt(ms + EPS) * qgain).astype(BF16)

        def finish(q0, r1, r2, hcols=hcols):
            o = r1[:, 0:V_DIM] / r1[:, V_DIM:] - lam * (r2[:, 0:V_DIM] / r2[:, V_DIM:])
            o = _rms_scale(o) * sn_ref[...] * (1.0 - lam_init)
            o_ref[pl.ds(q0, tq), hcols] = o.astype(BF16)

        def kt_rows(sl, hrow=hrow):
            return slice(hrow + sl.start, hrow + sl.stop)

        @pl.when(bound_ref[0] <= SAFE_SHIFT)
        def _fixed_shift(finish=finish, kt_rows=kt_rows):
            shift = bound_ref[0] * LOG2E

            def scores(qi):
                q0 = qi * tq
                out = []
                for sl in maps:
                    qc = qn_s[q0:q0 + tq, sl]
                    s_diag = _dot(qc, kt_ref[kt_rows(sl), q0:q0 + tq])
                    s_low = _dot(qc, kt_ref[kt_rows(sl), 0:q0]) if q0 > 0 else None
                    out.append((s_diag, s_low))
                return out

            order = list(reversed(range(n_q)))
            pending = scores(order[0])
            for idx, qi in enumerate(order):
                q0 = qi * tq
                cur = pending
                if idx + 1 < n_q:
                    pending = scores(order[idx + 1])
                res = []
                for s_diag, s_low in cur:
                    p = jnp.where(causal, jnp.exp2(s_diag - shift), 0.0).astype(BF16)
                    r = _dot(p, v1_ref[q0:q0 + tq, :])
                    if s_low is not None:
                        r = r + _dot(jnp.exp2(s_low - shift).astype(BF16), v1_ref[0:q0, :])
                    res.append(r)
                finish(q0, *res)

        @pl.when(jnp.logical_not(bound_ref[0] <= SAFE_SHIFT))
        def _running_max(finish=finish, kt_rows=kt_rows):
            def q_body(qi, carry):
                q0 = pl.multiple_of(qi * tq, tq)
                qs = [qn_s[pl.ds(q0, tq), sl] for sl in maps]

                def tile(j, state, masked):
                    k0 = pl.multiple_of(j * tq, tq)
                    out = []
                    for c, sl in enumerate(maps):
                        m, r = state[c]
                        s = _dot(qs[c], kt_ref[kt_rows(sl), pl.ds(k0, tq)])
                        if masked:
                            s = jnp.where(causal, s, -jnp.inf)
                        m_new = jnp.maximum(m, jnp.max(s, axis=-1, keepdims=True))
                        p = jnp.exp2(s - m_new).astype(BF16)
                        r = jnp.exp2(m - m_new) * r + _dot(p, v1_ref[pl.ds(k0, tq), :])
                        out.append((m_new, r))
                    return tuple(out)

                one = (jnp.full((tq, 1), -jnp.inf, F32), jnp.zeros((tq, 2 * V_DIM), F32))
                state = lax.fori_loop(0, qi, lambda j, st: tile(j, st, False), (one, one))
                (_, r1), (_, r2) = tile(qi, state, True)
                finish(q0, r1, r2)
                return carry

            lax.fori_loop(0, n_q, q_body, 0)


def _attention(score_bound, q, kt, v, q_norm2, lam_params, sub_norm, w_gu, w_down, batch, seq, lam_init):
    t, d = q.shape
    n_heads = d // V_DIM
    hps = ATTN_HEADS_PER_STEP
    assert n_heads % hps == 0
    hsteps = n_heads // hps
    steps = batch * hsteps
    wgu2 = w_gu.reshape(-1, w_gu.shape[-1])
    wdn2 = w_down.reshape(-1, w_down.shape[-1])
    gu_rows, dn_rows = wgu2.shape[0] // steps, wdn2.shape[0] // steps
    assert wgu2.shape[0] % steps == 0 and wdn2.shape[0] % steps == 0 and gu_rows % 16 == 0 and dn_rows % 16 == 0
    slab = lambda b, h: (b * hsteps + h, 0)
    hw = hps * V_DIM
    kern = functools.partial(_attn_kernel, seq=seq, tq=ATTN_TILE, heads=hps, lam_init=lam_init)
    o, wgu_bf, wdn_bf = pl.pallas_call(
        kern,
        grid=(batch, hsteps),
        in_specs=[
            pl.BlockSpec(memory_space=pltpu.SMEM),
            pl.BlockSpec((seq, hw), lambda b, h: (b, h)),
            pl.BlockSpec((hw, seq), lambda b, h: (h, b)),
            pl.BlockSpec((seq, hw), lambda b, h: (b, h)),
            pl.BlockSpec((1, V_DIM), lambda b, h: (0, 0)),
            pl.BlockSpec((4, HEAD_DIM), lambda b, h: (0, 0)),
            pl.BlockSpec((1, V_DIM), lambda b, h: (0, 0)),
            pl.BlockSpec((gu_rows, wgu2.shape[1]), slab),
            pl.BlockSpec((dn_rows, wdn2.shape[1]), slab),
        ],
        out_specs=[
            pl.BlockSpec((seq, hw), lambda b, h: (b, h)),
            pl.BlockSpec((gu_rows, wgu2.shape[1]), slab),
            pl.BlockSpec((dn_rows, wdn2.shape[1]), slab),
        ],
        out_shape=[
            jax.ShapeDtypeStruct((t, d), BF16),
            jax.ShapeDtypeStruct(wgu2.shape, BF16),
            jax.ShapeDtypeStruct(wdn2.shape, BF16),
        ],
        scratch_shapes=[pltpu.VMEM((seq, 2 * V_DIM), BF16), pltpu.VMEM((seq, V_DIM), BF16)],
        compiler_params=_params("parallel", "parallel"),
        name="diff_attn",
    )(score_bound, q, kt, v, q_norm2, lam_params, sub_norm, wgu2, wdn2)
    return o, wgu_bf.reshape(w_gu.shape), wdn_bf.reshape(w_down.shape)


def _oproj_router_kernel(x_ref, o_ref, wo_ref, g_ref, rw_ref,
                         x3_ref, h8_ref, topi_ref, gate_ref, rank_ref, cnt_ref, carry, lower,
                         *, tm, d, n_experts):
    i = pl.program_id(0)

    @pl.when(i == 0)
    def _():
        carry[...] = jnp.zeros_like(carry)
        r = lax.broadcasted_iota(jnp.int32, (tm, tm), 0)
        cidx = lax.broadcasted_iota(jnp.int32, (tm, tm), 1)
        lower[...] = jnp.where(cidx < r, 1.0, 0.0).astype(BF16)

    x3 = x_ref[...] + _dot(o_ref[...], wo_ref[...])
    x3_ref[...] = x3
    h = _rms_scale(x3) * g_ref[...]
    for c in range(d // LANES):
        h8_ref[pl.ds(c, tm, stride=d // LANES), :] = h[:, c * LANES:(c + 1) * LANES]

    rw = rw_ref[...]
    h_hi = h.astype(BF16)
    h_lo = (h - h_hi.astype(F32)).astype(BF16)
    w_hi = rw.astype(BF16)
    w_lo = (rw - w_hi.astype(F32)).astype(BF16)
    both = _dot(h_hi, jnp.concatenate([w_hi, w_lo], axis=1))
    logits = both[:, 0:LANES] + (both[:, LANES:] + _dot(h_lo, w_hi))

    lane = lax.broadcasted_iota(jnp.int32, (tm, LANES), 1)
    neg = -jnp.inf
    l1 = jnp.where(lane < n_experts, logits, neg)
    m1 = jnp.max(l1, axis=-1, keepdims=True)
    i1 = jnp.min(jnp.where(l1 == m1, lane, LANES), axis=-1, keepdims=True)
    l2 = jnp.where(lane == i1, neg, l1)
    m2 = jnp.max(l2, axis=-1, keepdims=True)
    i2 = jnp.min(jnp.where(l2 == m2, lane, LANES), axis=-1, keepdims=True)
    e2 = jnp.exp(m2 - m1)
    den = 1.0 + e2
    first = lax.broadcasted_iota(jnp.int32, (tm, TOP_K), 1) == 0
    topi_ref[...] = jnp.where(first, i1, i2)
    gate_ref[...] = jnp.where(first, 1.0 / den, e2 / den)

    oh1 = lane == i1
    oh2 = lane == i2
    cnt = jnp.where(oh1, 1.0, 0.0) + jnp.where(oh2, 1.0, 0.0)
    before = _dot(lower[...], cnt.astype(BF16)) + carry[...]
    r1 = jnp.sum(jnp.where(oh1, before, 0.0), axis=-1, keepdims=True)
    r2 = jnp.sum(jnp.where(oh2, before, 0.0), axis=-1, keepdims=True)
    rank_ref[...] = jnp.where(first, r1, r2).astype(jnp.int32)
    total = carry[...] + jnp.sum(cnt, axis=0, keepdims=True)
    carry[...] = total
    cnt_ref[...] = total.astype(jnp.int32)


def _oproj_router(x2d, o, w_o, g, rw_pad, n_experts):
    t, d = x2d.shape
    tm = ROW_TILE
    const = lambda i: (0, 0)
    kern = functools.partial(_oproj_router_kernel, tm=tm, d=d, n_experts=n_experts)
    return pl.pallas_call(
        kern,
        grid=(t // tm,),
        in_specs=[
            pl.BlockSpec((tm, d), lambda i: (i, 0)),
            pl.BlockSpec((tm, d), lambda i: (i, 0)),
            pl.BlockSpec((d, d), const),
            pl.BlockSpec((1, d), const),
            pl.BlockSpec((d, LANES), const),
        ],
        out_specs=[
            pl.BlockSpec((tm, d), lambda i: (i, 0)),
            pl.BlockSpec((tm * (d // LANES), LANES), lambda i: (i, 0)),
            pl.BlockSpec((tm, TOP_K), lambda i: (i, 0)),
            pl.BlockSpec((tm, TOP_K), lambda i: (i, 0)),
            pl.BlockSpec((tm, TOP_K), lambda i: (i, 0)),
            pl.BlockSpec((1, LANES), const),
        ],
        out_shape=[
            jax.ShapeDtypeStruct((t, d), F32),
            jax.ShapeDtypeStruct((t * (d // LANES), LANES), F32),
            jax.ShapeDtypeStruct((t, TOP_K), jnp.int32),
            jax.ShapeDtypeStruct((t, TOP_K), F32),
            jax.ShapeDtypeStruct((t, TOP_K), jnp.int32),
            jax.ShapeDtypeStruct((1, LANES), jnp.int32),
        ],
        scratch_shapes=[pltpu.VMEM((1, LANES), F32), pltpu.VMEM((tm, tm), BF16)],
        compiler_params=_params("arbitrary"),
        name="oproj_router",
    )(x2d, o, w_o, g, rw_pad)


def _pad_bits(tile):
    return [1 << b for b in reversed(range(int(math.log2(tile))))]


def _dispatch_kernel(s0_ref, s1_ref, pad_start_ref, pad_len_ref, h_ref, hs_hbm, zeros, sem, zsem,
                     *, tb, nc, n_experts, tile):
    i = pl.program_id(0)
    base = i * tb

    def rows(ref, first, count):
        start = first * nc
        if not isinstance(start, int):
            start = pl.multiple_of(start, nc)
        return ref.at[pl.ds(start, count * nc), :]

    def row_copy(r, slot):
        return pltpu.make_async_copy(rows(h_ref, r, 1), rows(hs_hbm, slot, 1), sem)

    def issue(g, carry):
        for u in range(DMA_UNROLL):
            r = g * DMA_UNROLL + u
            row_copy(r, s0_ref[base + r]).start(priority=0)
            row_copy(r, s1_ref[base + r]).start(priority=1)
        return carry

    lax.fori_loop(0, tb // DMA_UNROLL, issue, 0)

    def pad_copies(do):
        for e in range(n_experts):
            n = pad_len_ref[e]
            pos = pad_start_ref[e]
            for bit in _pad_bits(tile):
                @pl.when((n & bit) != 0)
                def _(pos=pos, bit=bit):
                    do(pltpu.make_async_copy(rows(zeros, 0, bit), rows(hs_hbm, pos, bit), zsem))
                pos = pos + (n & bit)
        for k in range(n_experts):
            @pl.when(k < pad_len_ref[n_experts])
            def _(k=k):
                pos = pad_start_ref[n_experts] + k * tile
                do(pltpu.make_async_copy(zeros, rows(hs_hbm, pos, tile), zsem))

    @pl.when(i == 0)
    def _():
        zeros[...] = jnp.zeros_like(zeros)
        pad_copies(lambda cp: cp.start())
        pad_copies(lambda cp: cp.wait())

    def drain(g, carry):
        for u in range(2 * DMA_UNROLL):
            row_copy(0, 0).wait()
        return carry

    lax.fori_loop(0, tb // DMA_UNROLL, drain, 0)


def _dispatch(slot0, slot1, pad_start, pad_len, h8, nc, n_slots, n_experts):
    t = h8.shape[0] // nc
    tb = DISPATCH_BLOCK
    kern = functools.partial(_dispatch_kernel, tb=tb, nc=nc, n_experts=n_experts, tile=GROUP_TILE)
    grid_spec = pltpu.PrefetchScalarGridSpec(
        num_scalar_prefetch=4,
        grid=(t // tb,),
        in_specs=[pl.BlockSpec((tb * nc, LANES), lambda i, *_: (i, 0))],
        out_specs=pl.BlockSpec(memory_space=pl.ANY),
        scratch_shapes=[
            pltpu.VMEM((GROUP_TILE * nc, LANES), F32),
            pltpu.SemaphoreType.DMA,
            pltpu.SemaphoreType.DMA,
        ],
    )
    return pl.pallas_call(
        kern,
        grid_spec=grid_spec,
        out_shape=jax.ShapeDtypeStruct((n_slots * nc, LANES), F32),
        compiler_params=_params("arbitrary"),
        name="moe_dispatch",
    )(slot0, slot1, pad_start, pad_len, h8)


def _moe_group_kernel(te_ref, nt_ref, hs_ref, wg_ref, wu_ref, wd_ref, ys_ref, *, tm, d, f_exp):
    j = pl.program_id(0)
    nc = d // LANES

    @pl.when(j >= nt_ref[0])
    def _():
        ys_ref[...] = jnp.zeros_like(ys_ref)

    @pl.when(j < nt_ref[0])
    def _():
        h = jnp.concatenate([hs_ref[pl.ds(c, tm, stride=nc), :].astype(BF16) for c in range(nc)], axis=1)
        y = None
        for lo in range(0, f_exp, MOE_F_CHUNK):
            hi = min(lo + MOE_F_CHUNK, f_exp)
            gate = _dot(h, wg_ref[:, lo:hi])
            up = _dot(h, wu_ref[:, lo:hi])
            a = (gate * jax.nn.sigmoid(gate) * up).astype(BF16)
            p = _dot(a, wd_ref[lo:hi, :])
            y = p if y is None else y + p
        for c in range(nc):
            ys_ref[pl.ds(c, tm, stride=nc), :] = y[:, c * LANES:(c + 1) * LANES]


def _moe_group(tile_expert, n_tiles, hs2d, w_gu, w_down, max_tiles):
    n_e, d, f2 = w_gu.shape
    f_exp = f2 // 2
    tm = GROUP_TILE
    nc = d // LANES

    def row_map(j, te, nt):
        return (jnp.minimum(j, nt[0] - 1), 0)

    kern = functools.partial(_moe_group_kernel, tm=tm, d=d, f_exp=f_exp)
    grid_spec = pltpu.PrefetchScalarGridSpec(
        num_scalar_prefetch=2,
        grid=(max_tiles,),
        in_specs=[
            pl.BlockSpec((tm * nc, LANES), row_map),
            pl.BlockSpec((None, d, f_exp), lambda j, te, nt: (te[j], 0, 0)),
            pl.BlockSpec((None, d, f_exp), lambda j, te, nt: (te[j], 0, 1)),
            pl.BlockSpec((None, f_exp, d), lambda j, te, nt: (te[j], 0, 0)),
        ],
        out_specs=pl.BlockSpec((tm * nc, LANES), lambda j, te, nt: (j, 0)),
    )
    return pl.pallas_call(
        kern,
        grid_spec=grid_spec,
        out_shape=jax.ShapeDtypeStruct(hs2d.shape, F32),
        compiler_params=pltpu.CompilerParams(dimension_semantics=("arbitrary",),
                                             vmem_limit_bytes=MOE_VMEM_LIMIT),
        name="moe_group",
    )(tile_expert, n_tiles, hs2d, w_gu, w_gu, w_down)


def _combine_kernel(s0_ref, s1_ref, x_ref, gate_ref, ys_hbm, o_ref, bufs, sems, *, tm, d):
    i = pl.program_id(0)
    n = pl.num_programs(0)
    nc = d // LANES

    def row_copy(slot, par, k, r):
        start = r * nc
        if not isinstance(start, int):
            start = pl.multiple_of(start, nc)
        src = ys_hbm.at[pl.ds(pl.multiple_of(slot * nc, nc), nc), :]
        return pltpu.make_async_copy(src, bufs.at[par, k, pl.ds(start, nc), :], sems.at[par])

    def issue(block, par):
        base = block * tm

        def body(g, carry):
            for u in range(DMA_UNROLL):
                r = g * DMA_UNROLL + u
                row_copy(s0_ref[base + r], par, 0, r).start(priority=0)
                row_copy(s1_ref[base + r], par, 1, r).start(priority=1)
            return carry

        lax.fori_loop(0, tm // DMA_UNROLL, body, 0)

    def drain(par):
        def body(g, carry):
            for u in range(DMA_UNROLL):
                row_copy(0, par, 0, 0).wait()
                row_copy(0, par, 1, 0).wait()
            return carry

        lax.fori_loop(0, tm // DMA_UNROLL, body, 0)

    @pl.when(i == 0)
    def _():
        issue(0, 0)

    for par in range(2):
        @pl.when(i % 2 == par)
        def _(par=par):
            @pl.when(i + 1 < n)
            def _():
                issue(i + 1, 1 - par)

            drain(par)
            gates = gate_ref[...]
            g0 = gates[:, 0:1]
            g1 = gates[:, 1:2]
            for c in range(nc):
                y0 = bufs[par, 0, pl.ds(c, tm, stride=nc), :]
                y1 = bufs[par, 1, pl.ds(c, tm, stride=nc), :]
                cols = slice(c * LANES, (c + 1) * LANES)
                o_ref[:, cols] = x_ref[:, cols] + (g0 * y0 + g1 * y1)


def _combine(slot0, slot1, x3, gates, ys2d):
    t, d = x3.shape
    tm = ROW_TILE
    nc = d // LANES
    grid_spec = pltpu.PrefetchScalarGridSpec(
        num_scalar_prefetch=2,
        grid=(t // tm,),
        in_specs=[
            pl.BlockSpec((tm, d), lambda i, s0, s1: (i, 0)),
            pl.BlockSpec((tm, TOP_K), lambda i, s0, s1: (i, 0)),
            pl.BlockSpec(memory_space=pl.ANY),
        ],
        out_specs=pl.BlockSpec((tm, d), lambda i, s0, s1: (i, 0)),
        scratch_shapes=[
            pltpu.VMEM((2, TOP_K, tm * nc, LANES), F32),
            pltpu.SemaphoreType.DMA((2,)),
        ],
    )
    return pl.pallas_call(
        functools.partial(_combine_kernel, tm=tm, d=d),
        grid_spec=grid_spec,
        out_shape=jax.ShapeDtypeStruct((t, d), F32),
        compiler_params=_params("arbitrary"),
        name="moe_combine",
    )(slot0, slot1, x3, gates, ys2d)


def _routing_tables(topi, rank, counts, n_experts, max_tiles):
    tile = GROUP_TILE
    padded = ((counts + tile - 1) // tile) * tile
    ends = jnp.cumsum(padded)
    offs = ends - padded
    onehot = topi[..., None] == jnp.arange(n_experts, dtype=jnp.int32)
    slot = jnp.sum(jnp.where(onehot, offs, 0), axis=-1) + rank
    tile_ends = ends // tile
    n_tiles = tile_ends[-1]
    j = jnp.minimum(jnp.arange(max_tiles, dtype=jnp.int32), n_tiles - 1)
    tile_expert = jnp.sum((j[:, None] >= tile_ends[None, :]).astype(jnp.int32), axis=-1)
    pad_start = jnp.concatenate([offs + counts, ends[-1:]]).astype(jnp.int32)
    pad_len = jnp.concatenate([padded - counts, max_tiles - n_tiles.reshape(1)]).astype(jnp.int32)
    return (slot[:, 0], slot[:, 1], pad_start, pad_len,
            tile_expert.astype(jnp.int32), n_tiles.reshape(1).astype(jnp.int32))


def _lambda_init(layer_idx_1based):
    return 0.8 - 0.6 * math.exp(-0.3 * (layer_idx_1based - 1))


def kernel(x, ln_mix, ln_ffn, conv_w_in, conv_w, conv_w_out, ln_kv, w_kv, k_norm, attn_w_q, q_norm, lam_params,
           sub_norm, attn_w_o, ffn_w_gu, ffn_w_down, router_w, moe_w_gu, moe_w_down):
    batch, seq, d = x.shape
    t = batch * seq
    n_experts = router_w.shape[-1]
    assert ln_mix.shape[0] == 2 and conv_w_in.shape[0] == 1 and attn_w_q.shape[0] == 1
    assert seq % ROW_TILE == 0 and seq % ATTN_TILE == 0 and d % LANES == 0
    nc = d // LANES
    x2d = x.reshape(t, d)

    x1 = _mixer_a(x2d, ln_mix[0:1], conv_w_in[0].astype(BF16), conv_w[0], conv_w_out[0].astype(BF16), seq)
    x2 = _ffn_dense(x1, ln_ffn[0:1], ffn_w_gu[0].astype(BF16), ffn_w_down[0].astype(BF16))

    n_k = d
    w_kt = w_kv[:, :n_k].T.astype(BF16)
    w_v = w_kv[:, n_k:].astype(BF16)
    q, kt, v = _qkv(x2, ln_mix[1:2], ln_kv.reshape(1, d), attn_w_q[0].astype(BF16), w_kt, w_v,
                    k_norm.reshape(HEAD_DIM, 1))

    score_bound = (math.sqrt(HEAD_DIM) * jnp.max(jnp.abs(q_norm[0])) * jnp.max(jnp.abs(k_norm))).reshape(1)
    o, moe_gu_bf, moe_down_bf = _attention(score_bound, q, kt, v, jnp.tile(q_norm[0:1], (1, 2)), lam_params[0],
                                           sub_norm[0:1], moe_w_gu[0], moe_w_down[0], batch, seq, _lambda_init(2))

    rw_pad = jnp.pad(router_w[0], ((0, 0), (0, LANES - n_experts)))
    x3, h8, topi, gates, rank, counts = _oproj_router(x2, o, attn_w_o[0].astype(BF16), ln_ffn[1:2], rw_pad,
                                                      n_experts)
    max_tiles = (TOP_K * t) // GROUP_TILE + n_experts
    n_slots = max_tiles * GROUP_TILE
    slot0, slot1, pad_start, pad_len, tile_expert, n_tiles = _routing_tables(
        topi, rank, counts[0, :n_experts], n_experts, max_tiles)
    hs = _dispatch(slot0, slot1, pad_start, pad_len, h8, nc, n_slots, n_experts)
    ys = _moe_group(tile_expert, n_tiles, hs, moe_gu_bf, moe_down_bf, max_tiles)
    out = _combine(slot0, slot1, x3, gates, ys)
    return out.reshape(batch, seq, d)
```

```python
import functools
import math

import jax
import jax.numpy as jnp
from jax import lax
from jax.experimental import pallas as pl
from jax.experimental.pallas import tpu as pltpu

F32 = jnp.float32
BF16 = jnp.bfloat16

EPS = 1e-6
HEAD_DIM = 64
V_DIM = 2 * HEAD_DIM
CONV_WIDTH = 3
TOP_K = 2

LANES = 128
SUBLANES = 8
VMEM_LIMIT = 56 * 1024 * 1024

ROW_TILE = 512
ATTN_TILE = 256
ATTN_HEADS_PER_STEP = 2
GROUP_TILE = 512
MOE_F_CHUNK = 512
MOE_VMEM_LIMIT = 62 * 1024 * 1024
DISPATCH_BLOCK = 1024
DMA_UNROLL = 16

LOG2E = 1.4426950408889634
RANK_BASE = 128
SAFE_SHIFT = 40.0


def _rms_scale(x):
    return x * lax.rsqrt(jnp.mean(x * x, axis=-1, keepdims=True) + EPS)


def _dot(a, b):
    return jnp.dot(a, b, preferred_element_type=F32)


def _resident(shape):
    return pl.BlockSpec(shape, lambda *_: (0,) * len(shape), pipeline_mode=pl.Buffered(1))


def _params(*sem):
    return pltpu.CompilerParams(dimension_semantics=sem, vmem_limit_bytes=VMEM_LIMIT)


def _mixer_a_kernel(x_ref, g_ref, win_ref, cw_ref, wout_ref, o_ref, ubuf, *, tm, d, tiles_per_seq):
    i = pl.program_id(0)
    x = x_ref[...]
    h = (_rms_scale(x) * g_ref[...]).astype(BF16)
    c = _dot(h, win_ref[:, d:2 * d].astype(BF16))
    v = _dot(h, win_ref[:, 2 * d:].astype(BF16))
    u = c * v

    @pl.when(i % tiles_per_seq == 0)
    def _():
        ubuf[0:SUBLANES, :] = jnp.zeros((SUBLANES, d), F32)

    ubuf[SUBLANES:tm + SUBLANES, :] = u
    u1 = ubuf[SUBLANES - 1:tm + SUBLANES - 1, :]
    u2 = ubuf[SUBLANES - 2:tm + SUBLANES - 2, :]
    cw = cw_ref[...]
    z = u2 * cw[0:1] + u1 * cw[1:2] + u * cw[2:3]
    b = _dot(h, win_ref[:, 0:d].astype(BF16))
    y = (b * z).astype(BF16)
    o_ref[...] = x + _dot(y, wout_ref[...].astype(BF16))
    ubuf[0:SUBLANES, :] = ubuf[tm:tm + SUBLANES, :]


def _mixer_a(x2d, g, w_in, conv_w, w_out, seq):
    t, d = x2d.shape
    tm = ROW_TILE
    kern = functools.partial(_mixer_a_kernel, tm=tm, d=d, tiles_per_seq=seq // tm)
    return pl.pallas_call(
        kern,
        grid=(t // tm,),
        in_specs=[
            pl.BlockSpec((tm, d), lambda i: (i, 0)),
            pl.BlockSpec((1, d), lambda i: (0, 0)),
            _resident((d, 3 * d)),
            pl.BlockSpec((CONV_WIDTH, d), lambda i: (0, 0)),
            _resident((d, d)),
        ],
        out_specs=pl.BlockSpec((tm, d), lambda i: (i, 0)),
        out_shape=jax.ShapeDtypeStruct((t, d), F32),
        scratch_shapes=[pltpu.VMEM((tm + SUBLANES, d), F32)],
        compiler_params=_params("arbitrary"),
        name="mixer_a",
    )(x2d, g, w_in, conv_w, w_out)


def _ffn_chunks(f):
    step = 1024
    return [(lo, min(lo + step, f)) for lo in range(0, f, step)]


def _ffn_kernel(x_ref, g_ref, wgu_ref, wd_ref, o_ref, *, f):
    x = x_ref[...]
    h = (_rms_scale(x) * g_ref[...]).astype(BF16)
    acc = x
    for lo, hi in _ffn_chunks(f):
        gate = _dot(h, wgu_ref[:, lo:hi].astype(BF16))
        up = _dot(h, wgu_ref[:, f + lo:f + hi].astype(BF16))
        a = (gate * jax.nn.sigmoid(gate) * up).astype(BF16)
        acc = acc + _dot(a, wd_ref[lo:hi, :].astype(BF16))
    o_ref[...] = acc


def _ffn_dense(x2d, g, w_gu, w_down):
    t, d = x2d.shape
    f = w_down.shape[0]
    tm = ROW_TILE
    return pl.pallas_call(
        functools.partial(_ffn_kernel, f=f),
        grid=(t // tm,),
        in_specs=[
            pl.BlockSpec((tm, d), lambda i: (i, 0)),
            pl.BlockSpec((1, d), lambda i: (0, 0)),
            _resident((d, 2 * f)),
            _resident((f, d)),
        ],
        out_specs=pl.BlockSpec((tm, d), lambda i: (i, 0)),
        out_shape=jax.ShapeDtypeStruct((t, d), F32),
        compiler_params=_params("parallel"),
        name="ffn_dense",
    )(x2d, g, w_gu, w_down)


def _qkv_kernel(x_ref, gq_ref, gkv_ref, wq_ref, wkt_ref, wv_ref, kn_ref, q_ref, kt_ref, v_ref, *, tm, d):
    y = _rms_scale(x_ref[...])
    hq = (y * gq_ref[...]).astype(BF16)
    hkv = (y * gkv_ref[...]).astype(BF16)
    q_ref[...] = _dot(hq, wq_ref[...].astype(BF16)).astype(BF16)
    v_ref[...] = _dot(hkv, wv_ref[...].astype(BF16)).astype(BF16)
    kt = lax.dot_general(wkt_ref[...].astype(BF16), hkv, (((1,), (1,)), ((), ())), preferred_element_type=F32)
    k3 = kt.reshape(d // HEAD_DIM, HEAD_DIM, tm)
    k3 = k3 * lax.rsqrt(jnp.mean(k3 * k3, axis=1, keepdims=True) + EPS) * kn_ref[...][None]
    kt_ref[...] = k3.reshape(d, tm).astype(BF16)


def _qkv(x2d, g_q, g_kv, w_q, w_kt, w_v, k_norm_col):
    t, d = x2d.shape
    tm = ROW_TILE
    const = lambda i: (0, 0)
    return pl.pallas_call(
        functools.partial(_qkv_kernel, tm=tm, d=d),
        grid=(t // tm,),
        in_specs=[
            pl.BlockSpec((tm, d), lambda i: (i, 0)),
            pl.BlockSpec((1, d), const),
            pl.BlockSpec((1, d), const),
            _resident((d, d)),
            _resident((d, d)),
            _resident((d, d)),
            pl.BlockSpec((HEAD_DIM, 1), const),
        ],
        out_specs=[
            pl.BlockSpec((tm, d), lambda i: (i, 0)),
            pl.BlockSpec((d, tm), lambda i: (0, i)),
            pl.BlockSpec((tm, d), lambda i: (i, 0)),
        ],
        out_shape=[
            jax.ShapeDtypeStruct((t, d), BF16),
            jax.ShapeDtypeStruct((d, t), BF16),
            jax.ShapeDtypeStruct((t, d), BF16),
        ],
        compiler_params=_params("parallel"),
        name="qkv_proj",
    )(x2d, g_q, g_kv, w_q, w_kt, w_v, k_norm_col)


def _group_mean_sq(x, ones_bd, group):
    sq = x * x
    hi = sq.astype(BF16)
    lo = (sq - hi.astype(F32)).astype(BF16)
    return (_dot(hi, ones_bd) + _dot(lo, ones_bd)) * (1.0 / group)


def _attn_kernel(bound_ref, q_ref, kt_ref, v_ref, qn_ref, lam_ref, sn_ref, wgu_ref, wdn_ref,
                 o_ref, wgu_bf_ref, wdn_bf_ref, v1_ref, qn_s, *, seq, tq, heads, lam_init):
    wgu_bf_ref[...] = wgu_ref[...].astype(BF16)
    wdn_bf_ref[...] = wdn_ref[...].astype(BF16)
    lp = lam_ref[...]
    lam = (jnp.exp(jnp.sum(lp[0:1] * lp[1:2], axis=-1, keepdims=True))
           - jnp.exp(jnp.sum(lp[2:3] * lp[3:4], axis=-1, keepdims=True)) + lam_init)
    row = lax.broadcasted_iota(jnp.int32, (tq, tq), 0)
    col = lax.broadcasted_iota(jnp.int32, (tq, tq), 1)
    causal = col <= row
    gi = lax.broadcasted_iota(jnp.int32, (V_DIM, V_DIM), 0) // HEAD_DIM
    gj = lax.broadcasted_iota(jnp.int32, (V_DIM, V_DIM), 1) // HEAD_DIM
    ones_bd = jnp.where(gi == gj, 1.0, 0.0).astype(BF16)
    qgain = qn_ref[...] * (HEAD_DIM ** -0.5 * LOG2E)
    maps = [slice(c * HEAD_DIM, (c + 1) * HEAD_DIM) for c in range(2)]
    n_q = seq // tq

    for hh in range(heads):
        hcols = slice(hh * V_DIM, (hh + 1) * V_DIM)
        hrow = hh * V_DIM
        v1_ref[:, 0:V_DIM] = v_ref[:, hcols]
        v1_ref[:, V_DIM:] = jnp.ones((seq, V_DIM), BF16)
        q = q_ref[:, hcols].astype(F32)
        ms = _group_mean_sq(q, ones_bd, HEAD_DIM)
        qn_s[...] = (q * lax.rsqrt(ms + EPS) * qgain).astype(BF16)

        def finish(q0, r1, r2, hcols=hcols):
            o = r1[:, 0:V_DIM] / r1[:, V_DIM:] - lam * (r2[:, 0:V_DIM] / r2[:, V_DIM:])
            o = _rms_scale(o) * sn_ref[...] * (1.0 - lam_init)
            o_ref[pl.ds(q0, tq), hcols] = o.astype(BF16)

        def kt_rows(sl, hrow=hrow):
            return slice(hrow + sl.start, hrow + sl.stop)

        @pl.when(bound_ref[0] <= SAFE_SHIFT)
        def _fixed_shift(finish=finish, kt_rows=kt_rows):
            shift = bound_ref[0] * LOG2E

            def scores(qi):
                q0 = qi * tq
                out = []
                for sl in maps:
                    qc = qn_s[q0:q0 + tq, sl]
                    s_diag = _dot(qc, kt_ref[kt_rows(sl), q0:q0 + tq])
                    s_low = _dot(qc, kt_ref[kt_rows(sl), 0:q0]) if q0 > 0 else None
                    out.append((s_diag, s_low))
                return out

            order = list(reversed(range(n_q)))
            pending = scores(order[0])
            for idx, qi in enumerate(order):
                q0 = qi * tq
                cur = pending
                if idx + 1 < n_q:
                    pending = scores(order[idx + 1])
                res = []
                for s_diag, s_low in cur:
                    p = jnp.where(causal, jnp.exp2(s_diag - shift), 0.0).astype(BF16)
                    r = _dot(p, v1_ref[q0:q0 + tq, :])
                    if s_low is not None:
                        r = r + _dot(jnp.exp2(s_low - shift).astype(BF16), v1_ref[0:q0, :])
                    res.append(r)
                finish(q0, *res)

        @pl.when(jnp.logical_not(bound_ref[0] <= SAFE_SHIFT))
        def _running_max(finish=finish, kt_rows=kt_rows):
            def q_body(qi, carry):
                q0 = pl.multiple_of(qi * tq, tq)
                qs = [qn_s[pl.ds(q0, tq), sl] for sl in maps]

                def tile(j, state, masked):
                    k0 = pl.multiple_of(j * tq, tq)
                    out = []
                    for c, sl in enumerate(maps):
                        m, r = state[c]
                        s = _dot(qs[c], kt_ref[kt_rows(sl), pl.ds(k0, tq)])
                        if masked:
                            s = jnp.where(causal, s, -jnp.inf)
                        m_new = jnp.maximum(m, jnp.max(s, axis=-1, keepdims=True))
                        p = jnp.exp2(s - m_new).astype(BF16)
                        r = jnp.exp2(m - m_new) * r + _dot(p, v1_ref[pl.ds(k0, tq), :])
                        out.append((m_new, r))
                    return tuple(out)

                one = (jnp.full((tq, 1), -jnp.inf, F32), jnp.zeros((tq, 2 * V_DIM), F32))
                state = lax.fori_loop(0, qi, lambda j, st: tile(j, st, False), (one, one))
                (_, r1), (_, r2) = tile(qi, state, True)
                finish(q0, r1, r2)
                return carry

            lax.fori_loop(0, n_q, q_body, 0)


def _attention(score_bound, q, kt, v, q_norm2, lam_params, sub_norm, w_gu, w_down, batch, seq, lam_init):
    t, d = q.shape
    n_heads = d // V_DIM
    hps = ATTN_HEADS_PER_STEP
    assert n_heads % hps == 0
    hsteps = n_heads // hps
    steps = batch * hsteps
    wgu2 = w_gu.reshape(-1, w_gu.shape[-1])
    wdn2 = w_down.reshape(-1, w_down.shape[-1])
    gu_rows, dn_rows = wgu2.shape[0] // steps, wdn2.shape[0] // steps
    assert wgu2.shape[0] % steps == 0 and wdn2.shape[0] % steps == 0 and gu_rows % 16 == 0 and dn_rows % 16 == 0
    slab = lambda b, h: (b * hsteps + h, 0)
    hw = hps * V_DIM
    kern = functools.partial(_attn_kernel, seq=seq, tq=ATTN_TILE, heads=hps, lam_init=lam_init)
    o, wgu_bf, wdn_bf = pl.pallas_call(
        kern,
        grid=(batch, hsteps),
        in_specs=[
            pl.BlockSpec(memory_space=pltpu.SMEM),
            pl.BlockSpec((seq, hw), lambda b, h: (b, h)),
            pl.BlockSpec((hw, seq), lambda b, h: (h, b)),
            pl.BlockSpec((seq, hw), lambda b, h: (b, h)),
            pl.BlockSpec((1, V_DIM), lambda b, h: (0, 0)),
            pl.BlockSpec((4, HEAD_DIM), lambda b, h: (0, 0)),
            pl.BlockSpec((1, V_DIM), lambda b, h: (0, 0)),
            pl.BlockSpec((gu_rows, wgu2.shape[1]), slab),
            pl.BlockSpec((dn_rows, wdn2.shape[1]), slab),
        ],
        out_specs=[
            pl.BlockSpec((seq, hw), lambda b, h: (b, h)),
            pl.BlockSpec((gu_rows, wgu2.shape[1]), slab),
            pl.BlockSpec((dn_rows, wdn2.shape[1]), slab),
        ],
        out_shape=[
            jax.ShapeDtypeStruct((t, d), BF16),
            jax.ShapeDtypeStruct(wgu2.shape, BF16),
            jax.ShapeDtypeStruct(wdn2.shape, BF16),
        ],
        scratch_shapes=[pltpu.VMEM((seq, 2 * V_DIM), BF16), pltpu.VMEM((seq, V_DIM), BF16)],
        compiler_params=_params("parallel", "parallel"),
        name="diff_attn",
    )(score_bound, q, kt, v, q_norm2, lam_params, sub_norm, wgu2, wdn2)
    return o, wgu_bf.reshape(w_gu.shape), wdn_bf.reshape(w_down.shape)


def _oproj_router_kernel(x_ref, o_ref, wo_ref, g_ref, rw_ref,
                         x3_ref, h8_ref, gate_ref, route_ref, cnt_ref, carry, lower,
                         *, tm, d, n_experts):
    i = pl.program_id(0)

    @pl.when(i == 0)
    def _():
        carry[...] = jnp.zeros_like(carry)
        r = lax.broadcasted_iota(jnp.int32, (tm, tm), 0)
        cidx = lax.broadcasted_iota(jnp.int32, (tm, tm), 1)
        lower[...] = jnp.where(cidx < r, 1.0, 0.0).astype(BF16)

    x3 = x_ref[...] + _dot(o_ref[...], wo_ref[...].astype(BF16))
    x3_ref[...] = x3
    h = _rms_scale(x3) * g_ref[...]
    for c in range(d // LANES):
        h8_ref[pl.ds(c, tm, stride=d // LANES), :] = h[:, c * LANES:(c + 1) * LANES]

    rw = rw_ref[...]
    h_hi = h.astype(BF16)
    h_lo = (h - h_hi.astype(F32)).astype(BF16)
    w_hi = rw.astype(BF16)
    w_lo = (rw - w_hi.astype(F32)).astype(BF16)
    both = _dot(h_hi, jnp.concatenate([w_hi, w_lo], axis=1))
    logits = both[:, 0:LANES] + (both[:, LANES:] + _dot(h_lo, w_hi))

    lane = lax.broadcasted_iota(jnp.int32, (tm, LANES), 1)
    neg = -jnp.inf
    l1 = jnp.where(lane < n_experts, logits, neg)
    m1 = jnp.max(l1, axis=-1, keepdims=True)
    i1 = jnp.min(jnp.where(l1 == m1, lane, LANES), axis=-1, keepdims=True)
    l2 = jnp.where(lane == i1, neg, l1)
    m2 = jnp.max(l2, axis=-1, keepdims=True)
    i2 = jnp.min(jnp.where(l2 == m2, lane, LANES), axis=-1, keepdims=True)
    e2 = jnp.exp(m2 - m1)
    den = 1.0 + e2
    first = lax.broadcasted_iota(jnp.int32, (tm, TOP_K), 1) == 0
    gate_ref[...] = jnp.where(first, 1.0 / den, e2 / den)

    oh1 = lane == i1
    oh2 = lane == i2
    cnt = jnp.where(oh1, 1.0, 0.0) + jnp.where(oh2, 1.0, 0.0)
    before = _dot(lower[...], cnt.astype(BF16)) + carry[...]
    r1 = jnp.sum(jnp.where(oh1, before, 0.0), axis=-1, keepdims=True)
    r2 = jnp.sum(jnp.where(oh2, before, 0.0), axis=-1, keepdims=True)
    cols = []
    for idx, rnk in ((i1.astype(F32), r1), (i2.astype(F32), r2)):
        hi = jnp.floor(rnk * (1.0 / RANK_BASE))
        cols += [idx, hi, rnk - hi * RANK_BASE]
    table = jnp.zeros((tm, LANES), F32)
    for c, col in enumerate(cols):
        table = jnp.where(lane == c, col, table)
    eye = jnp.where(lax.broadcasted_iota(jnp.int32, (LANES, LANES), 0)
                    == lax.broadcasted_iota(jnp.int32, (LANES, LANES), 1), 1.0, 0.0).astype(BF16)
    table_t = lax.dot_general(eye, table.astype(BF16), (((1,), (1,)), ((), ())), preferred_element_type=F32)
    route_ref[...] = table_t[0:SUBLANES, :]
    total = carry[...] + jnp.sum(cnt, axis=0, keepdims=True)
    carry[...] = total
    cnt_ref[...] = total.astype(jnp.int32)


def _oproj_router(x2d, o, w_o, g, rw_pad, n_experts):
    t, d = x2d.shape
    tm = ROW_TILE
    const = lambda i: (0, 0)
    kern = functools.partial(_oproj_router_kernel, tm=tm, d=d, n_experts=n_experts)
    return pl.pallas_call(
        kern,
        grid=(t // tm,),
        in_specs=[
            pl.BlockSpec((tm, d), lambda i: (i, 0)),
            pl.BlockSpec((tm, d), lambda i: (i, 0)),
            _resident((d, d)),
            pl.BlockSpec((1, d), const),
            pl.BlockSpec((d, LANES), const),
        ],
        out_specs=[
            pl.BlockSpec((tm, d), lambda i: (i, 0)),
            pl.BlockSpec((tm * (d // LANES), LANES), lambda i: (i, 0)),
            pl.BlockSpec((tm, TOP_K), lambda i: (i, 0)),
            pl.BlockSpec((SUBLANES, tm), lambda i: (0, i)),
            pl.BlockSpec((1, LANES), const),
        ],
        out_shape=[
            jax.ShapeDtypeStruct((t, d), F32),
            jax.ShapeDtypeStruct((t * (d // LANES), LANES), F32),
            jax.ShapeDtypeStruct((t, TOP_K), F32),
            jax.ShapeDtypeStruct((SUBLANES, t), F32),
            jax.ShapeDtypeStruct((1, LANES), jnp.int32),
        ],
        scratch_shapes=[pltpu.VMEM((1, LANES), F32), pltpu.VMEM((tm, tm), BF16)],
        compiler_params=_params("arbitrary"),
        name="oproj_router",
    )(x2d, o, w_o, g, rw_pad)


def _pad_bits(tile):
    return [1 << b for b in reversed(range(int(math.log2(tile))))]


def _dispatch_kernel(s0_ref, s1_ref, pad_start_ref, pad_len_ref, h_ref, hs_hbm, zeros, sem, zsem,
                     *, tb, nc, n_experts, tile):
    i = pl.program_id(0)
    base = i * tb

    def rows(ref, first, count):
        start = first * nc
        if not isinstance(start, int):
            start = pl.multiple_of(start, nc)
        return ref.at[pl.ds(start, count * nc), :]

    def row_copy(r, slot):
        return pltpu.make_async_copy(rows(h_ref, r, 1), rows(hs_hbm, slot, 1), sem)

    def issue(g, carry):
        for u in range(DMA_UNROLL):
            r = g * DMA_UNROLL + u
            row_copy(r, s0_ref[base + r]).start(priority=0)
            row_copy(r, s1_ref[base + r]).start(priority=1)
        return carry

    lax.fori_loop(0, tb // DMA_UNROLL, issue, 0)

    def pad_copies(do):
        for e in range(n_experts):
            n = pad_len_ref[e]
            pos = pad_start_ref[e]
            for bit in _pad_bits(tile):
                @pl.when((n & bit) != 0)
                def _(pos=pos, bit=bit):
                    do(pltpu.make_async_copy(rows(zeros, 0, bit), rows(hs_hbm, pos, bit), zsem))
                pos = pos + (n & bit)
        for k in range(n_experts):
            @pl.when(k < pad_len_ref[n_experts])
            def _(k=k):
                pos = pad_start_ref[n_experts] + k * tile
                do(pltpu.make_async_copy(zeros, rows(hs_hbm, pos, tile), zsem))

    @pl.when(i == 0)
    def _():
        zeros[...] = jnp.zeros_like(zeros)
        pad_copies(lambda cp: cp.start())
        pad_copies(lambda cp: cp.wait())

    def drain(g, carry):
        for u in range(2 * DMA_UNROLL):
            row_copy(0, 0).wait()
        return carry

    lax.fori_loop(0, tb // DMA_UNROLL, drain, 0)


def _dispatch(slot0, slot1, pad_start, pad_len, h8, nc, n_slots, n_experts):
    t = h8.shape[0] // nc
    tb = DISPATCH_BLOCK
    kern = functools.partial(_dispatch_kernel, tb=tb, nc=nc, n_experts=n_experts, tile=GROUP_TILE)
    grid_spec = pltpu.PrefetchScalarGridSpec(
        num_scalar_prefetch=4,
        grid=(t // tb,),
        in_specs=[pl.BlockSpec((tb * nc, LANES), lambda i, *_: (i, 0))],
        out_specs=pl.BlockSpec(memory_space=pl.ANY),
        scratch_shapes=[
            pltpu.VMEM((GROUP_TILE * nc, LANES), F32),
            pltpu.SemaphoreType.DMA,
            pltpu.SemaphoreType.DMA,
        ],
    )
    return pl.pallas_call(
        kern,
        grid_spec=grid_spec,
        out_shape=jax.ShapeDtypeStruct((n_slots * nc, LANES), F32),
        compiler_params=_params("arbitrary"),
        name="moe_dispatch",
    )(slot0, slot1, pad_start, pad_len, h8)


def _moe_group_kernel(te_ref, nt_ref, hs_ref, wg_ref, wu_ref, wd_ref, ys_ref, *, tm, d, f_exp):
    j = pl.program_id(0)
    nc = d // LANES

    @pl.when(j >= nt_ref[0])
    def _():
        ys_ref[...] = jnp.zeros_like(ys_ref)

    @pl.when(j < nt_ref[0])
    def _():
        h = jnp.concatenate([hs_ref[pl.ds(c, tm, stride=nc), :].astype(BF16) for c in range(nc)], axis=1)
        y = None
        for lo in range(0, f_exp, MOE_F_CHUNK):
            hi = min(lo + MOE_F_CHUNK, f_exp)
            gate = _dot(h, wg_ref[:, lo:hi])
            up = _dot(h, wu_ref[:, lo:hi])
            a = (gate * jax.nn.sigmoid(gate) * up).astype(BF16)
            p = _dot(a, wd_ref[lo:hi, :])
            y = p if y is None else y + p
        for c in range(nc):
            ys_ref[pl.ds(c, tm, stride=nc), :] = y[:, c * LANES:(c + 1) * LANES]


def _moe_group(tile_expert, n_tiles, hs2d, w_gu, w_down, max_tiles):
    n_e, d, f2 = w_gu.shape
    f_exp = f2 // 2
    tm = GROUP_TILE
    nc = d // LANES

    def row_map(j, te, nt):
        return (jnp.minimum(j, nt[0] - 1), 0)

    kern = functools.partial(_moe_group_kernel, tm=tm, d=d, f_exp=f_exp)
    grid_spec = pltpu.PrefetchScalarGridSpec(
        num_scalar_prefetch=2,
        grid=(max_tiles,),
        in_specs=[
            pl.BlockSpec((tm * nc, LANES), row_map),
            pl.BlockSpec((None, d, f_exp), lambda j, te, nt: (te[j], 0, 0)),
            pl.BlockSpec((None, d, f_exp), lambda j, te, nt: (te[j], 0, 1)),
            pl.BlockSpec((None, f_exp, d), lambda j, te, nt: (te[j], 0, 0)),
        ],
        out_specs=pl.BlockSpec((tm * nc, LANES), lambda j, te, nt: (j, 0)),
    )
    return pl.pallas_call(
        kern,
        grid_spec=grid_spec,
        out_shape=jax.ShapeDtypeStruct(hs2d.shape, F32),
        compiler_params=pltpu.CompilerParams(dimension_semantics=("arbitrary",),
                                             vmem_limit_bytes=MOE_VMEM_LIMIT),
        name="moe_group",
    )(tile_expert, n_tiles, hs2d, w_gu, w_gu, w_down)


def _combine_kernel(s0_ref, s1_ref, x_ref, gate_ref, ys_hbm, o_ref, bufs, sems, *, tm, d):
    i = pl.program_id(0)
    n = pl.num_programs(0)
    nc = d // LANES

    def row_copy(slot, par, k, r):
        start = r * nc
        if not isinstance(start, int):
            start = pl.multiple_of(start, nc)
        src = ys_hbm.at[pl.ds(pl.multiple_of(slot * nc, nc), nc), :]
        return pltpu.make_async_copy(src, bufs.at[par, k, pl.ds(start, nc), :], sems.at[par])

    def issue(block, par):
        base = block * tm

        def body(g, carry):
            for u in range(DMA_UNROLL):
                r = g * DMA_UNROLL + u
                row_copy(s0_ref[base + r], par, 0, r).start(priority=0)
                row_copy(s1_ref[base + r], par, 1, r).start(priority=1)
            return carry

        lax.fori_loop(0, tm // DMA_UNROLL, body, 0)

    def drain(par):
        def body(g, carry):
            for u in range(DMA_UNROLL):
                row_copy(0, par, 0, 0).wait()
                row_copy(0, par, 1, 0).wait()
            return carry

        lax.fori_loop(0, tm // DMA_UNROLL, body, 0)

    @pl.when(i == 0)
    def _():
        issue(0, 0)

    for par in range(2):
        @pl.when(i % 2 == par)
        def _(par=par):
            @pl.when(i + 1 < n)
            def _():
                issue(i + 1, 1 - par)

            drain(par)
            gates = gate_ref[...]
            g0 = gates[:, 0:1]
            g1 = gates[:, 1:2]
            for c in range(nc):
                y0 = bufs[par, 0, pl.ds(c, tm, stride=nc), :]
                y1 = bufs[par, 1, pl.ds(c, tm, stride=nc), :]
                cols = slice(c * LANES, (c + 1) * LANES)
                o_ref[:, cols] = x_ref[:, cols] + (g0 * y0 + g1 * y1)


def _combine(slot0, slot1, x3, gates, ys2d):
    t, d = x3.shape
    tm = ROW_TILE
    nc = d // LANES
    grid_spec = pltpu.PrefetchScalarGridSpec(
        num_scalar_prefetch=2,
        grid=(t // tm,),
        in_specs=[
            pl.BlockSpec((tm, d), lambda i, s0, s1: (i, 0)),
            pl.BlockSpec((tm, TOP_K), lambda i, s0, s1: (i, 0)),
            pl.BlockSpec(memory_space=pl.ANY),
        ],
        out_specs=pl.BlockSpec((tm, d), lambda i, s0, s1: (i, 0)),
        scratch_shapes=[
            pltpu.VMEM((2, TOP_K, tm * nc, LANES), F32),
            pltpu.SemaphoreType.DMA((2,)),
        ],
    )
    return pl.pallas_call(
        functools.partial(_combine_kernel, tm=tm, d=d),
        grid_spec=grid_spec,
        out_shape=jax.ShapeDtypeStruct((t, d), F32),
        compiler_params=_params("arbitrary"),
        name="moe_combine",
    )(slot0, slot1, x3, gates, ys2d)


def _routing_tables(route, counts, n_experts, max_tiles):
    tile = GROUP_TILE
    padded = ((counts + tile - 1) // tile) * tile
    ends = jnp.cumsum(padded)
    offs = ends - padded
    route = route.astype(jnp.int32)
    slots = []
    for k in range(TOP_K):
        expert, hi, lo = route[3 * k], route[3 * k + 1], route[3 * k + 2]
        base = jnp.zeros_like(expert)
        for e in range(n_experts):
            base = jnp.where(expert == e, offs[e], base)
        slots.append(base + hi * RANK_BASE + lo)
    tile_ends = ends // tile
    n_tiles = tile_ends[-1]
    j = jnp.minimum(jnp.arange(max_tiles, dtype=jnp.int32), n_tiles - 1)
    tile_expert = jnp.sum((j[:, None] >= tile_ends[None, :]).astype(jnp.int32), axis=-1)
    pad_start = jnp.concatenate([offs + counts, ends[-1:]]).astype(jnp.int32)
    pad_len = jnp.concatenate([padded - counts, max_tiles - n_tiles.reshape(1)]).astype(jnp.int32)
    return (slots[0], slots[1], pad_start, pad_len,
            tile_expert.astype(jnp.int32), n_tiles.reshape(1).astype(jnp.int32))


def _lambda_init(layer_idx_1based):
    return 0.8 - 0.6 * math.exp(-0.3 * (layer_idx_1based - 1))


def kernel(x, ln_mix, ln_ffn, conv_w_in, conv_w, conv_w_out, ln_kv, w_kv, k_norm, attn_w_q, q_norm, lam_params,
           sub_norm, attn_w_o, ffn_w_gu, ffn_w_down, router_w, moe_w_gu, moe_w_down):
    batch, seq, d = x.shape
    t = batch * seq
    n_experts = router_w.shape[-1]
    assert ln_mix.shape[0] == 2 and conv_w_in.shape[0] == 1 and attn_w_q.shape[0] == 1
    assert seq % ROW_TILE == 0 and seq % ATTN_TILE == 0 and d % LANES == 0
    nc = d // LANES
    x2d = x.reshape(t, d)

    x1 = _mixer_a(x2d, ln_mix[0:1], conv_w_in[0], conv_w[0], conv_w_out[0], seq)
    x2 = _ffn_dense(x1, ln_ffn[0:1], ffn_w_gu[0], ffn_w_down[0])

    n_k = d
    w_kt = w_kv[:, :n_k].T
    w_v = w_kv[:, n_k:]
    q, kt, v = _qkv(x2, ln_mix[1:2], ln_kv.reshape(1, d), attn_w_q[0], w_kt, w_v,
                    k_norm.reshape(HEAD_DIM, 1))

    score_bound = (math.sqrt(HEAD_DIM) * jnp.max(jnp.abs(q_norm[0])) * jnp.max(jnp.abs(k_norm))).reshape(1)
    o, moe_gu_bf, moe_down_bf = _attention(score_bound, q, kt, v, jnp.tile(q_norm[0:1], (1, 2)), lam_params[0],
                                           sub_norm[0:1], moe_w_gu[0], moe_w_down[0], batch, seq, _lambda_init(2))

    rw_pad = jnp.pad(router_w[0], ((0, 0), (0, LANES - n_experts)))
    assert t <= RANK_BASE * 256
    x3, h8, gates, route, counts = _oproj_router(x2, o, attn_w_o[0], ln_ffn[1:2], rw_pad, n_experts)
    max_tiles = (TOP_K * t) // GROUP_TILE + n_experts
    n_slots = max_tiles * GROUP_TILE
    slot0, slot1, pad_start, pad_len, tile_expert, n_tiles = _routing_tables(
        route, counts[0, :n_experts], n_experts, max_tiles)
    hs = _dispatch(slot0, slot1, pad_start, pad_len, h8, nc, n_slots, n_experts)
    ys = _moe_group(tile_expert, n_tiles, hs, moe_gu_bf, moe_down_bf, max_tiles)
    out = _combine(slot0, slot1, x3, gates, ys)
    return out.reshape(batch, seq, d)
```

```python
import functools
import math

import jax
import jax.numpy as jnp
from jax import lax
from jax.experimental import pallas as pl
from jax.experimental.pallas import tpu as pltpu

F32 = jnp.float32
BF16 = jnp.bfloat16

EPS = 1e-6
HEAD_DIM = 64
V_DIM = 2 * HEAD_DIM
CONV_WIDTH = 3
TOP_K = 2

LANES = 128
SUBLANES = 8
VMEM_LIMIT = 56 * 1024 * 1024

ROW_TILE = 512
PROJ_TILE = 1024
ATTN_TILE = 256
ATTN_HEADS_PER_STEP = 2
GROUP_TILE = 512
MOE_F_CHUNK = 512
MOE_VMEM_LIMIT = 62 * 1024 * 1024
DISPATCH_BLOCK = 1024
DMA_UNROLL = 16

LOG2E = 1.4426950408889634
RANK_BASE = 128
SAFE_SHIFT = 40.0


def _rms_scale(x):
    return x * lax.rsqrt(jnp.mean(x * x, axis=-1, keepdims=True) + EPS)


def _dot(a, b):
    return jnp.dot(a, b, preferred_element_type=F32)


def _resident(shape):
    return pl.BlockSpec(shape, lambda *_: (0,) * len(shape), pipeline_mode=pl.Buffered(1))


def _params(*sem):
    return pltpu.CompilerParams(dimension_semantics=sem, vmem_limit_bytes=VMEM_LIMIT)


def _mixer_a_kernel(x_ref, g_ref, win_ref, cw_ref, wout_ref, o_ref, ubuf, *, tm, d, tiles_per_seq):
    i = pl.program_id(0)
    x = x_ref[...]
    h = (_rms_scale(x) * g_ref[...]).astype(BF16)
    c = _dot(h, win_ref[:, d:2 * d].astype(BF16))
    v = _dot(h, win_ref[:, 2 * d:].astype(BF16))
    u = c * v

    @pl.when(i % tiles_per_seq == 0)
    def _():
        ubuf[0:SUBLANES, :] = jnp.zeros((SUBLANES, d), F32)

    ubuf[SUBLANES:tm + SUBLANES, :] = u
    u1 = ubuf[SUBLANES - 1:tm + SUBLANES - 1, :]
    u2 = ubuf[SUBLANES - 2:tm + SUBLANES - 2, :]
    cw = cw_ref[...]
    z = u2 * cw[0:1] + u1 * cw[1:2] + u * cw[2:3]
    b = _dot(h, win_ref[:, 0:d].astype(BF16))
    y = (b * z).astype(BF16)
    o_ref[...] = x + _dot(y, wout_ref[...].astype(BF16))
    ubuf[0:SUBLANES, :] = ubuf[tm:tm + SUBLANES, :]


def _mixer_a(x2d, g, w_in, conv_w, w_out, seq):
    t, d = x2d.shape
    tm = ROW_TILE
    kern = functools.partial(_mixer_a_kernel, tm=tm, d=d, tiles_per_seq=seq // tm)
    return pl.pallas_call(
        kern,
        grid=(t // tm,),
        in_specs=[
            pl.BlockSpec((tm, d), lambda i: (i, 0)),
            pl.BlockSpec((1, d), lambda i: (0, 0)),
            _resident((d, 3 * d)),
            pl.BlockSpec((CONV_WIDTH, d), lambda i: (0, 0)),
            _resident((d, d)),
        ],
        out_specs=pl.BlockSpec((tm, d), lambda i: (i, 0)),
        out_shape=jax.ShapeDtypeStruct((t, d), F32),
        scratch_shapes=[pltpu.VMEM((tm + SUBLANES, d), F32)],
        compiler_params=_params("arbitrary"),
        name="mixer_a",
    )(x2d, g, w_in, conv_w, w_out)


def _ffn_chunks(f):
    step = 1024
    return [(lo, min(lo + step, f)) for lo in range(0, f, step)]


def _ffn_kernel(x_ref, g_ref, wgu_ref, wd_ref, o_ref, *, f):
    x = x_ref[...]
    h = (_rms_scale(x) * g_ref[...]).astype(BF16)
    acc = x
    for lo, hi in _ffn_chunks(f):
        gate = _dot(h, wgu_ref[:, lo:hi].astype(BF16))
        up = _dot(h, wgu_ref[:, f + lo:f + hi].astype(BF16))
        a = (gate * jax.nn.sigmoid(gate) * up).astype(BF16)
        acc = acc + _dot(a, wd_ref[lo:hi, :].astype(BF16))
    o_ref[...] = acc


def _ffn_dense(x2d, g, w_gu, w_down):
    t, d = x2d.shape
    f = w_down.shape[0]
    tm = ROW_TILE
    return pl.pallas_call(
        functools.partial(_ffn_kernel, f=f),
        grid=(t // tm,),
        in_specs=[
            pl.BlockSpec((tm, d), lambda i: (i, 0)),
            pl.BlockSpec((1, d), lambda i: (0, 0)),
            _resident((d, 2 * f)),
            _resident((f, d)),
        ],
        out_specs=pl.BlockSpec((tm, d), lambda i: (i, 0)),
        out_shape=jax.ShapeDtypeStruct((t, d), F32),
        compiler_params=_params("parallel"),
        name="ffn_dense",
    )(x2d, g, w_gu, w_down)


def _qkv_kernel(x_ref, gq_ref, gkv_ref, wq_ref, wkt_ref, wv_ref, kn_ref, q_ref, kt_ref, v_ref, *, tm, d):
    y = _rms_scale(x_ref[...])
    hq = (y * gq_ref[...]).astype(BF16)
    hkv = (y * gkv_ref[...]).astype(BF16)
    q_ref[...] = _dot(hq, wq_ref[...].astype(BF16)).astype(BF16)
    v_ref[...] = _dot(hkv, wv_ref[...].astype(BF16)).astype(BF16)
    kt = lax.dot_general(wkt_ref[...].astype(BF16), hkv, (((1,), (1,)), ((), ())), preferred_element_type=F32)
    k3 = kt.reshape(d // HEAD_DIM, HEAD_DIM, tm)
    k3 = k3 * lax.rsqrt(jnp.mean(k3 * k3, axis=1, keepdims=True) + EPS) * kn_ref[...][None]
    kt_ref[...] = k3.reshape(d, tm).astype(BF16)


def _qkv(x2d, g_q, g_kv, w_q, w_kt, w_v, k_norm_col):
    t, d = x2d.shape
    tm = PROJ_TILE
    const = lambda i: (0, 0)
    return pl.pallas_call(
        functools.partial(_qkv_kernel, tm=tm, d=d),
        grid=(t // tm,),
        in_specs=[
            pl.BlockSpec((tm, d), lambda i: (i, 0)),
            pl.BlockSpec((1, d), const),
            pl.BlockSpec((1, d), const),
            _resident((d, d)),
            _resident((d, d)),
            _resident((d, d)),
            pl.BlockSpec((HEAD_DIM, 1), const),
        ],
        out_specs=[
            pl.BlockSpec((tm, d), lambda i: (i, 0)),
            pl.BlockSpec((d, tm), lambda i: (0, i)),
            pl.BlockSpec((tm, d), lambda i: (i, 0)),
        ],
        out_shape=[
            jax.ShapeDtypeStruct((t, d), BF16),
            jax.ShapeDtypeStruct((d, t), BF16),
            jax.ShapeDtypeStruct((t, d), BF16),
        ],
        compiler_params=_params("parallel"),
        name="qkv_proj",
    )(x2d, g_q, g_kv, w_q, w_kt, w_v, k_norm_col)


def _group_mean_sq(x, ones_bd, group):
    return _dot((x * x).astype(BF16), ones_bd) * (1.0 / group)


def _attn_kernel(bound_ref, q_ref, kt_ref, v_ref, qn_ref, lam_ref, sn_ref, wgu_ref, wdn_ref,
                 o_ref, wgu_bf_ref, wdn_bf_ref, v1_ref, qn_s, *, seq, tq, heads, lam_init):
    wgu_bf_ref[...] = wgu_ref[...].astype(BF16)
    wdn_bf_ref[...] = wdn_ref[...].astype(BF16)
    lp = lam_ref[...]
    lam = (jnp.exp(jnp.sum(lp[0:1] * lp[1:2], axis=-1, keepdims=True))
           - jnp.exp(jnp.sum(lp[2:3] * lp[3:4], axis=-1, keepdims=True)) + lam_init)
    row = lax.broadcasted_iota(jnp.int32, (tq, tq), 0)
    col = lax.broadcasted_iota(jnp.int32, (tq, tq), 1)
    causal = col <= row
    gi = lax.broadcasted_iota(jnp.int32, (V_DIM, V_DIM), 0) // HEAD_DIM
    gj = lax.broadcasted_iota(jnp.int32, (V_DIM, V_DIM), 1) // HEAD_DIM
    ones_bd = jnp.where(gi == gj, 1.0, 0.0).astype(BF16)
    qgain = qn_ref[...] * (HEAD_DIM ** -0.5 * LOG2E)
    maps = [slice(c * HEAD_DIM, (c + 1) * HEAD_DIM) for c in range(2)]
    n_q = seq // tq

    v1_ref[:, V_DIM:] = jnp.ones((seq, V_DIM), BF16)
    for hh in range(heads):
        hcols = slice(hh * V_DIM, (hh + 1) * V_DIM)
        hrow = hh * V_DIM
        v1_ref[:, 0:V_DIM] = v_ref[:, hcols]
        q = q_ref[:, hcols].astype(F32)
        ms = _group_mean_sq(q, ones_bd, HEAD_DIM)
        qn_s[...] = (q * lax.rsqrt(ms + EPS) * qgain).astype(BF16)

        def finish(q0, r1, r2, hcols=hcols):
            o = r1[:, 0:V_DIM] / r1[:, V_DIM:] - lam * (r2[:, 0:V_DIM] / r2[:, V_DIM:])
            o = _rms_scale(o) * sn_ref[...] * (1.0 - lam_init)
            o_ref[pl.ds(q0, tq), hcols] = o.astype(BF16)

        def kt_rows(sl, hrow=hrow):
            return slice(hrow + sl.start, hrow + sl.stop)

        @pl.when(bound_ref[0] <= SAFE_SHIFT)
        def _fixed_shift(finish=finish, kt_rows=kt_rows):
            shift = bound_ref[0] * LOG2E

            def scores(qi):
                q0 = qi * tq
                out = []
                for sl in maps:
                    qc = qn_s[q0:q0 + tq, sl]
                    s_diag = _dot(qc, kt_ref[kt_rows(sl), q0:q0 + tq])
                    s_low = _dot(qc, kt_ref[kt_rows(sl), 0:q0]) if q0 > 0 else None
                    out.append((s_diag, s_low))
                return out

            order = list(reversed(range(n_q)))
            pending = scores(order[0])
            for idx, qi in enumerate(order):
                q0 = qi * tq
                cur = pending
                if idx + 1 < n_q:
                    pending = scores(order[idx + 1])
                res = []
                for s_diag, s_low in cur:
                    p = jnp.where(causal, jnp.exp2(s_diag - shift), 0.0).astype(BF16)
                    r = _dot(p, v1_ref[q0:q0 + tq, :])
                    if s_low is not None:
                        r = r + _dot(jnp.exp2(s_low - shift).astype(BF16), v1_ref[0:q0, :])
                    res.append(r)
                finish(q0, *res)

        @pl.when(jnp.logical_not(bound_ref[0] <= SAFE_SHIFT))
        def _running_max(finish=finish, kt_rows=kt_rows):
            def q_body(qi, carry):
                q0 = pl.multiple_of(qi * tq, tq)
                qs = [qn_s[pl.ds(q0, tq), sl] for sl in maps]

                def tile(j, state, masked):
                    k0 = pl.multiple_of(j * tq, tq)
                    out = []
                    for c, sl in enumerate(maps):
                        m, r = state[c]
                        s = _dot(qs[c], kt_ref[kt_rows(sl), pl.ds(k0, tq)])
                        if masked:
                            s = jnp.where(causal, s, -jnp.inf)
                        m_new = jnp.maximum(m, jnp.max(s, axis=-1, keepdims=True))
                        p = jnp.exp2(s - m_new).astype(BF16)
                        r = jnp.exp2(m - m_new) * r + _dot(p, v1_ref[pl.ds(k0, tq), :])
                        out.append((m_new, r))
                    return tuple(out)

                one = (jnp.full((tq, 1), -jnp.inf, F32), jnp.zeros((tq, 2 * V_DIM), F32))
                state = lax.fori_loop(0, qi, lambda j, st: tile(j, st, False), (one, one))
                (_, r1), (_, r2) = tile(qi, state, True)
                finish(q0, r1, r2)
                return carry

            lax.fori_loop(0, n_q, q_body, 0)


def _attention(score_bound, q, kt, v, q_norm2, lam_params, sub_norm, w_gu, w_down, batch, seq, lam_init):
    t, d = q.shape
    n_heads = d // V_DIM
    hps = ATTN_HEADS_PER_STEP
    assert n_heads % hps == 0
    hsteps = n_heads // hps
    steps = batch * hsteps
    wgu2 = w_gu.reshape(-1, w_gu.shape[-1])
    wdn2 = w_down.reshape(-1, w_down.shape[-1])
    gu_rows, dn_rows = wgu2.shape[0] // steps, wdn2.shape[0] // steps
    assert wgu2.shape[0] % steps == 0 and wdn2.shape[0] % steps == 0 and gu_rows % 16 == 0 and dn_rows % 16 == 0
    slab = lambda b, h: (b * hsteps + h, 0)
    hw = hps * V_DIM
    kern = functools.partial(_attn_kernel, seq=seq, tq=ATTN_TILE, heads=hps, lam_init=lam_init)
    o, wgu_bf, wdn_bf = pl.pallas_call(
        kern,
        grid=(batch, hsteps),
        in_specs=[
            pl.BlockSpec(memory_space=pltpu.SMEM),
            pl.BlockSpec((seq, hw), lambda b, h: (b, h)),
            pl.BlockSpec((hw, seq), lambda b, h: (h, b)),
            pl.BlockSpec((seq, hw), lambda b, h: (b, h)),
            pl.BlockSpec((1, V_DIM), lambda b, h: (0, 0)),
            pl.BlockSpec((4, HEAD_DIM), lambda b, h: (0, 0)),
            pl.BlockSpec((1, V_DIM), lambda b, h: (0, 0)),
            pl.BlockSpec((gu_rows, wgu2.shape[1]), slab),
            pl.BlockSpec((dn_rows, wdn2.shape[1]), slab),
        ],
        out_specs=[
            pl.BlockSpec((seq, hw), lambda b, h: (b, h)),
            pl.BlockSpec((gu_rows, wgu2.shape[1]), slab),
            pl.BlockSpec((dn_rows, wdn2.shape[1]), slab),
        ],
        out_shape=[
            jax.ShapeDtypeStruct((t, d), BF16),
            jax.ShapeDtypeStruct(wgu2.shape, BF16),
            jax.ShapeDtypeStruct(wdn2.shape, BF16),
        ],
        scratch_shapes=[pltpu.VMEM((seq, 2 * V_DIM), BF16), pltpu.VMEM((seq, V_DIM), BF16)],
        compiler_params=_params("parallel", "parallel"),
        name="diff_attn",
    )(score_bound, q, kt, v, q_norm2, lam_params, sub_norm, wgu2, wdn2)
    return o, wgu_bf.reshape(w_gu.shape), wdn_bf.reshape(w_down.shape)


def _oproj_router_kernel(x_ref, o_ref, wo_ref, g_ref, rw_ref,
                         x3_ref, h8_ref, gate_ref, route_ref, cnt_ref, carry, lower,
                         *, tm, d, n_experts):
    i = pl.program_id(0)

    @pl.when(i == 0)
    def _():
        carry[...] = jnp.zeros_like(carry)
        r = lax.broadcasted_iota(jnp.int32, (tm, tm), 0)
        cidx = lax.broadcasted_iota(jnp.int32, (tm, tm), 1)
        lower[...] = jnp.where(cidx < r, 1.0, 0.0).astype(BF16)

    x3 = x_ref[...] + _dot(o_ref[...], wo_ref[...].astype(BF16))
    x3_ref[...] = x3
    h = _rms_scale(x3) * g_ref[...]
    for c in range(d // LANES):
        h8_ref[pl.ds(c, tm, stride=d // LANES), :] = h[:, c * LANES:(c + 1) * LANES]

    rw = rw_ref[...]
    h_hi = h.astype(BF16)
    h_lo = (h - h_hi.astype(F32)).astype(BF16)
    w_hi = rw.astype(BF16)
    w_lo = (rw - w_hi.astype(F32)).astype(BF16)
    both = _dot(h_hi, jnp.concatenate([w_hi, w_lo], axis=1))
    logits = both[:, 0:LANES] + (both[:, LANES:] + _dot(h_lo, w_hi))

    lane = lax.broadcasted_iota(jnp.int32, (tm, LANES), 1)
    neg = -jnp.inf
    l1 = jnp.where(lane < n_experts, logits, neg)
    m1 = jnp.max(l1, axis=-1, keepdims=True)
    i1 = jnp.min(jnp.where(l1 == m1, lane, LANES), axis=-1, keepdims=True)
    l2 = jnp.where(lane == i1, neg, l1)
    m2 = jnp.max(l2, axis=-1, keepdims=True)
    i2 = jnp.min(jnp.where(l2 == m2, lane, LANES), axis=-1, keepdims=True)
    e2 = jnp.exp(m2 - m1)
    den = 1.0 + e2
    first = lax.broadcasted_iota(jnp.int32, (tm, TOP_K), 1) == 0
    gate_ref[...] = jnp.where(first, 1.0 / den, e2 / den)

    oh1 = lane == i1
    oh2 = lane == i2
    cnt = jnp.where(oh1, 1.0, 0.0) + jnp.where(oh2, 1.0, 0.0)
    before = _dot(lower[...], cnt.astype(BF16)) + carry[...]
    r1 = jnp.sum(jnp.where(oh1, before, 0.0), axis=-1, keepdims=True)
    r2 = jnp.sum(jnp.where(oh2, before, 0.0), axis=-1, keepdims=True)
    cols = []
    for idx, rnk in ((i1.astype(F32), r1), (i2.astype(F32), r2)):
        hi = jnp.floor(rnk * (1.0 / RANK_BASE))
        cols += [idx, hi, rnk - hi * RANK_BASE]
    table = jnp.zeros((tm, LANES), F32)
    for c, col in enumerate(cols):
        table = jnp.where(lane == c, col, table)
    eye = jnp.where(lax.broadcasted_iota(jnp.int32, (LANES, LANES), 0)
                    == lax.broadcasted_iota(jnp.int32, (LANES, LANES), 1), 1.0, 0.0).astype(BF16)
    table_t = lax.dot_general(eye, table.astype(BF16), (((1,), (1,)), ((), ())), preferred_element_type=F32)
    route_ref[...] = table_t[0:SUBLANES, :]
    total = carry[...] + jnp.sum(cnt, axis=0, keepdims=True)
    carry[...] = total
    cnt_ref[...] = total.astype(jnp.int32)


def _oproj_router(x2d, o, w_o, g, rw_pad, n_experts):
    t, d = x2d.shape
    tm = PROJ_TILE
    const = lambda i: (0, 0)
    kern = functools.partial(_oproj_router_kernel, tm=tm, d=d, n_experts=n_experts)
    return pl.pallas_call(
        kern,
        grid=(t // tm,),
        in_specs=[
            pl.BlockSpec((tm, d), lambda i: (i, 0)),
            pl.BlockSpec((tm, d), lambda i: (i, 0)),
            _resident((d, d)),
            pl.BlockSpec((1, d), const),
            pl.BlockSpec((d, LANES), const),
        ],
        out_specs=[
            pl.BlockSpec((tm, d), lambda i: (i, 0)),
            pl.BlockSpec((tm * (d // LANES), LANES), lambda i: (i, 0)),
            pl.BlockSpec((tm, TOP_K), lambda i: (i, 0)),
            pl.BlockSpec((SUBLANES, tm), lambda i: (0, i)),
            pl.BlockSpec((1, LANES), const),
        ],
        out_shape=[
            jax.ShapeDtypeStruct((t, d), F32),
            jax.ShapeDtypeStruct((t * (d // LANES), LANES), F32),
            jax.ShapeDtypeStruct((t, TOP_K), F32),
            jax.ShapeDtypeStruct((SUBLANES, t), F32),
            jax.ShapeDtypeStruct((1, LANES), jnp.int32),
        ],
        scratch_shapes=[pltpu.VMEM((1, LANES), F32), pltpu.VMEM((tm, tm), BF16)],
        compiler_params=_params("arbitrary"),
        name="oproj_router",
    )(x2d, o, w_o, g, rw_pad)


def _pad_bits(tile):
    return [1 << b for b in reversed(range(int(math.log2(tile))))]


def _dispatch_kernel(s0_ref, s1_ref, pad_start_ref, pad_len_ref, h_ref, hs_hbm, zeros, sem, zsem,
                     *, tb, nc, n_experts, tile):
    i = pl.program_id(0)
    base = i * tb

    def rows(ref, first, count):
        start = first * nc
        if not isinstance(start, int):
            start = pl.multiple_of(start, nc)
        return ref.at[pl.ds(start, count * nc), :]

    def row_copy(r, slot):
        return pltpu.make_async_copy(rows(h_ref, r, 1), rows(hs_hbm, slot, 1), sem)

    def issue(g, carry):
        for u in range(DMA_UNROLL):
            r = g * DMA_UNROLL + u
            row_copy(r, s0_ref[base + r]).start(priority=0)
            row_copy(r, s1_ref[base + r]).start(priority=1)
        return carry

    lax.fori_loop(0, tb // DMA_UNROLL, issue, 0)

    def pad_copies(do):
        for e in range(n_experts):
            n = pad_len_ref[e]
            pos = pad_start_ref[e]
            for bit in _pad_bits(tile):
                @pl.when((n & bit) != 0)
                def _(pos=pos, bit=bit):
                    do(pltpu.make_async_copy(rows(zeros, 0, bit), rows(hs_hbm, pos, bit), zsem))
                pos = pos + (n & bit)
        for k in range(n_experts):
            @pl.when(k < pad_len_ref[n_experts])
            def _(k=k):
                pos = pad_start_ref[n_experts] + k * tile
                do(pltpu.make_async_copy(zeros, rows(hs_hbm, pos, tile), zsem))

    @pl.when(i == 0)
    def _():
        zeros[...] = jnp.zeros_like(zeros)
        pad_copies(lambda cp: cp.start())
        pad_copies(lambda cp: cp.wait())

    def drain(g, carry):
        for u in range(2 * DMA_UNROLL):
            row_copy(0, 0).wait()
        return carry

    lax.fori_loop(0, tb // DMA_UNROLL, drain, 0)


def _dispatch(slot0, slot1, pad_start, pad_len, h8, nc, n_slots, n_experts):
    t = h8.shape[0] // nc
    tb = DISPATCH_BLOCK
    kern = functools.partial(_dispatch_kernel, tb=tb, nc=nc, n_experts=n_experts, tile=GROUP_TILE)
    grid_spec = pltpu.PrefetchScalarGridSpec(
        num_scalar_prefetch=4,
        grid=(t // tb,),
        in_specs=[pl.BlockSpec((tb * nc, LANES), lambda i, *_: (i, 0))],
        out_specs=pl.BlockSpec(memory_space=pl.ANY),
        scratch_shapes=[
            pltpu.VMEM((GROUP_TILE * nc, LANES), F32),
            pltpu.SemaphoreType.DMA,
            pltpu.SemaphoreType.DMA,
        ],
    )
    return pl.pallas_call(
        kern,
        grid_spec=grid_spec,
        out_shape=jax.ShapeDtypeStruct((n_slots * nc, LANES), F32),
        compiler_params=_params("arbitrary"),
        name="moe_dispatch",
    )(slot0, slot1, pad_start, pad_len, h8)


def _moe_group_kernel(te_ref, nt_ref, hs_ref, wg_ref, wu_ref, wd_ref, ys_ref, *, tm, d, f_exp):
    j = pl.program_id(0)
    nc = d // LANES

    @pl.when(j >= nt_ref[0])
    def _():
        ys_ref[...] = jnp.zeros_like(ys_ref)

    @pl.when(j < nt_ref[0])
    def _():
        h = jnp.concatenate([hs_ref[pl.ds(c, tm, stride=nc), :].astype(BF16) for c in range(nc)], axis=1)
        y = None
        for lo in range(0, f_exp, MOE_F_CHUNK):
            hi = min(lo + MOE_F_CHUNK, f_exp)
            gate = _dot(h, wg_ref[:, lo:hi])
            up = _dot(h, wu_ref[:, lo:hi])
            a = (gate * jax.nn.sigmoid(gate) * up).astype(BF16)
            p = _dot(a, wd_ref[lo:hi, :])
            y = p if y is None else y + p
        for c in range(nc):
            ys_ref[pl.ds(c, tm, stride=nc), :] = y[:, c * LANES:(c + 1) * LANES]


def _moe_group(tile_expert, n_tiles, hs2d, w_gu, w_down, max_tiles):
    n_e, d, f2 = w_gu.shape
    f_exp = f2 // 2
    tm = GROUP_TILE
    nc = d // LANES

    def row_map(j, te, nt):
        return (jnp.minimum(j, nt[0] - 1), 0)

    kern = functools.partial(_moe_group_kernel, tm=tm, d=d, f_exp=f_exp)
    grid_spec = pltpu.PrefetchScalarGridSpec(
        num_scalar_prefetch=2,
        grid=(max_tiles,),
        in_specs=[
            pl.BlockSpec((tm * nc, LANES), row_map),
            pl.BlockSpec((None, d, f_exp), lambda j, te, nt: (te[j], 0, 0)),
            pl.BlockSpec((None, d, f_exp), lambda j, te, nt: (te[j], 0, 1)),
            pl.BlockSpec((None, f_exp, d), lambda j, te, nt: (te[j], 0, 0)),
        ],
        out_specs=pl.BlockSpec((tm * nc, LANES), lambda j, te, nt: (j, 0)),
    )
    return pl.pallas_call(
        kern,
        grid_spec=grid_spec,
        out_shape=jax.ShapeDtypeStruct(hs2d.shape, F32),
        compiler_params=pltpu.CompilerParams(dimension_semantics=("arbitrary",),
                                             vmem_limit_bytes=MOE_VMEM_LIMIT),
        name="moe_group",
    )(tile_expert, n_tiles, hs2d, w_gu, w_gu, w_down)


def _combine_kernel(s0_ref, s1_ref, x_ref, gate_ref, ys_hbm, o_ref, b00, b01, b10, b11, sems, *, tm, d):
    i = pl.program_id(0)
    n = pl.num_programs(0)
    nc = d // LANES
    bufs = ((b00, b01), (b10, b11))

    def row_copy(slot, par, k, r):
        start = r * nc
        if not isinstance(start, int):
            start = pl.multiple_of(start, nc)
        src = ys_hbm.at[pl.ds(pl.multiple_of(slot * nc, nc), nc), :]
        return pltpu.make_async_copy(src, bufs[par][k].at[pl.ds(start, nc), :], sems.at[par])

    def issue(block, par):
        base = block * tm

        def body(g, carry):
            for u in range(DMA_UNROLL):
                r = g * DMA_UNROLL + u
                row_copy(s0_ref[base + r], par, 0, r).start(priority=0)
                row_copy(s1_ref[base + r], par, 1, r).start(priority=1)
            return carry

        lax.fori_loop(0, tm // DMA_UNROLL, body, 0)

    def drain(par):
        def body(g, carry):
            for u in range(DMA_UNROLL):
                row_copy(0, par, 0, 0).wait()
                row_copy(0, par, 1, 0).wait()
            return carry

        lax.fori_loop(0, tm // DMA_UNROLL, body, 0)

    @pl.when(i == 0)
    def _():
        issue(0, 0)

    for par in range(2):
        @pl.when(i % 2 == par)
        def _(par=par):
            @pl.when(i + 1 < n)
            def _():
                issue(i + 1, 1 - par)

            drain(par)
            gates = gate_ref[...]
            g0 = gates[:, 0:1]
            g1 = gates[:, 1:2]
            for c in range(nc):
                y0 = bufs[par][0][pl.ds(c, tm, stride=nc), :]
                y1 = bufs[par][1][pl.ds(c, tm, stride=nc), :]
                cols = slice(c * LANES, (c + 1) * LANES)
                o_ref[:, cols] = x_ref[:, cols] + (g0 * y0 + g1 * y1)


def _combine(slot0, slot1, x3, gates, ys2d):
    t, d = x3.shape
    tm = ROW_TILE
    nc = d // LANES
    grid_spec = pltpu.PrefetchScalarGridSpec(
        num_scalar_prefetch=2,
        grid=(t // tm,),
        in_specs=[
            pl.BlockSpec((tm, d), lambda i, s0, s1: (i, 0)),
            pl.BlockSpec((tm, TOP_K), lambda i, s0, s1: (i, 0)),
            pl.BlockSpec(memory_space=pl.ANY),
        ],
        out_specs=pl.BlockSpec((tm, d), lambda i, s0, s1: (i, 0)),
        scratch_shapes=[
            pltpu.VMEM((tm * nc, LANES), F32),
            pltpu.VMEM((tm * nc, LANES), F32),
            pltpu.VMEM((tm * nc, LANES), F32),
            pltpu.VMEM((tm * nc, LANES), F32),
            pltpu.SemaphoreType.DMA((2,)),
        ],
    )
    return pl.pallas_call(
        functools.partial(_combine_kernel, tm=tm, d=d),
        grid_spec=grid_spec,
        out_shape=jax.ShapeDtypeStruct((t, d), F32),
        compiler_params=_params("arbitrary"),
        name="moe_combine",
    )(slot0, slot1, x3, gates, ys2d)


def _routing_tables(route, counts, n_experts, max_tiles):
    tile = GROUP_TILE
    padded = ((counts + tile - 1) // tile) * tile
    ends = jnp.cumsum(padded)
    offs = ends - padded
    route = route.astype(jnp.int32)
    slots = []
    for k in range(TOP_K):
        expert, hi, lo = route[3 * k], route[3 * k + 1], route[3 * k + 2]
        base = jnp.zeros_like(expert)
        for e in range(n_experts):
            base = jnp.where(expert == e, offs[e], base)
        slots.append(base + hi * RANK_BASE + lo)
    tile_ends = ends // tile
    n_tiles = tile_ends[-1]
    j = jnp.minimum(jnp.arange(max_tiles, dtype=jnp.int32), n_tiles - 1)
    tile_expert = jnp.sum((j[:, None] >= tile_ends[None, :]).astype(jnp.int32), axis=-1)
    pad_start = jnp.concatenate([offs + counts, ends[-1:]]).astype(jnp.int32)
    pad_len = jnp.concatenate([padded - counts, max_tiles - n_tiles.reshape(1)]).astype(jnp.int32)
    return (slots[0], slots[1], pad_start, pad_len,
            tile_expert.astype(jnp.int32), n_tiles.reshape(1).astype(jnp.int32))


def _lambda_init(layer_idx_1based):
    return 0.8 - 0.6 * math.exp(-0.3 * (layer_idx_1based - 1))


def kernel(x, ln_mix, ln_ffn, conv_w_in, conv_w, conv_w_out, ln_kv, w_kv, k_norm, attn_w_q, q_norm, lam_params,
           sub_norm, attn_w_o, ffn_w_gu, ffn_w_down, router_w, moe_w_gu, moe_w_down):
    batch, seq, d = x.shape
    t = batch * seq
    n_experts = router_w.shape[-1]
    assert ln_mix.shape[0] == 2 and conv_w_in.shape[0] == 1 and attn_w_q.shape[0] == 1
    assert seq % ROW_TILE == 0 and seq % ATTN_TILE == 0 and d % LANES == 0
    nc = d // LANES
    x2d = x.reshape(t, d)

    x1 = _mixer_a(x2d, ln_mix[0:1], conv_w_in[0], conv_w[0], conv_w_out[0], seq)
    x2 = _ffn_dense(x1, ln_ffn[0:1], ffn_w_gu[0], ffn_w_down[0])

    n_k = d
    w_kt = w_kv[:, :n_k].T
    w_v = w_kv[:, n_k:]
    q, kt, v = _qkv(x2, ln_mix[1:2], ln_kv.reshape(1, d), attn_w_q[0], w_kt, w_v,
                    k_norm.reshape(HEAD_DIM, 1))

    score_bound = (math.sqrt(HEAD_DIM) * jnp.max(jnp.abs(q_norm[0])) * jnp.max(jnp.abs(k_norm))).reshape(1)
    o, moe_gu_bf, moe_down_bf = _attention(score_bound, q, kt, v, jnp.tile(q_norm[0:1], (1, 2)), lam_params[0],
                                           sub_norm[0:1], moe_w_gu[0], moe_w_down[0], batch, seq, _lambda_init(2))

    rw_pad = jnp.pad(router_w[0], ((0, 0), (0, LANES - n_experts)))
    assert t <= RANK_BASE * 256
    x3, h8, gates, route, counts = _oproj_router(x2, o, attn_w_o[0], ln_ffn[1:2], rw_pad, n_experts)
    max_tiles = (TOP_K * t) // GROUP_TILE + n_experts
    n_slots = max_tiles * GROUP_TILE
    slot0, slot1, pad_start, pad_len, tile_expert, n_tiles = _routing_tables(
        route, counts[0, :n_experts], n_experts, max_tiles)
    hs = _dispatch(slot0, slot1, pad_start, pad_len, h8, nc, n_slots, n_experts)
    ys = _moe_group(tile_expert, n_tiles, hs, moe_gu_bf, moe_down_bf, max_tiles)
    out = _combine(slot0, slot1, x3, gates, ys)
    return out.reshape(batch, seq, d)
```

```python
import functools
import math

import jax
import jax.numpy as jnp
from jax import lax
from jax.experimental import pallas as pl
from jax.experimental.pallas import tpu as pltpu

F32 = jnp.float32
BF16 = jnp.bfloat16

EPS = 1e-6
HEAD_DIM = 64
V_DIM = 2 * HEAD_DIM
CONV_WIDTH = 3
TOP_K = 2

LANES = 128
SUBLANES = 8
VMEM_LIMIT = 56 * 1024 * 1024

ROW_TILE = 512
PROJ_TILE = 1024
ATTN_TILE = 256
ATTN_HEADS_PER_STEP = 2
GROUP_TILE = 512
MOE_F_CHUNK = 512
MOE_VMEM_LIMIT = 62 * 1024 * 1024
DISPATCH_BLOCK = 1024
DMA_UNROLL = 16

LOG2E = 1.4426950408889634
SAFE_SHIFT = 40.0


def _rms_scale(x):
    return x * lax.rsqrt(jnp.mean(x * x, axis=-1, keepdims=True) + EPS)


def _dot(a, b):
    return jnp.dot(a, b, preferred_element_type=F32)


def _resident(shape):
    return pl.BlockSpec(shape, lambda *_: (0,) * len(shape), pipeline_mode=pl.Buffered(1))


def _params(*sem):
    return pltpu.CompilerParams(dimension_semantics=sem, vmem_limit_bytes=VMEM_LIMIT)


def _mixer_a_kernel(x_ref, g_ref, win_ref, cw_ref, wout_ref, o_ref, ubuf, *, tm, d, tiles_per_seq):
    i = pl.program_id(0)
    x = x_ref[...]
    h = (_rms_scale(x) * g_ref[...]).astype(BF16)
    c = _dot(h, win_ref[:, d:2 * d].astype(BF16))
    v = _dot(h, win_ref[:, 2 * d:].astype(BF16))
    u = c * v

    @pl.when(i % tiles_per_seq == 0)
    def _():
        ubuf[0:SUBLANES, :] = jnp.zeros((SUBLANES, d), F32)

    ubuf[SUBLANES:tm + SUBLANES, :] = u
    u1 = ubuf[SUBLANES - 1:tm + SUBLANES - 1, :]
    u2 = ubuf[SUBLANES - 2:tm + SUBLANES - 2, :]
    cw = cw_ref[...]
    z = u2 * cw[0:1] + u1 * cw[1:2] + u * cw[2:3]
    b = _dot(h, win_ref[:, 0:d].astype(BF16))
    y = (b * z).astype(BF16)
    o_ref[...] = x + _dot(y, wout_ref[...].astype(BF16))
    ubuf[0:SUBLANES, :] = ubuf[tm:tm + SUBLANES, :]


def _mixer_a(x2d, g, w_in, conv_w, w_out, seq):
    t, d = x2d.shape
    tm = ROW_TILE
    kern = functools.partial(_mixer_a_kernel, tm=tm, d=d, tiles_per_seq=seq // tm)
    return pl.pallas_call(
        kern,
        grid=(t // tm,),
        in_specs=[
            pl.BlockSpec((tm, d), lambda i: (i, 0)),
            pl.BlockSpec((1, d), lambda i: (0, 0)),
            _resident((d, 3 * d)),
            pl.BlockSpec((CONV_WIDTH, d), lambda i: (0, 0)),
            _resident((d, d)),
        ],
        out_specs=pl.BlockSpec((tm, d), lambda i: (i, 0)),
        out_shape=jax.ShapeDtypeStruct((t, d), F32),
        scratch_shapes=[pltpu.VMEM((tm + SUBLANES, d), F32)],
        compiler_params=_params("arbitrary"),
        name="mixer_a",
    )(x2d, g, w_in, conv_w, w_out)


def _ffn_chunks(f):
    step = 1024
    return [(lo, min(lo + step, f)) for lo in range(0, f, step)]


def _ffn_kernel(x_ref, g_ref, wgu_ref, wd_ref, o_ref, *, f):
    x = x_ref[...]
    h = (_rms_scale(x) * g_ref[...]).astype(BF16)
    acc = x
    for lo, hi in _ffn_chunks(f):
        gate = _dot(h, wgu_ref[:, lo:hi].astype(BF16))
        up = _dot(h, wgu_ref[:, f + lo:f + hi].astype(BF16))
        a = (gate * jax.nn.sigmoid(gate) * up).astype(BF16)
        acc = acc + _dot(a, wd_ref[lo:hi, :].astype(BF16))
    o_ref[...] = acc


def _ffn_dense(x2d, g, w_gu, w_down):
    t, d = x2d.shape
    f = w_down.shape[0]
    tm = ROW_TILE
    return pl.pallas_call(
        functools.partial(_ffn_kernel, f=f),
        grid=(t // tm,),
        in_specs=[
            pl.BlockSpec((tm, d), lambda i: (i, 0)),
            pl.BlockSpec((1, d), lambda i: (0, 0)),
            _resident((d, 2 * f)),
            _resident((f, d)),
        ],
        out_specs=pl.BlockSpec((tm, d), lambda i: (i, 0)),
        out_shape=jax.ShapeDtypeStruct((t, d), F32),
        compiler_params=_params("parallel"),
        name="ffn_dense",
    )(x2d, g, w_gu, w_down)


def _qkv_kernel(x_ref, gq_ref, gkv_ref, wq_ref, wkt_ref, wv_ref, kn_ref, q_ref, kt_ref, v_ref, *, tm, d):
    y = _rms_scale(x_ref[...])
    hq = (y * gq_ref[...]).astype(BF16)
    hkv = (y * gkv_ref[...]).astype(BF16)
    q_ref[...] = _dot(hq, wq_ref[...].astype(BF16)).astype(BF16)
    v_ref[...] = _dot(hkv, wv_ref[...].astype(BF16)).astype(BF16)
    kt = lax.dot_general(wkt_ref[...].astype(BF16), hkv, (((1,), (1,)), ((), ())), preferred_element_type=F32)
    k3 = kt.reshape(d // HEAD_DIM, HEAD_DIM, tm)
    k3 = k3 * lax.rsqrt(jnp.mean(k3 * k3, axis=1, keepdims=True) + EPS) * kn_ref[...][None]
    kt_ref[...] = k3.reshape(d, tm).astype(BF16)


def _qkv(x2d, g_q, g_kv, w_q, w_kt, w_v, k_norm_col):
    t, d = x2d.shape
    tm = PROJ_TILE
    const = lambda i: (0, 0)
    return pl.pallas_call(
        functools.partial(_qkv_kernel, tm=tm, d=d),
        grid=(t // tm,),
        in_specs=[
            pl.BlockSpec((tm, d), lambda i: (i, 0)),
            pl.BlockSpec((1, d), const),
            pl.BlockSpec((1, d), const),
            _resident((d, d)),
            _resident((d, d)),
            _resident((d, d)),
            pl.BlockSpec((HEAD_DIM, 1), const),
        ],
        out_specs=[
            pl.BlockSpec((tm, d), lambda i: (i, 0)),
            pl.BlockSpec((d, tm), lambda i: (0, i)),
            pl.BlockSpec((tm, d), lambda i: (i, 0)),
        ],
        out_shape=[
            jax.ShapeDtypeStruct((t, d), BF16),
            jax.ShapeDtypeStruct((d, t), BF16),
            jax.ShapeDtypeStruct((t, d), BF16),
        ],
        compiler_params=_params("parallel"),
        name="qkv_proj",
    )(x2d, g_q, g_kv, w_q, w_kt, w_v, k_norm_col)


def _group_mean_sq(x, ones_bd, group):
    return _dot((x * x).astype(BF16), ones_bd) * (1.0 / group)


def _attn_kernel(bound_ref, q_ref, kt_ref, v_ref, qn_ref, lam_ref, sn_ref, wgu_ref, wdn_ref,
                 o_ref, wgu_bf_ref, wdn_bf_ref, v1_ref, qn_s, *, seq, tq, heads, lam_init):
    wgu_bf_ref[...] = wgu_ref[...].astype(BF16)
    wdn_bf_ref[...] = wdn_ref[...].astype(BF16)
    lp = lam_ref[...]
    lam = (jnp.exp(jnp.sum(lp[0:1] * lp[1:2], axis=-1, keepdims=True))
           - jnp.exp(jnp.sum(lp[2:3] * lp[3:4], axis=-1, keepdims=True)) + lam_init)
    row = lax.broadcasted_iota(jnp.int32, (tq, tq), 0)
    col = lax.broadcasted_iota(jnp.int32, (tq, tq), 1)
    causal = col <= row
    gi = lax.broadcasted_iota(jnp.int32, (V_DIM, V_DIM), 0) // HEAD_DIM
    gj = lax.broadcasted_iota(jnp.int32, (V_DIM, V_DIM), 1) // HEAD_DIM
    ones_bd = jnp.where(gi == gj, 1.0, 0.0).astype(BF16)
    qgain = qn_ref[...] * (HEAD_DIM ** -0.5 * LOG2E)
    maps = [slice(c * HEAD_DIM, (c + 1) * HEAD_DIM) for c in range(2)]
    n_q = seq // tq

    v1_ref[:, V_DIM:] = jnp.ones((seq, V_DIM), BF16)
    for hh in range(heads):
        hcols = slice(hh * V_DIM, (hh + 1) * V_DIM)
        hrow = hh * V_DIM
        v1_ref[:, 0:V_DIM] = v_ref[:, hcols]
        q = q_ref[:, hcols].astype(F32)
        ms = _group_mean_sq(q, ones_bd, HEAD_DIM)
        qn_s[...] = (q * lax.rsqrt(ms + EPS) * qgain).astype(BF16)

        def finish(q0, r1, r2, hcols=hcols):
            o = r1[:, 0:V_DIM] / r1[:, V_DIM:] - lam * (r2[:, 0:V_DIM] / r2[:, V_DIM:])
            o = _rms_scale(o) * sn_ref[...] * (1.0 - lam_init)
            o_ref[pl.ds(q0, tq), hcols] = o.astype(BF16)

        def kt_rows(sl, hrow=hrow):
            return slice(hrow + sl.start, hrow + sl.stop)

        @pl.when(bound_ref[0] <= SAFE_SHIFT)
        def _fixed_shift(finish=finish, kt_rows=kt_rows):
            shift = bound_ref[0] * LOG2E

            def scores(qi):
                q0 = qi * tq
                out = []
                for sl in maps:
                    qc = qn_s[q0:q0 + tq, sl]
                    s_diag = _dot(qc, kt_ref[kt_rows(sl), q0:q0 + tq])
                    s_low = _dot(qc, kt_ref[kt_rows(sl), 0:q0]) if q0 > 0 else None
                    out.append((s_diag, s_low))
                return out

            order = list(reversed(range(n_q)))
            pending = scores(order[0])
            for idx, qi in enumerate(order):
                q0 = qi * tq
                cur = pending
                if idx + 1 < n_q:
                    pending = scores(order[idx + 1])
                res = []
                for s_diag, s_low in cur:
                    p = jnp.where(causal, jnp.exp2(s_diag - shift), 0.0).astype(BF16)
                    r = _dot(p, v1_ref[q0:q0 + tq, :])
                    if s_low is not None:
                        r = r + _dot(jnp.exp2(s_low - shift).astype(BF16), v1_ref[0:q0, :])
                    res.append(r)
                finish(q0, *res)

        @pl.when(jnp.logical_not(bound_ref[0] <= SAFE_SHIFT))
        def _running_max(finish=finish, kt_rows=kt_rows):
            def q_body(qi, carry):
                q0 = pl.multiple_of(qi * tq, tq)
                qs = [qn_s[pl.ds(q0, tq), sl] for sl in maps]

                def tile(j, state, masked):
                    k0 = pl.multiple_of(j * tq, tq)
                    out = []
                    for c, sl in enumerate(maps):
                        m, r = state[c]
                        s = _dot(qs[c], kt_ref[kt_rows(sl), pl.ds(k0, tq)])
                        if masked:
                            s = jnp.where(causal, s, -jnp.inf)
                        m_new = jnp.maximum(m, jnp.max(s, axis=-1, keepdims=True))
                        p = jnp.exp2(s - m_new).astype(BF16)
                        r = jnp.exp2(m - m_new) * r + _dot(p, v1_ref[pl.ds(k0, tq), :])
                        out.append((m_new, r))
                    return tuple(out)

                one = (jnp.full((tq, 1), -jnp.inf, F32), jnp.zeros((tq, 2 * V_DIM), F32))
                state = lax.fori_loop(0, qi, lambda j, st: tile(j, st, False), (one, one))
                (_, r1), (_, r2) = tile(qi, state, True)
                finish(q0, r1, r2)
                return carry

            lax.fori_loop(0, n_q, q_body, 0)


def _attention(score_bound, q, kt, v, q_norm2, lam_params, sub_norm, w_gu, w_down, batch, seq, lam_init):
    t, d = q.shape
    n_heads = d // V_DIM
    hps = ATTN_HEADS_PER_STEP
    assert n_heads % hps == 0
    hsteps = n_heads // hps
    steps = batch * hsteps
    wgu2 = w_gu.reshape(-1, w_gu.shape[-1])
    wdn2 = w_down.reshape(-1, w_down.shape[-1])
    gu_rows, dn_rows = wgu2.shape[0] // steps, wdn2.shape[0] // steps
    assert wgu2.shape[0] % steps == 0 and wdn2.shape[0] % steps == 0 and gu_rows % 16 == 0 and dn_rows % 16 == 0
    slab = lambda b, h: (b * hsteps + h, 0)
    hw = hps * V_DIM
    kern = functools.partial(_attn_kernel, seq=seq, tq=ATTN_TILE, heads=hps, lam_init=lam_init)
    o, wgu_bf, wdn_bf = pl.pallas_call(
        kern,
        grid=(batch, hsteps),
        in_specs=[
            pl.BlockSpec(memory_space=pltpu.SMEM),
            pl.BlockSpec((seq, hw), lambda b, h: (b, h)),
            pl.BlockSpec((hw, seq), lambda b, h: (h, b)),
            pl.BlockSpec((seq, hw), lambda b, h: (b, h)),
            pl.BlockSpec((1, V_DIM), lambda b, h: (0, 0)),
            pl.BlockSpec((4, HEAD_DIM), lambda b, h: (0, 0)),
            pl.BlockSpec((1, V_DIM), lambda b, h: (0, 0)),
            pl.BlockSpec((gu_rows, wgu2.shape[1]), slab),
            pl.BlockSpec((dn_rows, wdn2.shape[1]), slab),
        ],
        out_specs=[
            pl.BlockSpec((seq, hw), lambda b, h: (b, h)),
            pl.BlockSpec((gu_rows, wgu2.shape[1]), slab),
            pl.BlockSpec((dn_rows, wdn2.shape[1]), slab),
        ],
        out_shape=[
            jax.ShapeDtypeStruct((t, d), BF16),
            jax.ShapeDtypeStruct(wgu2.shape, BF16),
            jax.ShapeDtypeStruct(wdn2.shape, BF16),
        ],
        scratch_shapes=[pltpu.VMEM((seq, 2 * V_DIM), BF16), pltpu.VMEM((seq, V_DIM), BF16)],
        compiler_params=_params("parallel", "parallel"),
        name="diff_attn",
    )(score_bound, q, kt, v, q_norm2, lam_params, sub_norm, wgu2, wdn2)
    return o, wgu_bf.reshape(w_gu.shape), wdn_bf.reshape(w_down.shape)


def _nt_dot(a, b):
    return lax.dot_general(a, b, (((1,), (1,)), ((), ())), preferred_element_type=F32)


def _oproj_router_kernel(x_ref, o_ref, wo_ref, g_ref, rwt_ref,
                         x3_ref, h8_ref, gate_ref, route_ref, cnt_ref, carry,
                         *, tm, d, n_experts):
    i = pl.program_id(0)

    @pl.when(i == 0)
    def _():
        carry[...] = jnp.zeros_like(carry)

    x3 = x_ref[...] + _dot(o_ref[...], wo_ref[...].astype(BF16))
    x3_ref[...] = x3
    h = _rms_scale(x3) * g_ref[...]
    for c in range(d // LANES):
        h8_ref[pl.ds(c, tm, stride=d // LANES), :] = h[:, c * LANES:(c + 1) * LANES]

    rwt = rwt_ref[...]
    h_hi = h.astype(BF16)
    h_lo = (h - h_hi.astype(F32)).astype(BF16)
    w_hi = rwt.astype(BF16)
    w_lo = (rwt - w_hi.astype(F32)).astype(BF16)
    logits = _nt_dot(w_hi, h_hi) + (_nt_dot(w_lo, h_hi) + _nt_dot(w_hi, h_lo))

    rows = rwt.shape[0]
    sub = lax.broadcasted_iota(jnp.int32, (rows, tm), 0)
    lane = lax.broadcasted_iota(jnp.int32, (rows, tm), 1)
    neg = -jnp.inf
    l1 = jnp.where(sub < n_experts, logits, neg)
    m1 = jnp.max(l1, axis=0, keepdims=True)
    i1 = jnp.min(jnp.where(l1 == m1, sub, rows), axis=0, keepdims=True)
    l2 = jnp.where(sub == i1, neg, l1)
    m2 = jnp.max(l2, axis=0, keepdims=True)
    i2 = jnp.min(jnp.where(l2 == m2, sub, rows), axis=0, keepdims=True)
    e2 = jnp.exp(m2 - m1)
    den = 1.0 + e2
    gate_ref[...] = jnp.where(sub == 0, 1.0 / den, jnp.where(sub == 1, e2 / den, 0.0))

    oh1 = sub == i1
    oh2 = sub == i2
    cnt = jnp.where(oh1, 1.0, 0.0) + jnp.where(oh2, 1.0, 0.0)
    incl = cnt
    shift = 1
    while shift < tm:
        incl = incl + jnp.where(lane >= shift, pltpu.roll(incl, shift, 1), 0.0)
        shift *= 2
    before = incl - cnt + carry[:, 0:1]
    r1 = jnp.sum(jnp.where(oh1, before, 0.0), axis=0, keepdims=True)
    r2 = jnp.sum(jnp.where(oh2, before, 0.0), axis=0, keepdims=True)
    route = jnp.where(sub == 0, i1.astype(F32), jnp.where(sub == 1, i2.astype(F32),
                      jnp.where(sub == 2, r1, jnp.where(sub == 3, r2, 0.0))))
    route_ref[...] = route.astype(jnp.int32)
    total = carry[...] + jnp.sum(cnt, axis=1, keepdims=True)
    carry[...] = total
    cnt_ref[...] = total.astype(jnp.int32)


def _oproj_router(x2d, o, w_o, g, rw_t, n_experts):
    t, d = x2d.shape
    tm = PROJ_TILE
    const = lambda i: (0, 0)
    kern = functools.partial(_oproj_router_kernel, tm=tm, d=d, n_experts=n_experts)
    return pl.pallas_call(
        kern,
        grid=(t // tm,),
        in_specs=[
            pl.BlockSpec((tm, d), lambda i: (i, 0)),
            pl.BlockSpec((tm, d), lambda i: (i, 0)),
            _resident((d, d)),
            pl.BlockSpec((1, d), const),
            pl.BlockSpec((SUBLANES, d), const),
        ],
        out_specs=[
            pl.BlockSpec((tm, d), lambda i: (i, 0)),
            pl.BlockSpec((tm * (d // LANES), LANES), lambda i: (i, 0)),
            pl.BlockSpec((SUBLANES, tm), lambda i: (0, i)),
            pl.BlockSpec((SUBLANES, tm), lambda i: (0, i)),
            pl.BlockSpec((SUBLANES, LANES), const),
        ],
        out_shape=[
            jax.ShapeDtypeStruct((t, d), F32),
            jax.ShapeDtypeStruct((t * (d // LANES), LANES), F32),
            jax.ShapeDtypeStruct((SUBLANES, t), F32),
            jax.ShapeDtypeStruct((SUBLANES, t), jnp.int32),
            jax.ShapeDtypeStruct((SUBLANES, LANES), jnp.int32),
        ],
        scratch_shapes=[pltpu.VMEM((SUBLANES, LANES), F32)],
        compiler_params=_params("arbitrary"),
        name="oproj_router",
    )(x2d, o, w_o, g, rw_t)


def _pad_bits(tile):
    return [1 << b for b in reversed(range(int(math.log2(tile))))]


def _dispatch_kernel(s0_ref, s1_ref, pad_start_ref, pad_len_ref, h_ref, hs_hbm, zeros, sem, zsem,
                     *, tb, nc, n_experts, tile):
    i = pl.program_id(0)
    base = i * tb

    def rows(ref, first, count):
        start = first * nc
        if not isinstance(start, int):
            start = pl.multiple_of(start, nc)
        return ref.at[pl.ds(start, count * nc), :]

    def row_copy(r, slot):
        return pltpu.make_async_copy(rows(h_ref, r, 1), rows(hs_hbm, slot, 1), sem)

    def issue(g, carry):
        for u in range(DMA_UNROLL):
            r = g * DMA_UNROLL + u
            row_copy(r, s0_ref[base + r]).start(priority=0)
            row_copy(r, s1_ref[base + r]).start(priority=1)
        return carry

    lax.fori_loop(0, tb // DMA_UNROLL, issue, 0)

    def pad_copies(do):
        for e in range(n_experts):
            n = pad_len_ref[e]
            pos = pad_start_ref[e]
            for bit in _pad_bits(tile):
                @pl.when((n & bit) != 0)
                def _(pos=pos, bit=bit):
                    do(pltpu.make_async_copy(rows(zeros, 0, bit), rows(hs_hbm, pos, bit), zsem))
                pos = pos + (n & bit)
        for k in range(n_experts):
            @pl.when(k < pad_len_ref[n_experts])
            def _(k=k):
                pos = pad_start_ref[n_experts] + k * tile
                do(pltpu.make_async_copy(zeros, rows(hs_hbm, pos, tile), zsem))

    @pl.when(i == 0)
    def _():
        zeros[...] = jnp.zeros_like(zeros)
        pad_copies(lambda cp: cp.start())
        pad_copies(lambda cp: cp.wait())

    def drain(g, carry):
        for u in range(2 * DMA_UNROLL):
            row_copy(0, 0).wait()
        return carry

    lax.fori_loop(0, tb // DMA_UNROLL, drain, 0)


def _dispatch(slot0, slot1, pad_start, pad_len, h8, nc, n_slots, n_experts):
    t = h8.shape[0] // nc
    tb = DISPATCH_BLOCK
    kern = functools.partial(_dispatch_kernel, tb=tb, nc=nc, n_experts=n_experts, tile=GROUP_TILE)
    grid_spec = pltpu.PrefetchScalarGridSpec(
        num_scalar_prefetch=4,
        grid=(t // tb,),
        in_specs=[pl.BlockSpec((tb * nc, LANES), lambda i, *_: (i, 0))],
        out_specs=pl.BlockSpec(memory_space=pl.ANY),
        scratch_shapes=[
            pltpu.VMEM((GROUP_TILE * nc, LANES), F32),
            pltpu.SemaphoreType.DMA,
            pltpu.SemaphoreType.DMA,
        ],
    )
    return pl.pallas_call(
        kern,
        grid_spec=grid_spec,
        out_shape=jax.ShapeDtypeStruct((n_slots * nc, LANES), F32),
        compiler_params=_params("arbitrary"),
        name="moe_dispatch",
    )(slot0, slot1, pad_start, pad_len, h8)


def _moe_group_kernel(te_ref, nt_ref, valid_ref, hs_ref, wg_ref, wu_ref, wd_ref, ys_ref, *, tm, d, f_exp):
    j = pl.program_id(0)
    nc = d // LANES
    valid = valid_ref[j]

    def expert_ffn(rows):
        h = jnp.concatenate([hs_ref[pl.ds(c, rows, stride=nc), :].astype(BF16) for c in range(nc)], axis=1)
        y = None
        for lo in range(0, f_exp, MOE_F_CHUNK):
            hi = min(lo + MOE_F_CHUNK, f_exp)
            gate = _dot(h, wg_ref[:, lo:hi])
            up = _dot(h, wu_ref[:, lo:hi])
            a = (gate * jax.nn.sigmoid(gate) * up).astype(BF16)
            p = _dot(a, wd_ref[lo:hi, :])
            y = p if y is None else y + p
        for c in range(nc):
            ys_ref[pl.ds(c, rows, stride=nc), :] = y[:, c * LANES:(c + 1) * LANES]

    half = tm // 2

    @pl.when(valid > half)
    def _():
        expert_ffn(tm)

    @pl.when(jnp.logical_and(valid > 0, valid <= half))
    def _():
        expert_ffn(half)
        ys_ref[half * nc:, :] = jnp.zeros(((tm - half) * nc, LANES), F32)

    @pl.when(valid == 0)
    def _():
        ys_ref[...] = jnp.zeros_like(ys_ref)


def _moe_group(tile_expert, n_tiles, tile_valid, hs2d, w_gu, w_down, max_tiles):
    n_e, d, f2 = w_gu.shape
    f_exp = f2 // 2
    tm = GROUP_TILE
    nc = d // LANES

    def row_map(j, te, nt, valid):
        return (jnp.minimum(j, nt[0] - 1), 0)

    kern = functools.partial(_moe_group_kernel, tm=tm, d=d, f_exp=f_exp)
    grid_spec = pltpu.PrefetchScalarGridSpec(
        num_scalar_prefetch=3,
        grid=(max_tiles,),
        in_specs=[
            pl.BlockSpec((tm * nc, LANES), row_map),
            pl.BlockSpec((None, d, f_exp), lambda j, te, nt, valid: (te[j], 0, 0)),
            pl.BlockSpec((None, d, f_exp), lambda j, te, nt, valid: (te[j], 0, 1)),
            pl.BlockSpec((None, f_exp, d), lambda j, te, nt, valid: (te[j], 0, 0)),
        ],
        out_specs=pl.BlockSpec((tm * nc, LANES), lambda j, te, nt, valid: (j, 0)),
    )
    return pl.pallas_call(
        kern,
        grid_spec=grid_spec,
        out_shape=jax.ShapeDtypeStruct(hs2d.shape, F32),
        compiler_params=pltpu.CompilerParams(dimension_semantics=("arbitrary",),
                                             vmem_limit_bytes=MOE_VMEM_LIMIT),
        name="moe_group",
    )(tile_expert, n_tiles, tile_valid, hs2d, w_gu, w_gu, w_down)


def _combine_kernel(s0_ref, s1_ref, x_ref, gate_ref, ys_hbm, o_ref, b00, b01, b10, b11, sems, *, tm, d):
    i = pl.program_id(0)
    n = pl.num_programs(0)
    nc = d // LANES
    bufs = ((b00, b01), (b10, b11))

    def row_copy(slot, par, k, r):
        start = r * nc
        if not isinstance(start, int):
            start = pl.multiple_of(start, nc)
        src = ys_hbm.at[pl.ds(pl.multiple_of(slot * nc, nc), nc), :]
        return pltpu.make_async_copy(src, bufs[par][k].at[pl.ds(start, nc), :], sems.at[par])

    def issue(block, par):
        base = block * tm

        def body(g, carry):
            for u in range(DMA_UNROLL):
                r = g * DMA_UNROLL + u
                row_copy(s0_ref[base + r], par, 0, r).start(priority=0)
                row_copy(s1_ref[base + r], par, 1, r).start(priority=1)
            return carry

        lax.fori_loop(0, tm // DMA_UNROLL, body, 0)

    def drain(par):
        def body(g, carry):
            for u in range(DMA_UNROLL):
                row_copy(0, par, 0, 0).wait()
                row_copy(0, par, 1, 0).wait()
            return carry

        lax.fori_loop(0, tm // DMA_UNROLL, body, 0)

    @pl.when(i == 0)
    def _():
        issue(0, 0)

    for par in range(2):
        @pl.when(i % 2 == par)
        def _(par=par):
            @pl.when(i + 1 < n)
            def _():
                issue(i + 1, 1 - par)

            drain(par)
            gates = jnp.transpose(gate_ref[...])
            g0 = gates[:, 0:1]
            g1 = gates[:, 1:2]
            for c in range(nc):
                y0 = bufs[par][0][pl.ds(c, tm, stride=nc), :]
                y1 = bufs[par][1][pl.ds(c, tm, stride=nc), :]
                cols = slice(c * LANES, (c + 1) * LANES)
                o_ref[:, cols] = x_ref[:, cols] + (g0 * y0 + g1 * y1)


def _combine(slot0, slot1, x3, gates, ys2d):
    t, d = x3.shape
    tm = ROW_TILE
    nc = d // LANES
    grid_spec = pltpu.PrefetchScalarGridSpec(
        num_scalar_prefetch=2,
        grid=(t // tm,),
        in_specs=[
            pl.BlockSpec((tm, d), lambda i, s0, s1: (i, 0)),
            pl.BlockSpec((SUBLANES, tm), lambda i, s0, s1: (0, i)),
            pl.BlockSpec(memory_space=pl.ANY),
        ],
        out_specs=pl.BlockSpec((tm, d), lambda i, s0, s1: (i, 0)),
        scratch_shapes=[
            pltpu.VMEM((tm * nc, LANES), F32),
            pltpu.VMEM((tm * nc, LANES), F32),
            pltpu.VMEM((tm * nc, LANES), F32),
            pltpu.VMEM((tm * nc, LANES), F32),
            pltpu.SemaphoreType.DMA((2,)),
        ],
    )
    return pl.pallas_call(
        functools.partial(_combine_kernel, tm=tm, d=d),
        grid_spec=grid_spec,
        out_shape=jax.ShapeDtypeStruct((t, d), F32),
        compiler_params=_params("arbitrary"),
        name="moe_combine",
    )(slot0, slot1, x3, gates, ys2d)


def _routing_tables(route, counts, n_experts, max_tiles):
    tile = GROUP_TILE
    padded = ((counts + tile - 1) // tile) * tile
    ends = jnp.cumsum(padded)
    offs = ends - padded
    slots = []
    for k in range(TOP_K):
        expert, rank = route[k], route[TOP_K + k]
        base = jnp.zeros_like(expert)
        for e in range(n_experts):
            base = jnp.where(expert == e, offs[e], base)
        slots.append(base + rank)
    tile_ends = ends // tile
    n_tiles = tile_ends[-1]
    j = jnp.arange(max_tiles, dtype=jnp.int32)
    tile_expert = jnp.sum((jnp.minimum(j, n_tiles - 1)[:, None] >= tile_ends[None, :]).astype(jnp.int32), axis=-1)
    row_end = jnp.sum(jnp.where(tile_expert[:, None] == jnp.arange(n_experts, dtype=jnp.int32)[None, :],
                                (offs + counts)[None, :], 0), axis=-1)
    tile_valid = jnp.where(j < n_tiles, jnp.clip(row_end - j * tile, 0, tile), 0)
    pad_start = jnp.concatenate([offs + counts, ends[-1:]]).astype(jnp.int32)
    pad_len = jnp.concatenate([padded - counts, max_tiles - n_tiles.reshape(1)]).astype(jnp.int32)
    return (slots[0], slots[1], pad_start, pad_len,
            tile_expert.astype(jnp.int32), n_tiles.reshape(1).astype(jnp.int32), tile_valid.astype(jnp.int32))


def _lambda_init(layer_idx_1based):
    return 0.8 - 0.6 * math.exp(-0.3 * (layer_idx_1based - 1))


def kernel(x, ln_mix, ln_ffn, conv_w_in, conv_w, conv_w_out, ln_kv, w_kv, k_norm, attn_w_q, q_norm, lam_params,
           sub_norm, attn_w_o, ffn_w_gu, ffn_w_down, router_w, moe_w_gu, moe_w_down):
    batch, seq, d = x.shape
    t = batch * seq
    n_experts = router_w.shape[-1]
    assert ln_mix.shape[0] == 2 and conv_w_in.shape[0] == 1 and attn_w_q.shape[0] == 1
    assert seq % ROW_TILE == 0 and seq % ATTN_TILE == 0 and d % LANES == 0
    nc = d // LANES
    x2d = x.reshape(t, d)

    x1 = _mixer_a(x2d, ln_mix[0:1], conv_w_in[0], conv_w[0], conv_w_out[0], seq)
    x2 = _ffn_dense(x1, ln_ffn[0:1], ffn_w_gu[0], ffn_w_down[0])

    n_k = d
    w_kt = w_kv[:, :n_k].T
    w_v = w_kv[:, n_k:]
    q, kt, v = _qkv(x2, ln_mix[1:2], ln_kv.reshape(1, d), attn_w_q[0], w_kt, w_v,
                    k_norm.reshape(HEAD_DIM, 1))

    score_bound = (math.sqrt(HEAD_DIM) * jnp.max(jnp.abs(q_norm[0])) * jnp.max(jnp.abs(k_norm))).reshape(1)
    o, moe_gu_bf, moe_down_bf = _attention(score_bound, q, kt, v, jnp.tile(q_norm[0:1], (1, 2)), lam_params[0],
                                           sub_norm[0:1], moe_w_gu[0], moe_w_down[0], batch, seq, _lambda_init(2))

    assert n_experts <= SUBLANES
    rw_t = jnp.pad(router_w[0].T, ((0, SUBLANES - n_experts), (0, 0)))
    x3, h8, gates, route, counts = _oproj_router(x2, o, attn_w_o[0], ln_ffn[1:2], rw_t, n_experts)
    max_tiles = (TOP_K * t) // GROUP_TILE + n_experts
    n_slots = max_tiles * GROUP_TILE
    slot0, slot1, pad_start, pad_len, tile_expert, n_tiles, tile_valid = _routing_tables(
        route, counts[:n_experts, 0], n_experts, max_tiles)
    hs = _dispatch(slot0, slot1, pad_start, pad_len, h8, nc, n_slots, n_experts)
    ys = _moe_group(tile_expert, n_tiles, tile_valid, hs, moe_gu_bf, moe_down_bf, max_tiles)
    out = _combine(slot0, slot1, x3, gates, ys)
    return out.reshape(batch, seq, d)
```

```python
import functools
import math

import jax
import jax.numpy as jnp
from jax import lax
from jax.experimental import pallas as pl
from jax.experimental.pallas import tpu as pltpu

F32 = jnp.float32
BF16 = jnp.bfloat16

EPS = 1e-6
HEAD_DIM = 64
V_DIM = 2 * HEAD_DIM
CONV_WIDTH = 3
TOP_K = 2

LANES = 128
SUBLANES = 8
VMEM_LIMIT = 56 * 1024 * 1024

ROW_TILE = 512
PROJ_TILE = 1024
ATTN_TILE = 256
ATTN_HEADS_PER_STEP = 2
GROUP_TILE = 512
MOE_F_CHUNK = 512
MOE_VMEM_LIMIT = 62 * 1024 * 1024
DISPATCH_BLOCK = 2048
DMA_UNROLL = 16

LOG2E = 1.4426950408889634
SAFE_SHIFT = 40.0


def _rms_scale(x):
    return x * lax.rsqrt(jnp.mean(x * x, axis=-1, keepdims=True) + EPS)


def _dot(a, b):
    return jnp.dot(a, b, preferred_element_type=F32)


def _resident(shape):
    return pl.BlockSpec(shape, lambda *_: (0,) * len(shape), pipeline_mode=pl.Buffered(1))


def _params(*sem):
    return pltpu.CompilerParams(dimension_semantics=sem, vmem_limit_bytes=VMEM_LIMIT)


def _mixer_a_kernel(x_ref, g_ref, win_ref, cw_ref, wout_ref, o_ref, ubuf, *, tm, d, tiles_per_seq):
    i = pl.program_id(0)
    x = x_ref[...]
    h = (_rms_scale(x) * g_ref[...]).astype(BF16)
    c = _dot(h, win_ref[:, d:2 * d].astype(BF16))
    v = _dot(h, win_ref[:, 2 * d:].astype(BF16))
    u = c * v

    @pl.when(i % tiles_per_seq == 0)
    def _():
        ubuf[0:SUBLANES, :] = jnp.zeros((SUBLANES, d), F32)

    ubuf[SUBLANES:tm + SUBLANES, :] = u
    u1 = ubuf[SUBLANES - 1:tm + SUBLANES - 1, :]
    u2 = ubuf[SUBLANES - 2:tm + SUBLANES - 2, :]
    cw = cw_ref[...]
    z = u2 * cw[0:1] + u1 * cw[1:2] + u * cw[2:3]
    b = _dot(h, win_ref[:, 0:d].astype(BF16))
    y = (b * z).astype(BF16)
    o_ref[...] = x + _dot(y, wout_ref[...].astype(BF16))
    ubuf[0:SUBLANES, :] = ubuf[tm:tm + SUBLANES, :]


def _mixer_a(x2d, g, w_in, conv_w, w_out, seq):
    t, d = x2d.shape
    tm = ROW_TILE
    kern = functools.partial(_mixer_a_kernel, tm=tm, d=d, tiles_per_seq=seq // tm)
    return pl.pallas_call(
        kern,
        grid=(t // tm,),
        in_specs=[
            pl.BlockSpec((tm, d), lambda i: (i, 0)),
            pl.BlockSpec((1, d), lambda i: (0, 0)),
            _resident((d, 3 * d)),
            pl.BlockSpec((CONV_WIDTH, d), lambda i: (0, 0)),
            _resident((d, d)),
        ],
        out_specs=pl.BlockSpec((tm, d), lambda i: (i, 0)),
        out_shape=jax.ShapeDtypeStruct((t, d), F32),
        scratch_shapes=[pltpu.VMEM((tm + SUBLANES, d), F32)],
        compiler_params=_params("arbitrary"),
        name="mixer_a",
    )(x2d, g, w_in, conv_w, w_out)


def _ffn_chunks(f):
    step = 768
    return [(lo, min(lo + step, f)) for lo in range(0, f, step)]


def _ffn_kernel(x_ref, g_ref, wgu_ref, wd_ref, o_ref, *, f):
    x = x_ref[...]
    h = (_rms_scale(x) * g_ref[...]).astype(BF16)
    acc = x
    for lo, hi in _ffn_chunks(f):
        gate = _dot(h, wgu_ref[:, lo:hi].astype(BF16))
        up = _dot(h, wgu_ref[:, f + lo:f + hi].astype(BF16))
        a = (gate * jax.nn.sigmoid(gate) * up).astype(BF16)
        acc = acc + _dot(a, wd_ref[lo:hi, :].astype(BF16))
    o_ref[...] = acc


def _ffn_dense(x2d, g, w_gu, w_down):
    t, d = x2d.shape
    f = w_down.shape[0]
    tm = ROW_TILE
    return pl.pallas_call(
        functools.partial(_ffn_kernel, f=f),
        grid=(t // tm,),
        in_specs=[
            pl.BlockSpec((tm, d), lambda i: (i, 0)),
            pl.BlockSpec((1, d), lambda i: (0, 0)),
            _resident((d, 2 * f)),
            _resident((f, d)),
        ],
        out_specs=pl.BlockSpec((tm, d), lambda i: (i, 0)),
        out_shape=jax.ShapeDtypeStruct((t, d), F32),
        compiler_params=_params("parallel"),
        name="ffn_dense",
    )(x2d, g, w_gu, w_down)


def _qkv_kernel(x_ref, gq_ref, gkv_ref, wq_ref, wkt_ref, wv_ref, kn_ref, q_ref, kt_ref, v_ref, *, tm, d):
    y = _rms_scale(x_ref[...])
    hq = (y * gq_ref[...]).astype(BF16)
    hkv = (y * gkv_ref[...]).astype(BF16)
    q_ref[...] = _dot(hq, wq_ref[...].astype(BF16)).astype(BF16)
    v_ref[...] = _dot(hkv, wv_ref[...].astype(BF16)).astype(BF16)
    kt = lax.dot_general(wkt_ref[...].astype(BF16), hkv, (((1,), (1,)), ((), ())), preferred_element_type=F32)
    k3 = kt.reshape(d // HEAD_DIM, HEAD_DIM, tm)
    k3 = k3 * lax.rsqrt(jnp.mean(k3 * k3, axis=1, keepdims=True) + EPS) * kn_ref[...][None]
    kt_ref[...] = k3.reshape(d, tm).astype(BF16)


def _qkv(x2d, g_q, g_kv, w_q, w_kt, w_v, k_norm_col):
    t, d = x2d.shape
    tm = PROJ_TILE
    const = lambda i: (0, 0)
    return pl.pallas_call(
        functools.partial(_qkv_kernel, tm=tm, d=d),
        grid=(t // tm,),
        in_specs=[
            pl.BlockSpec((tm, d), lambda i: (i, 0)),
            pl.BlockSpec((1, d), const),
            pl.BlockSpec((1, d), const),
            _resident((d, d)),
            _resident((d, d)),
            _resident((d, d)),
            pl.BlockSpec((HEAD_DIM, 1), const),
        ],
        out_specs=[
            pl.BlockSpec((tm, d), lambda i: (i, 0)),
            pl.BlockSpec((d, tm), lambda i: (0, i)),
            pl.BlockSpec((tm, d), lambda i: (i, 0)),
        ],
        out_shape=[
            jax.ShapeDtypeStruct((t, d), BF16),
            jax.ShapeDtypeStruct((d, t), BF16),
            jax.ShapeDtypeStruct((t, d), BF16),
        ],
        compiler_params=_params("parallel"),
        name="qkv_proj",
    )(x2d, g_q, g_kv, w_q, w_kt, w_v, k_norm_col)


def _group_mean_sq(x, ones_bd, group):
    return _dot((x * x).astype(BF16), ones_bd) * (1.0 / group)


def _attn_kernel(bound_ref, q_ref, kt_ref, v_ref, qn_ref, lam_ref, sn_ref, wgu_ref, wdn_ref,
                 o_ref, wgu_bf_ref, wdn_bf_ref, v1_ref, qn_s, *, seq, tq, heads, lam_init):
    wgu_bf_ref[...] = wgu_ref[...].astype(BF16)
    wdn_bf_ref[...] = wdn_ref[...].astype(BF16)
    lp = lam_ref[...]
    lam = (jnp.exp(jnp.sum(lp[0:1] * lp[1:2], axis=-1, keepdims=True))
           - jnp.exp(jnp.sum(lp[2:3] * lp[3:4], axis=-1, keepdims=True)) + lam_init)
    row = lax.broadcasted_iota(jnp.int32, (tq, tq), 0)
    col = lax.broadcasted_iota(jnp.int32, (tq, tq), 1)
    causal = col <= row
    gi = lax.broadcasted_iota(jnp.int32, (V_DIM, V_DIM), 0) // HEAD_DIM
    gj = lax.broadcasted_iota(jnp.int32, (V_DIM, V_DIM), 1) // HEAD_DIM
    ones_bd = jnp.where(gi == gj, 1.0, 0.0).astype(BF16)
    qgain = qn_ref[...] * (HEAD_DIM ** -0.5 * LOG2E)
    maps = [slice(c * HEAD_DIM, (c + 1) * HEAD_DIM) for c in range(2)]
    n_q = seq // tq

    v1_ref[:, V_DIM:] = jnp.ones((seq, V_DIM), BF16)
    for hh in range(heads):
        hcols = slice(hh * V_DIM, (hh + 1) * V_DIM)
        hrow = hh * V_DIM
        v1_ref[:, 0:V_DIM] = v_ref[:, hcols]
        q = q_ref[:, hcols].astype(F32)
        ms = _group_mean_sq(q, ones_bd, HEAD_DIM)
        qn_s[...] = (q * lax.rsqrt(ms + EPS) * qgain).astype(BF16)

        def finish(q0, r1, r2, hcols=hcols):
            o = r1[:, 0:V_DIM] / r1[:, V_DIM:] - lam * (r2[:, 0:V_DIM] / r2[:, V_DIM:])
            o = _rms_scale(o) * sn_ref[...] * (1.0 - lam_init)
            o_ref[pl.ds(q0, tq), hcols] = o.astype(BF16)

        def kt_rows(sl, hrow=hrow):
            return slice(hrow + sl.start, hrow + sl.stop)

        @pl.when(bound_ref[0] <= SAFE_SHIFT)
        def _fixed_shift(finish=finish, kt_rows=kt_rows):
            shift = bound_ref[0] * LOG2E

            def scores(qi):
                q0 = qi * tq
                out = []
                for sl in maps:
                    qc = qn_s[q0:q0 + tq, sl]
                    s_diag = _dot(qc, kt_ref[kt_rows(sl), q0:q0 + tq])
                    s_low = _dot(qc, kt_ref[kt_rows(sl), 0:q0]) if q0 > 0 else None
                    out.append((s_diag, s_low))
                return out

            order = list(reversed(range(n_q)))
            pending = scores(order[0])
            for idx, qi in enumerate(order):
                q0 = qi * tq
                cur = pending
                if idx + 1 < n_q:
                    pending = scores(order[idx + 1])
                res = []
                for s_diag, s_low in cur:
                    p = jnp.where(causal, jnp.exp2(s_diag - shift), 0.0).astype(BF16)
                    r = _dot(p, v1_ref[q0:q0 + tq, :])
                    if s_low is not None:
                        r = r + _dot(jnp.exp2(s_low - shift).astype(BF16), v1_ref[0:q0, :])
                    res.append(r)
                finish(q0, *res)

        @pl.when(jnp.logical_not(bound_ref[0] <= SAFE_SHIFT))
        def _running_max(finish=finish, kt_rows=kt_rows):
            def q_body(qi, carry):
                q0 = pl.multiple_of(qi * tq, tq)
                qs = [qn_s[pl.ds(q0, tq), sl] for sl in maps]

                def tile(j, state, masked):
                    k0 = pl.multiple_of(j * tq, tq)
                    out = []
                    for c, sl in enumerate(maps):
                        m, r = state[c]
                        s = _dot(qs[c], kt_ref[kt_rows(sl), pl.ds(k0, tq)])
                        if masked:
                            s = jnp.where(causal, s, -jnp.inf)
                        m_new = jnp.maximum(m, jnp.max(s, axis=-1, keepdims=True))
                        p = jnp.exp2(s - m_new).astype(BF16)
                        r = jnp.exp2(m - m_new) * r + _dot(p, v1_ref[pl.ds(k0, tq), :])
                        out.append((m_new, r))
                    return tuple(out)

                one = (jnp.full((tq, 1), -jnp.inf, F32), jnp.zeros((tq, 2 * V_DIM), F32))
                state = lax.fori_loop(0, qi, lambda j, st: tile(j, st, False), (one, one))
                (_, r1), (_, r2) = tile(qi, state, True)
                finish(q0, r1, r2)
                return carry

            lax.fori_loop(0, n_q, q_body, 0)


def _attention(score_bound, q, kt, v, q_norm2, lam_params, sub_norm, w_gu, w_down, batch, seq, lam_init):
    t, d = q.shape
    n_heads = d // V_DIM
    hps = ATTN_HEADS_PER_STEP
    assert n_heads % hps == 0
    hsteps = n_heads // hps
    steps = batch * hsteps
    wgu2 = w_gu.reshape(-1, w_gu.shape[-1])
    wdn2 = w_down.reshape(-1, w_down.shape[-1])
    gu_rows, dn_rows = wgu2.shape[0] // steps, wdn2.shape[0] // steps
    assert wgu2.shape[0] % steps == 0 and wdn2.shape[0] % steps == 0 and gu_rows % 16 == 0 and dn_rows % 16 == 0
    slab = lambda b, h: (b * hsteps + h, 0)
    hw = hps * V_DIM
    kern = functools.partial(_attn_kernel, seq=seq, tq=ATTN_TILE, heads=hps, lam_init=lam_init)
    o, wgu_bf, wdn_bf = pl.pallas_call(
        kern,
        grid=(batch, hsteps),
        in_specs=[
            pl.BlockSpec(memory_space=pltpu.SMEM),
            pl.BlockSpec((seq, hw), lambda b, h: (b, h)),
            pl.BlockSpec((hw, seq), lambda b, h: (h, b)),
            pl.BlockSpec((seq, hw), lambda b, h: (b, h)),
            pl.BlockSpec((1, V_DIM), lambda b, h: (0, 0)),
            pl.BlockSpec((4, HEAD_DIM), lambda b, h: (0, 0)),
            pl.BlockSpec((1, V_DIM), lambda b, h: (0, 0)),
            pl.BlockSpec((gu_rows, wgu2.shape[1]), slab),
            pl.BlockSpec((dn_rows, wdn2.shape[1]), slab),
        ],
        out_specs=[
            pl.BlockSpec((seq, hw), lambda b, h: (b, h)),
            pl.BlockSpec((gu_rows, wgu2.shape[1]), slab),
            pl.BlockSpec((dn_rows, wdn2.shape[1]), slab),
        ],
        out_shape=[
            jax.ShapeDtypeStruct((t, d), BF16),
            jax.ShapeDtypeStruct(wgu2.shape, BF16),
            jax.ShapeDtypeStruct(wdn2.shape, BF16),
        ],
        scratch_shapes=[pltpu.VMEM((seq, 2 * V_DIM), BF16), pltpu.VMEM((seq, V_DIM), BF16)],
        compiler_params=_params("parallel", "parallel"),
        name="diff_attn",
    )(score_bound, q, kt, v, q_norm2, lam_params, sub_norm, wgu2, wdn2)
    return o, wgu_bf.reshape(w_gu.shape), wdn_bf.reshape(w_down.shape)


def _nt_dot(a, b):
    return lax.dot_general(a, b, (((1,), (1,)), ((), ())), preferred_element_type=F32)


def _oproj_router_kernel(x_ref, o_ref, wo_ref, g_ref, rwt_ref,
                         x3_ref, h8_ref, gate_ref, route_ref, cnt_ref, carry,
                         *, tm, d, n_experts):
    i = pl.program_id(0)

    @pl.when(i == 0)
    def _():
        carry[...] = jnp.zeros_like(carry)

    x3 = x_ref[...] + _dot(o_ref[...], wo_ref[...].astype(BF16))
    x3_ref[...] = x3
    h = _rms_scale(x3) * g_ref[...]
    for c in range(d // LANES):
        h8_ref[pl.ds(c, tm, stride=d // LANES), :] = h[:, c * LANES:(c + 1) * LANES]

    rwt = rwt_ref[...]
    h_hi = h.astype(BF16)
    h_lo = (h - h_hi.astype(F32)).astype(BF16)
    w_hi = rwt.astype(BF16)
    w_lo = (rwt - w_hi.astype(F32)).astype(BF16)
    logits = _nt_dot(w_hi, h_hi) + (_nt_dot(w_lo, h_hi) + _nt_dot(w_hi, h_lo))

    rows = rwt.shape[0]
    sub = lax.broadcasted_iota(jnp.int32, (rows, tm), 0)
    lane = lax.broadcasted_iota(jnp.int32, (rows, tm), 1)
    neg = -jnp.inf
    l1 = jnp.where(sub < n_experts, logits, neg)
    m1 = jnp.max(l1, axis=0, keepdims=True)
    i1 = jnp.min(jnp.where(l1 == m1, sub, rows), axis=0, keepdims=True)
    l2 = jnp.where(sub == i1, neg, l1)
    m2 = jnp.max(l2, axis=0, keepdims=True)
    i2 = jnp.min(jnp.where(l2 == m2, sub, rows), axis=0, keepdims=True)
    e2 = jnp.exp(m2 - m1)
    den = 1.0 + e2
    gate_ref[...] = jnp.where(sub == 0, 1.0 / den, jnp.where(sub == 1, e2 / den, 0.0))

    oh1 = sub == i1
    oh2 = sub == i2
    cnt = jnp.where(oh1, 1.0, 0.0) + jnp.where(oh2, 1.0, 0.0)
    incl = cnt
    shift = 1
    while shift < tm:
        incl = incl + jnp.where(lane >= shift, pltpu.roll(incl, shift, 1), 0.0)
        shift *= 2
    before = incl - cnt + carry[:, 0:1]
    r1 = jnp.sum(jnp.where(oh1, before, 0.0), axis=0, keepdims=True)
    r2 = jnp.sum(jnp.where(oh2, before, 0.0), axis=0, keepdims=True)
    route = jnp.where(sub == 0, i1.astype(F32), jnp.where(sub == 1, i2.astype(F32),
                      jnp.where(sub == 2, r1, jnp.where(sub == 3, r2, 0.0))))
    route_ref[...] = route.astype(jnp.int32)
    total = carry[...] + jnp.sum(cnt, axis=1, keepdims=True)
    carry[...] = total
    cnt_ref[...] = total.astype(jnp.int32)


def _oproj_router(x2d, o, w_o, g, rw_t, n_experts):
    t, d = x2d.shape
    tm = PROJ_TILE
    const = lambda i: (0, 0)
    kern = functools.partial(_oproj_router_kernel, tm=tm, d=d, n_experts=n_experts)
    return pl.pallas_call(
        kern,
        grid=(t // tm,),
        in_specs=[
            pl.BlockSpec((tm, d), lambda i: (i, 0)),
            pl.BlockSpec((tm, d), lambda i: (i, 0)),
            _resident((d, d)),
            pl.BlockSpec((1, d), const),
            pl.BlockSpec((SUBLANES, d), const),
        ],
        out_specs=[
            pl.BlockSpec((tm, d), lambda i: (i, 0)),
            pl.BlockSpec((tm * (d // LANES), LANES), lambda i: (i, 0)),
            pl.BlockSpec((SUBLANES, tm), lambda i: (0, i)),
            pl.BlockSpec((SUBLANES, tm), lambda i: (0, i)),
            pl.BlockSpec((SUBLANES, LANES), const),
        ],
        out_shape=[
            jax.ShapeDtypeStruct((t, d), F32),
            jax.ShapeDtypeStruct((t * (d // LANES), LANES), F32),
            jax.ShapeDtypeStruct((SUBLANES, t), F32),
            jax.ShapeDtypeStruct((SUBLANES, t), jnp.int32),
            jax.ShapeDtypeStruct((SUBLANES, LANES), jnp.int32),
        ],
        scratch_shapes=[pltpu.VMEM((SUBLANES, LANES), F32)],
        compiler_params=_params("arbitrary"),
        name="oproj_router",
    )(x2d, o, w_o, g, rw_t)


def _pad_bits(tile):
    return [1 << b for b in reversed(range(int(math.log2(tile))))]


def _dispatch_kernel(s0_ref, s1_ref, pad_start_ref, pad_len_ref, h_ref, hs_hbm, zeros, sem, zsem,
                     *, tb, nc, n_experts, tile):
    i = pl.program_id(0)
    base = i * tb

    def rows(ref, first, count):
        start = first * nc
        if not isinstance(start, int):
            start = pl.multiple_of(start, nc)
        return ref.at[pl.ds(start, count * nc), :]

    def row_copy(r, slot):
        return pltpu.make_async_copy(rows(h_ref, r, 1), rows(hs_hbm, slot, 1), sem)

    def issue(g, carry):
        for u in range(DMA_UNROLL):
            r = g * DMA_UNROLL + u
            row_copy(r, s0_ref[base + r]).start(priority=0)
            row_copy(r, s1_ref[base + r]).start(priority=1)
        return carry

    lax.fori_loop(0, tb // DMA_UNROLL, issue, 0)

    def pad_copies(do):
        for e in range(n_experts):
            n = pad_len_ref[e]
            pos = pad_start_ref[e]
            for bit in _pad_bits(tile):
                @pl.when((n & bit) != 0)
                def _(pos=pos, bit=bit):
                    do(pltpu.make_async_copy(rows(zeros, 0, bit), rows(hs_hbm, pos, bit), zsem))
                pos = pos + (n & bit)
        for k in range(n_experts):
            @pl.when(k < pad_len_ref[n_experts])
            def _(k=k):
                pos = pad_start_ref[n_experts] + k * tile
                do(pltpu.make_async_copy(zeros, rows(hs_hbm, pos, tile), zsem))

    @pl.when(i == 0)
    def _():
        zeros[...] = jnp.zeros_like(zeros)
        pad_copies(lambda cp: cp.start())
        pad_copies(lambda cp: cp.wait())

    def drain(g, carry):
        for u in range(2 * DMA_UNROLL):
            row_copy(0, 0).wait()
        return carry

    lax.fori_loop(0, tb // DMA_UNROLL, drain, 0)


def _dispatch(slot0, slot1, pad_start, pad_len, h8, nc, n_slots, n_experts):
    t = h8.shape[0] // nc
    tb = DISPATCH_BLOCK
    kern = functools.partial(_dispatch_kernel, tb=tb, nc=nc, n_experts=n_experts, tile=GROUP_TILE)
    grid_spec = pltpu.PrefetchScalarGridSpec(
        num_scalar_prefetch=4,
        grid=(t // tb,),
        in_specs=[pl.BlockSpec((tb * nc, LANES), lambda i, *_: (i, 0))],
        out_specs=pl.BlockSpec(memory_space=pl.ANY),
        scratch_shapes=[
            pltpu.VMEM((GROUP_TILE * nc, LANES), F32),
            pltpu.SemaphoreType.DMA,
            pltpu.SemaphoreType.DMA,
        ],
    )
    return pl.pallas_call(
        kern,
        grid_spec=grid_spec,
        out_shape=jax.ShapeDtypeStruct((n_slots * nc, LANES), F32),
        compiler_params=_params("arbitrary"),
        name="moe_dispatch",
    )(slot0, slot1, pad_start, pad_len, h8)


def _moe_group_kernel(te_ref, nt_ref, valid_ref, hs_ref, wg_ref, wu_ref, wd_ref, ys_ref, *, tm, d, f_exp):
    j = pl.program_id(0)
    nc = d // LANES
    valid = valid_ref[j]

    def expert_ffn(rows):
        h = jnp.concatenate([hs_ref[pl.ds(c, rows, stride=nc), :].astype(BF16) for c in range(nc)], axis=1)
        y = None
        for lo in range(0, f_exp, MOE_F_CHUNK):
            hi = min(lo + MOE_F_CHUNK, f_exp)
            gate = _dot(h, wg_ref[:, lo:hi])
            up = _dot(h, wu_ref[:, lo:hi])
            a = (gate * jax.nn.sigmoid(gate) * up).astype(BF16)
            p = _dot(a, wd_ref[lo:hi, :])
            y = p if y is None else y + p
        for c in range(nc):
            ys_ref[pl.ds(c, rows, stride=nc), :] = y[:, c * LANES:(c + 1) * LANES]

    half = tm // 2

    @pl.when(valid > half)
    def _():
        expert_ffn(tm)

    @pl.when(jnp.logical_and(valid > 0, valid <= half))
    def _():
        expert_ffn(half)
        ys_ref[half * nc:, :] = jnp.zeros(((tm - half) * nc, LANES), F32)

    @pl.when(valid == 0)
    def _():
        ys_ref[...] = jnp.zeros_like(ys_ref)


def _moe_group(tile_expert, n_tiles, tile_valid, hs2d, w_gu, w_down, max_tiles):
    n_e, d, f2 = w_gu.shape
    f_exp = f2 // 2
    tm = GROUP_TILE
    nc = d // LANES

    def row_map(j, te, nt, valid):
        return (jnp.minimum(j, nt[0] - 1), 0)

    kern = functools.partial(_moe_group_kernel, tm=tm, d=d, f_exp=f_exp)
    grid_spec = pltpu.PrefetchScalarGridSpec(
        num_scalar_prefetch=3,
        grid=(max_tiles,),
        in_specs=[
            pl.BlockSpec((tm * nc, LANES), row_map),
            pl.BlockSpec((None, d, f_exp), lambda j, te, nt, valid: (te[j], 0, 0)),
            pl.BlockSpec((None, d, f_exp), lambda j, te, nt, valid: (te[j], 0, 1)),
            pl.BlockSpec((None, f_exp, d), lambda j, te, nt, valid: (te[j], 0, 0)),
        ],
        out_specs=pl.BlockSpec((tm * nc, LANES), lambda j, te, nt, valid: (j, 0)),
    )
    return pl.pallas_call(
        kern,
        grid_spec=grid_spec,
        out_shape=jax.ShapeDtypeStruct(hs2d.shape, F32),
        compiler_params=pltpu.CompilerParams(dimension_semantics=("arbitrary",),
                                             vmem_limit_bytes=MOE_VMEM_LIMIT),
        name="moe_group",
    )(tile_expert, n_tiles, tile_valid, hs2d, w_gu, w_gu, w_down)


def _combine_kernel(s0_ref, s1_ref, x_ref, gate_ref, ys_hbm, o_ref, b00, b01, b10, b11, sems, *, tm, d):
    i = pl.program_id(0)
    n = pl.num_programs(0)
    nc = d // LANES
    bufs = ((b00, b01), (b10, b11))

    def row_copy(slot, par, k, r):
        start = r * nc
        if not isinstance(start, int):
            start = pl.multiple_of(start, nc)
        src = ys_hbm.at[pl.ds(pl.multiple_of(slot * nc, nc), nc), :]
        return pltpu.make_async_copy(src, bufs[par][k].at[pl.ds(start, nc), :], sems.at[par])

    def issue(block, par):
        base = block * tm

        def body(g, carry):
            for u in range(DMA_UNROLL):
                r = g * DMA_UNROLL + u
                row_copy(s0_ref[base + r], par, 0, r).start(priority=0)
                row_copy(s1_ref[base + r], par, 1, r).start(priority=1)
            return carry

        lax.fori_loop(0, tm // DMA_UNROLL, body, 0)

    def drain(par):
        def body(g, carry):
            for u in range(DMA_UNROLL):
                row_copy(0, par, 0, 0).wait()
                row_copy(0, par, 1, 0).wait()
            return carry

        lax.fori_loop(0, tm // DMA_UNROLL, body, 0)

    @pl.when(i == 0)
    def _():
        issue(0, 0)

    for par in range(2):
        @pl.when(i % 2 == par)
        def _(par=par):
            @pl.when(i + 1 < n)
            def _():
                issue(i + 1, 1 - par)

            drain(par)
            gates = jnp.transpose(gate_ref[...])
            g0 = gates[:, 0:1]
            g1 = gates[:, 1:2]
            for c in range(nc):
                y0 = bufs[par][0][pl.ds(c, tm, stride=nc), :]
                y1 = bufs[par][1][pl.ds(c, tm, stride=nc), :]
                cols = slice(c * LANES, (c + 1) * LANES)
                o_ref[:, cols] = x_ref[:, cols] + (g0 * y0 + g1 * y1)


def _combine(slot0, slot1, x3, gates, ys2d):
    t, d = x3.shape
    tm = PROJ_TILE
    nc = d // LANES
    grid_spec = pltpu.PrefetchScalarGridSpec(
        num_scalar_prefetch=2,
        grid=(t // tm,),
        in_specs=[
            pl.BlockSpec((tm, d), lambda i, s0, s1: (i, 0)),
            pl.BlockSpec((SUBLANES, tm), lambda i, s0, s1: (0, i)),
            pl.BlockSpec(memory_space=pl.ANY),
        ],
        out_specs=pl.BlockSpec((tm, d), lambda i, s0, s1: (i, 0)),
        scratch_shapes=[
            pltpu.VMEM((tm * nc, LANES), F32),
            pltpu.VMEM((tm * nc, LANES), F32),
            pltpu.VMEM((tm * nc, LANES), F32),
            pltpu.VMEM((tm * nc, LANES), F32),
            pltpu.SemaphoreType.DMA((2,)),
        ],
    )
    return pl.pallas_call(
        functools.partial(_combine_kernel, tm=tm, d=d),
        grid_spec=grid_spec,
        out_shape=jax.ShapeDtypeStruct((t, d), F32),
        compiler_params=_params("arbitrary"),
        name="moe_combine",
    )(slot0, slot1, x3, gates, ys2d)


def _routing_tables(route, counts, n_experts, max_tiles):
    tile = GROUP_TILE
    padded = ((counts + tile - 1) // tile) * tile
    ends = jnp.cumsum(padded)
    offs = ends - padded
    slots = []
    for k in range(TOP_K):
        expert, rank = route[k], route[TOP_K + k]
        base = jnp.zeros_like(expert)
        for e in range(n_experts):
            base = jnp.where(expert == e, offs[e], base)
        slots.append(base + rank)
    tile_ends = ends // tile
    n_tiles = tile_ends[-1]
    j = jnp.arange(max_tiles, dtype=jnp.int32)
    tile_expert = jnp.sum((jnp.minimum(j, n_tiles - 1)[:, None] >= tile_ends[None, :]).astype(jnp.int32), axis=-1)
    row_end = jnp.sum(jnp.where(tile_expert[:, None] == jnp.arange(n_experts, dtype=jnp.int32)[None, :],
                                (offs + counts)[None, :], 0), axis=-1)
    tile_valid = jnp.where(j < n_tiles, jnp.clip(row_end - j * tile, 0, tile), 0)
    pad_start = jnp.concatenate([offs + counts, ends[-1:]]).astype(jnp.int32)
    pad_len = jnp.concatenate([padded - counts, max_tiles - n_tiles.reshape(1)]).astype(jnp.int32)
    return (slots[0], slots[1], pad_start, pad_len,
            tile_expert.astype(jnp.int32), n_tiles.reshape(1).astype(jnp.int32), tile_valid.astype(jnp.int32))


def _lambda_init(layer_idx_1based):
    return 0.8 - 0.6 * math.exp(-0.3 * (layer_idx_1based - 1))


def kernel(x, ln_mix, ln_ffn, conv_w_in, conv_w, conv_w_out, ln_kv, w_kv, k_norm, attn_w_q, q_norm, lam_params,
           sub_norm, attn_w_o, ffn_w_gu, ffn_w_down, router_w, moe_w_gu, moe_w_down):
    batch, seq, d = x.shape
    t = batch * seq
    n_experts = router_w.shape[-1]
    assert ln_mix.shape[0] == 2 and conv_w_in.shape[0] == 1 and attn_w_q.shape[0] == 1
    assert seq % ROW_TILE == 0 and seq % ATTN_TILE == 0 and d % LANES == 0
    nc = d // LANES
    x2d = x.reshape(t, d)

    x1 = _mixer_a(x2d, ln_mix[0:1], conv_w_in[0], conv_w[0], conv_w_out[0], seq)
    x2 = _ffn_dense(x1, ln_ffn[0:1], ffn_w_gu[0], ffn_w_down[0])

    n_k = d
    w_kt = w_kv[:, :n_k].T
    w_v = w_kv[:, n_k:]
    q, kt, v = _qkv(x2, ln_mix[1:2], ln_kv.reshape(1, d), attn_w_q[0], w_kt, w_v,
                    k_norm.reshape(HEAD_DIM, 1))

    score_bound = (math.sqrt(HEAD_DIM) * jnp.max(jnp.abs(q_norm[0])) * jnp.max(jnp.abs(k_norm))).reshape(1)
    o, moe_gu_bf, moe_down_bf = _attention(score_bound, q, kt, v, jnp.tile(q_norm[0:1], (1, 2)), lam_params[0],
                                           sub_norm[0:1], moe_w_gu[0], moe_w_down[0], batch, seq, _lambda_init(2))

    assert n_experts <= SUBLANES
    rw_t = jnp.pad(router_w[0].T, ((0, SUBLANES - n_experts), (0, 0)))
    x3, h8, gates, route, counts = _oproj_router(x2, o, attn_w_o[0], ln_ffn[1:2], rw_t, n_experts)
    max_tiles = (TOP_K * t) // GROUP_TILE + n_experts
    n_slots = max_tiles * GROUP_TILE
    slot0, slot1, pad_start, pad_len, tile_expert, n_tiles, tile_valid = _routing_tables(
        route, counts[:n_experts, 0], n_experts, max_tiles)
    hs = _dispatch(slot0, slot1, pad_start, pad_len, h8, nc, n_slots, n_experts)
    ys = _moe_group(tile_expert, n_tiles, tile_valid, hs, moe_gu_bf, moe_down_bf, max_tiles)
    out = _combine(slot0, slot1, x3, gates, ys)
    return out.reshape(batch, seq, d)
```

```python
import functools
import math

import jax
import jax.numpy as jnp
from jax import lax
from jax.experimental import pallas as pl
from jax.experimental.pallas import tpu as pltpu

F32 = jnp.float32
BF16 = jnp.bfloat16

EPS = 1e-6
HEAD_DIM = 64
V_DIM = 2 * HEAD_DIM
CONV_WIDTH = 3
TOP_K = 2

LANES = 128
SUBLANES = 8
VMEM_LIMIT = 56 * 1024 * 1024

ROW_TILE = 512
PROJ_TILE = 1024
ATTN_TILE = 256
ATTN_HEADS_PER_STEP = 2
GROUP_TILE = 512
MOE_F_CHUNK = 512
MOE_VMEM_LIMIT = 62 * 1024 * 1024
DISPATCH_BLOCK = 4096
COMBINE_BLOCK = 256
DMA_UNROLL = 16

LOG2E = 1.4426950408889634
SAFE_SHIFT = 40.0


def _rms_scale(x):
    return x * lax.rsqrt(jnp.mean(x * x, axis=-1, keepdims=True) + EPS)


def _dot(a, b):
    return jnp.dot(a, b, preferred_element_type=F32)


def _resident(shape):
    return pl.BlockSpec(shape, lambda *_: (0,) * len(shape), pipeline_mode=pl.Buffered(1))


def _params(*sem):
    return pltpu.CompilerParams(dimension_semantics=sem, vmem_limit_bytes=VMEM_LIMIT)


def _mixer_a_kernel(x_ref, g_ref, win_ref, cw_ref, wout_ref, o_ref, ubuf, *, tm, d, tiles_per_seq):
    i = pl.program_id(0)
    x = x_ref[...]
    h = (_rms_scale(x) * g_ref[...]).astype(BF16)
    c = _dot(h, win_ref[:, d:2 * d].astype(BF16))
    v = _dot(h, win_ref[:, 2 * d:].astype(BF16))
    u = c * v

    @pl.when(i % tiles_per_seq == 0)
    def _():
        ubuf[0:SUBLANES, :] = jnp.zeros((SUBLANES, d), F32)

    ubuf[SUBLANES:tm + SUBLANES, :] = u
    u1 = ubuf[SUBLANES - 1:tm + SUBLANES - 1, :]
    u2 = ubuf[SUBLANES - 2:tm + SUBLANES - 2, :]
    cw = cw_ref[...]
    z = u2 * cw[0:1] + u1 * cw[1:2] + u * cw[2:3]
    b = _dot(h, win_ref[:, 0:d].astype(BF16))
    y = (b * z).astype(BF16)
    o_ref[...] = x + _dot(y, wout_ref[...].astype(BF16))
    ubuf[0:SUBLANES, :] = ubuf[tm:tm + SUBLANES, :]


def _mixer_a(x2d, g, w_in, conv_w, w_out, seq):
    t, d = x2d.shape
    tm = ROW_TILE
    kern = functools.partial(_mixer_a_kernel, tm=tm, d=d, tiles_per_seq=seq // tm)
    return pl.pallas_call(
        kern,
        grid=(t // tm,),
        in_specs=[
            pl.BlockSpec((tm, d), lambda i: (i, 0)),
            pl.BlockSpec((1, d), lambda i: (0, 0)),
            _resident((d, 3 * d)),
            pl.BlockSpec((CONV_WIDTH, d), lambda i: (0, 0)),
            _resident((d, d)),
        ],
        out_specs=pl.BlockSpec((tm, d), lambda i: (i, 0)),
        out_shape=jax.ShapeDtypeStruct((t, d), F32),
        scratch_shapes=[pltpu.VMEM((tm + SUBLANES, d), F32)],
        compiler_params=_params("arbitrary"),
        name="mixer_a",
    )(x2d, g, w_in, conv_w, w_out)


def _ffn_chunks(f):
    step = 768
    return [(lo, min(lo + step, f)) for lo in range(0, f, step)]


def _ffn_kernel(x_ref, g_ref, wgu_ref, wd_ref, o_ref, *, f):
    x = x_ref[...]
    h = (_rms_scale(x) * g_ref[...]).astype(BF16)
    acc = x
    for lo, hi in _ffn_chunks(f):
        gate = _dot(h, wgu_ref[:, lo:hi].astype(BF16))
        up = _dot(h, wgu_ref[:, f + lo:f + hi].astype(BF16))
        a = (gate * jax.nn.sigmoid(gate) * up).astype(BF16)
        acc = acc + _dot(a, wd_ref[lo:hi, :].astype(BF16))
    o_ref[...] = acc


def _ffn_dense(x2d, g, w_gu, w_down):
    t, d = x2d.shape
    f = w_down.shape[0]
    tm = ROW_TILE
    return pl.pallas_call(
        functools.partial(_ffn_kernel, f=f),
        grid=(t // tm,),
        in_specs=[
            pl.BlockSpec((tm, d), lambda i: (i, 0)),
            pl.BlockSpec((1, d), lambda i: (0, 0)),
            _resident((d, 2 * f)),
            _resident((f, d)),
        ],
        out_specs=pl.BlockSpec((tm, d), lambda i: (i, 0)),
        out_shape=jax.ShapeDtypeStruct((t, d), F32),
        compiler_params=_params("parallel"),
        name="ffn_dense",
    )(x2d, g, w_gu, w_down)


def _qkv_kernel(x_ref, gq_ref, gkv_ref, wq_ref, wkt_ref, wv_ref, kn_ref, q_ref, kt_ref, v_ref, *, tm, d):
    y = _rms_scale(x_ref[...])
    hq = (y * gq_ref[...]).astype(BF16)
    hkv = (y * gkv_ref[...]).astype(BF16)
    q_ref[...] = _dot(hq, wq_ref[...].astype(BF16)).astype(BF16)
    v_ref[...] = _dot(hkv, wv_ref[...].astype(BF16)).astype(BF16)
    kt = lax.dot_general(wkt_ref[...].astype(BF16), hkv, (((1,), (1,)), ((), ())), preferred_element_type=F32)
    k3 = kt.reshape(d // HEAD_DIM, HEAD_DIM, tm)
    k3 = k3 * lax.rsqrt(jnp.mean(k3 * k3, axis=1, keepdims=True) + EPS) * kn_ref[...][None]
    kt_ref[...] = k3.reshape(d, tm).astype(BF16)


def _qkv(x2d, g_q, g_kv, w_q, w_kt, w_v, k_norm_col):
    t, d = x2d.shape
    tm = PROJ_TILE
    const = lambda i: (0, 0)
    return pl.pallas_call(
        functools.partial(_qkv_kernel, tm=tm, d=d),
        grid=(t // tm,),
        in_specs=[
            pl.BlockSpec((tm, d), lambda i: (i, 0)),
            pl.BlockSpec((1, d), const),
            pl.BlockSpec((1, d), const),
            _resident((d, d)),
            _resident((d, d)),
            _resident((d, d)),
            pl.BlockSpec((HEAD_DIM, 1), const),
        ],
        out_specs=[
            pl.BlockSpec((tm, d), lambda i: (i, 0)),
            pl.BlockSpec((d, tm), lambda i: (0, i)),
            pl.BlockSpec((tm, d), lambda i: (i, 0)),
        ],
        out_shape=[
            jax.ShapeDtypeStruct((t, d), BF16),
            jax.ShapeDtypeStruct((d, t), BF16),
            jax.ShapeDtypeStruct((t, d), BF16),
        ],
        compiler_params=_params("parallel"),
        name="qkv_proj",
    )(x2d, g_q, g_kv, w_q, w_kt, w_v, k_norm_col)


def _group_mean_sq(x, ones_bd, group):
    return _dot((x * x).astype(BF16), ones_bd) * (1.0 / group)


def _attn_kernel(bound_ref, q_ref, kt_ref, v_ref, qn_ref, lam_ref, sn_ref, wgu_ref, wdn_ref,
                 o_ref, wgu_bf_ref, wdn_bf_ref, v1_ref, qn_s, *, seq, tq, heads, lam_init):
    wgu_bf_ref[...] = wgu_ref[...].astype(BF16)
    wdn_bf_ref[...] = wdn_ref[...].astype(BF16)
    lp = lam_ref[...]
    lam = (jnp.exp(jnp.sum(lp[0:1] * lp[1:2], axis=-1, keepdims=True))
           - jnp.exp(jnp.sum(lp[2:3] * lp[3:4], axis=-1, keepdims=True)) + lam_init)
    row = lax.broadcasted_iota(jnp.int32, (tq, tq), 0)
    col = lax.broadcasted_iota(jnp.int32, (tq, tq), 1)
    causal = col <= row
    gi = lax.broadcasted_iota(jnp.int32, (V_DIM, V_DIM), 0) // HEAD_DIM
    gj = lax.broadcasted_iota(jnp.int32, (V_DIM, V_DIM), 1) // HEAD_DIM
    ones_bd = jnp.where(gi == gj, 1.0, 0.0).astype(BF16)
    qgain = qn_ref[...] * (HEAD_DIM ** -0.5 * LOG2E)
    maps = [slice(c * HEAD_DIM, (c + 1) * HEAD_DIM) for c in range(2)]
    n_q = seq // tq

    v1_ref[:, V_DIM:] = jnp.ones((seq, V_DIM), BF16)
    for hh in range(heads):
        hcols = slice(hh * V_DIM, (hh + 1) * V_DIM)
        hrow = hh * V_DIM
        v1_ref[:, 0:V_DIM] = v_ref[:, hcols]
        q = q_ref[:, hcols].astype(F32)
        ms = _group_mean_sq(q, ones_bd, HEAD_DIM)
        qn_s[...] = (q * lax.rsqrt(ms + EPS) * qgain).astype(BF16)

        def finish(q0, r1, r2, hcols=hcols):
            o = r1[:, 0:V_DIM] / r1[:, V_DIM:] - lam * (r2[:, 0:V_DIM] / r2[:, V_DIM:])
            o = _rms_scale(o) * sn_ref[...] * (1.0 - lam_init)
            o_ref[pl.ds(q0, tq), hcols] = o.astype(BF16)

        def kt_rows(sl, hrow=hrow):
            return slice(hrow + sl.start, hrow + sl.stop)

        @pl.when(bound_ref[0] <= SAFE_SHIFT)
        def _fixed_shift(finish=finish, kt_rows=kt_rows):
            shift = bound_ref[0] * LOG2E

            def scores(qi):
                q0 = qi * tq
                out = []
                for sl in maps:
                    qc = qn_s[q0:q0 + tq, sl]
                    s_diag = _dot(qc, kt_ref[kt_rows(sl), q0:q0 + tq])
                    s_low = _dot(qc, kt_ref[kt_rows(sl), 0:q0]) if q0 > 0 else None
                    out.append((s_diag, s_low))
                return out

            order = list(reversed(range(n_q)))
            pending = scores(order[0])
            for idx, qi in enumerate(order):
                q0 = qi * tq
                cur = pending
                if idx + 1 < n_q:
                    pending = scores(order[idx + 1])
                res = []
                for s_diag, s_low in cur:
                    p = jnp.where(causal, jnp.exp2(s_diag - shift), 0.0).astype(BF16)
                    r = _dot(p, v1_ref[q0:q0 + tq, :])
                    if s_low is not None:
                        r = r + _dot(jnp.exp2(s_low - shift).astype(BF16), v1_ref[0:q0, :])
                    res.append(r)
                finish(q0, *res)

        @pl.when(jnp.logical_not(bound_ref[0] <= SAFE_SHIFT))
        def _running_max(finish=finish, kt_rows=kt_rows):
            def q_body(qi, carry):
                q0 = pl.multiple_of(qi * tq, tq)
                qs = [qn_s[pl.ds(q0, tq), sl] for sl in maps]

                def tile(j, state, masked):
                    k0 = pl.multiple_of(j * tq, tq)
                    out = []
                    for c, sl in enumerate(maps):
                        m, r = state[c]
                        s = _dot(qs[c], kt_ref[kt_rows(sl), pl.ds(k0, tq)])
                        if masked:
                            s = jnp.where(causal, s, -jnp.inf)
                        m_new = jnp.maximum(m, jnp.max(s, axis=-1, keepdims=True))
                        p = jnp.exp2(s - m_new).astype(BF16)
                        r = jnp.exp2(m - m_new) * r + _dot(p, v1_ref[pl.ds(k0, tq), :])
                        out.append((m_new, r))
                    return tuple(out)

                one = (jnp.full((tq, 1), -jnp.inf, F32), jnp.zeros((tq, 2 * V_DIM), F32))
                state = lax.fori_loop(0, qi, lambda j, st: tile(j, st, False), (one, one))
                (_, r1), (_, r2) = tile(qi, state, True)
                finish(q0, r1, r2)
                return carry

            lax.fori_loop(0, n_q, q_body, 0)


def _attention(score_bound, q, kt, v, q_norm2, lam_params, sub_norm, w_gu, w_down, batch, seq, lam_init):
    t, d = q.shape
    n_heads = d // V_DIM
    hps = ATTN_HEADS_PER_STEP
    assert n_heads % hps == 0
    hsteps = n_heads // hps
    steps = batch * hsteps
    wgu2 = w_gu.reshape(-1, w_gu.shape[-1])
    wdn2 = w_down.reshape(-1, w_down.shape[-1])
    gu_rows, dn_rows = wgu2.shape[0] // steps, wdn2.shape[0] // steps
    assert wgu2.shape[0] % steps == 0 and wdn2.shape[0] % steps == 0 and gu_rows % 16 == 0 and dn_rows % 16 == 0
    slab = lambda b, h: (b * hsteps + h, 0)
    hw = hps * V_DIM
    kern = functools.partial(_attn_kernel, seq=seq, tq=ATTN_TILE, heads=hps, lam_init=lam_init)
    o, wgu_bf, wdn_bf = pl.pallas_call(
        kern,
        grid=(batch, hsteps),
        in_specs=[
            pl.BlockSpec(memory_space=pltpu.SMEM),
            pl.BlockSpec((seq, hw), lambda b, h: (b, h)),
            pl.BlockSpec((hw, seq), lambda b, h: (h, b)),
            pl.BlockSpec((seq, hw), lambda b, h: (b, h)),
            pl.BlockSpec((1, V_DIM), lambda b, h: (0, 0)),
            pl.BlockSpec((4, HEAD_DIM), lambda b, h: (0, 0)),
            pl.BlockSpec((1, V_DIM), lambda b, h: (0, 0)),
            pl.BlockSpec((gu_rows, wgu2.shape[1]), slab),
            pl.BlockSpec((dn_rows, wdn2.shape[1]), slab),
        ],
        out_specs=[
            pl.BlockSpec((seq, hw), lambda b, h: (b, h)),
            pl.BlockSpec((gu_rows, wgu2.shape[1]), slab),
            pl.BlockSpec((dn_rows, wdn2.shape[1]), slab),
        ],
        out_shape=[
            jax.ShapeDtypeStruct((t, d), BF16),
            jax.ShapeDtypeStruct(wgu2.shape, BF16),
            jax.ShapeDtypeStruct(wdn2.shape, BF16),
        ],
        scratch_shapes=[pltpu.VMEM((seq, 2 * V_DIM), BF16), pltpu.VMEM((seq, V_DIM), BF16)],
        compiler_params=_params("parallel", "parallel"),
        name="diff_attn",
    )(score_bound, q, kt, v, q_norm2, lam_params, sub_norm, wgu2, wdn2)
    return o, wgu_bf.reshape(w_gu.shape), wdn_bf.reshape(w_down.shape)


def _nt_dot(a, b):
    return lax.dot_general(a, b, (((1,), (1,)), ((), ())), preferred_element_type=F32)


def _oproj_router_kernel(x_ref, o_ref, wo_ref, g_ref, rwt_ref,
                         x3_ref, h8_ref, gate_ref, route_ref, cnt_ref, carry,
                         *, tm, d, n_experts):
    i = pl.program_id(0)

    @pl.when(i == 0)
    def _():
        carry[...] = jnp.zeros_like(carry)

    x3 = x_ref[...] + _dot(o_ref[...], wo_ref[...].astype(BF16))
    x3_ref[...] = x3
    h = _rms_scale(x3) * g_ref[...]
    for c in range(d // LANES):
        h8_ref[pl.ds(c, tm, stride=d // LANES), :] = h[:, c * LANES:(c + 1) * LANES]

    rwt = rwt_ref[...]
    h_hi = h.astype(BF16)
    h_lo = (h - h_hi.astype(F32)).astype(BF16)
    w_hi = rwt.astype(BF16)
    w_lo = (rwt - w_hi.astype(F32)).astype(BF16)
    logits = _nt_dot(w_hi, h_hi) + (_nt_dot(w_lo, h_hi) + _nt_dot(w_hi, h_lo))

    rows = rwt.shape[0]
    sub = lax.broadcasted_iota(jnp.int32, (rows, tm), 0)
    lane = lax.broadcasted_iota(jnp.int32, (rows, tm), 1)
    neg = -jnp.inf
    l1 = jnp.where(sub < n_experts, logits, neg)
    m1 = jnp.max(l1, axis=0, keepdims=True)
    i1 = jnp.min(jnp.where(l1 == m1, sub, rows), axis=0, keepdims=True)
    l2 = jnp.where(sub == i1, neg, l1)
    m2 = jnp.max(l2, axis=0, keepdims=True)
    i2 = jnp.min(jnp.where(l2 == m2, sub, rows), axis=0, keepdims=True)
    e2 = jnp.exp(m2 - m1)
    den = 1.0 + e2
    gate_ref[...] = jnp.where(sub == 0, 1.0 / den, jnp.where(sub == 1, e2 / den, 0.0))

    oh1 = sub == i1
    oh2 = sub == i2
    cnt = jnp.where(oh1, 1.0, 0.0) + jnp.where(oh2, 1.0, 0.0)
    incl = cnt
    shift = 1
    while shift < tm:
        incl = incl + jnp.where(lane >= shift, pltpu.roll(incl, shift, 1), 0.0)
        shift *= 2
    before = incl - cnt + carry[:, 0:1]
    r1 = jnp.sum(jnp.where(oh1, before, 0.0), axis=0, keepdims=True)
    r2 = jnp.sum(jnp.where(oh2, before, 0.0), axis=0, keepdims=True)
    route = jnp.where(sub == 0, i1.astype(F32), jnp.where(sub == 1, i2.astype(F32),
                      jnp.where(sub == 2, r1, jnp.where(sub == 3, r2, 0.0))))
    route_ref[...] = route.astype(jnp.int32)
    total = carry[...] + jnp.sum(cnt, axis=1, keepdims=True)
    carry[...] = total
    cnt_ref[...] = total.astype(jnp.int32)


def _oproj_router(x2d, o, w_o, g, rw_t, n_experts):
    t, d = x2d.shape
    tm = PROJ_TILE
    const = lambda i: (0, 0)
    kern = functools.partial(_oproj_router_kernel, tm=tm, d=d, n_experts=n_experts)
    return pl.pallas_call(
        kern,
        grid=(t // tm,),
        in_specs=[
            pl.BlockSpec((tm, d), lambda i: (i, 0)),
            pl.BlockSpec((tm, d), lambda i: (i, 0)),
            _resident((d, d)),
            pl.BlockSpec((1, d), const),
            pl.BlockSpec((SUBLANES, d), const),
        ],
        out_specs=[
            pl.BlockSpec((tm, d), lambda i: (i, 0)),
            pl.BlockSpec((tm * (d // LANES), LANES), lambda i: (i, 0)),
            pl.BlockSpec((SUBLANES, tm), lambda i: (0, i)),
            pl.BlockSpec((SUBLANES, tm), lambda i: (0, i)),
            pl.BlockSpec((SUBLANES, LANES), const),
        ],
        out_shape=[
            jax.ShapeDtypeStruct((t, d), F32),
            jax.ShapeDtypeStruct((t * (d // LANES), LANES), F32),
            jax.ShapeDtypeStruct((SUBLANES, t), F32),
            jax.ShapeDtypeStruct((SUBLANES, t), jnp.int32),
            jax.ShapeDtypeStruct((SUBLANES, LANES), jnp.int32),
        ],
        scratch_shapes=[pltpu.VMEM((SUBLANES, LANES), F32)],
        compiler_params=_params("arbitrary"),
        name="oproj_router",
    )(x2d, o, w_o, g, rw_t)


def _pad_bits(tile):
    return [1 << b for b in reversed(range(int(math.log2(tile))))]


def _dispatch_kernel(s0_ref, s1_ref, pad_start_ref, pad_len_ref, h_ref, hs_hbm, zeros, sem, zsem,
                     *, tb, nc, n_experts, tile):
    i = pl.program_id(0)
    base = i * tb

    def rows(ref, first, count):
        start = first * nc
        if not isinstance(start, int):
            start = pl.multiple_of(start, nc)
        return ref.at[pl.ds(start, count * nc), :]

    def row_copy(r, slot):
        return pltpu.make_async_copy(rows(h_ref, r, 1), rows(hs_hbm, slot, 1), sem)

    def issue(g, carry):
        for u in range(DMA_UNROLL):
            r = g * DMA_UNROLL + u
            row_copy(r, s0_ref[base + r]).start(priority=0)
            row_copy(r, s1_ref[base + r]).start(priority=1)
        return carry

    lax.fori_loop(0, tb // DMA_UNROLL, issue, 0)

    def pad_copies(do):
        for e in range(n_experts):
            n = pad_len_ref[e]
            pos = pad_start_ref[e]
            for bit in _pad_bits(tile):
                @pl.when((n & bit) != 0)
                def _(pos=pos, bit=bit):
                    do(pltpu.make_async_copy(rows(zeros, 0, bit), rows(hs_hbm, pos, bit), zsem))
                pos = pos + (n & bit)
        for k in range(n_experts):
            @pl.when(k < pad_len_ref[n_experts])
            def _(k=k):
                pos = pad_start_ref[n_experts] + k * tile
                do(pltpu.make_async_copy(zeros, rows(hs_hbm, pos, tile), zsem))

    @pl.when(i == 0)
    def _():
        zeros[...] = jnp.zeros_like(zeros)
        pad_copies(lambda cp: cp.start())
        pad_copies(lambda cp: cp.wait())

    def drain(g, carry):
        for u in range(2 * DMA_UNROLL):
            row_copy(0, 0).wait()
        return carry

    lax.fori_loop(0, tb // DMA_UNROLL, drain, 0)


def _dispatch(slot0, slot1, pad_start, pad_len, h8, nc, n_slots, n_experts):
    t = h8.shape[0] // nc
    tb = DISPATCH_BLOCK
    kern = functools.partial(_dispatch_kernel, tb=tb, nc=nc, n_experts=n_experts, tile=GROUP_TILE)
    grid_spec = pltpu.PrefetchScalarGridSpec(
        num_scalar_prefetch=4,
        grid=(t // tb,),
        in_specs=[pl.BlockSpec((tb * nc, LANES), lambda i, *_: (i, 0))],
        out_specs=pl.BlockSpec(memory_space=pl.ANY),
        scratch_shapes=[
            pltpu.VMEM((GROUP_TILE * nc, LANES), F32),
            pltpu.SemaphoreType.DMA,
            pltpu.SemaphoreType.DMA,
        ],
    )
    return pl.pallas_call(
        kern,
        grid_spec=grid_spec,
        out_shape=jax.ShapeDtypeStruct((n_slots * nc, LANES), F32),
        compiler_params=_params("arbitrary"),
        name="moe_dispatch",
    )(slot0, slot1, pad_start, pad_len, h8)


def _moe_group_kernel(te_ref, nt_ref, valid_ref, hs_ref, wg_ref, wu_ref, wd_ref, ys_ref, *, tm, d, f_exp):
    j = pl.program_id(0)
    nc = d // LANES
    valid = valid_ref[j]

    def expert_ffn(rows):
        h = jnp.concatenate([hs_ref[pl.ds(c, rows, stride=nc), :].astype(BF16) for c in range(nc)], axis=1)
        y = None
        for lo in range(0, f_exp, MOE_F_CHUNK):
            hi = min(lo + MOE_F_CHUNK, f_exp)
            gate = _dot(h, wg_ref[:, lo:hi])
            up = _dot(h, wu_ref[:, lo:hi])
            a = (gate * jax.nn.sigmoid(gate) * up).astype(BF16)
            p = _dot(a, wd_ref[lo:hi, :])
            y = p if y is None else y + p
        for c in range(nc):
            ys_ref[pl.ds(c, rows, stride=nc), :] = y[:, c * LANES:(c + 1) * LANES]

    half = tm // 2

    @pl.when(valid > half)
    def _():
        expert_ffn(tm)

    @pl.when(jnp.logical_and(valid > 0, valid <= half))
    def _():
        expert_ffn(half)
        ys_ref[half * nc:, :] = jnp.zeros(((tm - half) * nc, LANES), F32)

    @pl.when(valid == 0)
    def _():
        ys_ref[...] = jnp.zeros_like(ys_ref)


def _moe_group(tile_expert, n_tiles, tile_valid, hs2d, w_gu, w_down, max_tiles):
    n_e, d, f2 = w_gu.shape
    f_exp = f2 // 2
    tm = GROUP_TILE
    nc = d // LANES

    def row_map(j, te, nt, valid):
        return (jnp.minimum(j, nt[0] - 1), 0)

    kern = functools.partial(_moe_group_kernel, tm=tm, d=d, f_exp=f_exp)
    grid_spec = pltpu.PrefetchScalarGridSpec(
        num_scalar_prefetch=3,
        grid=(max_tiles,),
        in_specs=[
            pl.BlockSpec((tm * nc, LANES), row_map),
            pl.BlockSpec((None, d, f_exp), lambda j, te, nt, valid: (te[j], 0, 0)),
            pl.BlockSpec((None, d, f_exp), lambda j, te, nt, valid: (te[j], 0, 1)),
            pl.BlockSpec((None, f_exp, d), lambda j, te, nt, valid: (te[j], 0, 0)),
        ],
        out_specs=pl.BlockSpec((tm * nc, LANES), lambda j, te, nt, valid: (j, 0)),
    )
    return pl.pallas_call(
        kern,
        grid_spec=grid_spec,
        out_shape=jax.ShapeDtypeStruct(hs2d.shape, F32),
        compiler_params=pltpu.CompilerParams(dimension_semantics=("arbitrary",),
                                             vmem_limit_bytes=MOE_VMEM_LIMIT),
        name="moe_group",
    )(tile_expert, n_tiles, tile_valid, hs2d, w_gu, w_gu, w_down)


def _combine_kernel(s0_ref, s1_ref, x_ref, gate_ref, ys_hbm, o_ref, b00, b01, b10, b11, sems, *, tm, d):
    i = pl.program_id(0)
    n = pl.num_programs(0)
    nc = d // LANES
    bufs = ((b00, b01), (b10, b11))

    def row_copy(slot, par, k, r):
        start = r * nc
        if not isinstance(start, int):
            start = pl.multiple_of(start, nc)
        src = ys_hbm.at[pl.ds(pl.multiple_of(slot * nc, nc), nc), :]
        return pltpu.make_async_copy(src, bufs[par][k].at[pl.ds(start, nc), :], sems.at[par])

    def issue(block, par):
        base = block * tm

        def body(g, carry):
            for u in range(DMA_UNROLL):
                r = g * DMA_UNROLL + u
                row_copy(s0_ref[base + r], par, 0, r).start(priority=0)
                row_copy(s1_ref[base + r], par, 1, r).start(priority=1)
            return carry

        lax.fori_loop(0, tm // DMA_UNROLL, body, 0)

    def drain(par):
        def body(g, carry):
            for u in range(DMA_UNROLL):
                row_copy(0, par, 0, 0).wait()
                row_copy(0, par, 1, 0).wait()
            return carry

        lax.fori_loop(0, tm // DMA_UNROLL, body, 0)

    @pl.when(i == 0)
    def _():
        issue(0, 0)

    for par in range(2):
        @pl.when(i % 2 == par)
        def _(par=par):
            @pl.when(i + 1 < n)
            def _():
                issue(i + 1, 1 - par)

            drain(par)
            gates = jnp.transpose(gate_ref[...])
            g0 = gates[:, 0:1]
            g1 = gates[:, 1:2]
            for c in range(nc):
                y0 = bufs[par][0][pl.ds(c, tm, stride=nc), :]
                y1 = bufs[par][1][pl.ds(c, tm, stride=nc), :]
                cols = slice(c * LANES, (c + 1) * LANES)
                o_ref[:, cols] = x_ref[:, cols] + (g0 * y0 + g1 * y1)


def _combine(slot0, slot1, x3, gates, ys2d):
    t, d = x3.shape
    tm = COMBINE_BLOCK
    nc = d // LANES
    grid_spec = pltpu.PrefetchScalarGridSpec(
        num_scalar_prefetch=2,
        grid=(t // tm,),
        in_specs=[
            pl.BlockSpec((tm, d), lambda i, s0, s1: (i, 0)),
            pl.BlockSpec((SUBLANES, tm), lambda i, s0, s1: (0, i)),
            pl.BlockSpec(memory_space=pl.ANY),
        ],
        out_specs=pl.BlockSpec((tm, d), lambda i, s0, s1: (i, 0)),
        scratch_shapes=[
            pltpu.VMEM((tm * nc, LANES), F32),
            pltpu.VMEM((tm * nc, LANES), F32),
            pltpu.VMEM((tm * nc, LANES), F32),
            pltpu.VMEM((tm * nc, LANES), F32),
            pltpu.SemaphoreType.DMA((2,)),
        ],
    )
    return pl.pallas_call(
        functools.partial(_combine_kernel, tm=tm, d=d),
        grid_spec=grid_spec,
        out_shape=jax.ShapeDtypeStruct((t, d), F32),
        compiler_params=_params("arbitrary"),
        name="moe_combine",
    )(slot0, slot1, x3, gates, ys2d)


def _routing_tables(route, counts, n_experts, max_tiles):
    tile = GROUP_TILE
    padded = ((counts + tile - 1) // tile) * tile
    ends = jnp.cumsum(padded)
    offs = ends - padded
    slots = []
    for k in range(TOP_K):
        expert, rank = route[k], route[TOP_K + k]
        base = jnp.zeros_like(expert)
        for e in range(n_experts):
            base = jnp.where(expert == e, offs[e], base)
        slots.append(base + rank)
    tile_ends = ends // tile
    n_tiles = tile_ends[-1]
    j = jnp.arange(max_tiles, dtype=jnp.int32)
    tile_expert = jnp.sum((jnp.minimum(j, n_tiles - 1)[:, None] >= tile_ends[None, :]).astype(jnp.int32), axis=-1)
    row_end = jnp.sum(jnp.where(tile_expert[:, None] == jnp.arange(n_experts, dtype=jnp.int32)[None, :],
                                (offs + counts)[None, :], 0), axis=-1)
    tile_valid = jnp.where(j < n_tiles, jnp.clip(row_end - j * tile, 0, tile), 0)
    pad_start = jnp.concatenate([offs + counts, ends[-1:]]).astype(jnp.int32)
    pad_len = jnp.concatenate([padded - counts, max_tiles - n_tiles.reshape(1)]).astype(jnp.int32)
    return (slots[0], slots[1], pad_start, pad_len,
            tile_expert.astype(jnp.int32), n_tiles.reshape(1).astype(jnp.int32), tile_valid.astype(jnp.int32))


def _lambda_init(layer_idx_1based):
    return 0.8 - 0.6 * math.exp(-0.3 * (layer_idx_1based - 1))


def kernel(x, ln_mix, ln_ffn, conv_w_in, conv_w, conv_w_out, ln_kv, w_kv, k_norm, attn_w_q, q_norm, lam_params,
           sub_norm, attn_w_o, ffn_w_gu, ffn_w_down, router_w, moe_w_gu, moe_w_down):
    batch, seq, d = x.shape
    t = batch * seq
    n_experts = router_w.shape[-1]
    assert ln_mix.shape[0] == 2 and conv_w_in.shape[0] == 1 and attn_w_q.shape[0] == 1
    assert seq % ROW_TILE == 0 and seq % ATTN_TILE == 0 and d % LANES == 0
    nc = d // LANES
    x2d = x.reshape(t, d)

    x1 = _mixer_a(x2d, ln_mix[0:1], conv_w_in[0], conv_w[0], conv_w_out[0], seq)
    x2 = _ffn_dense(x1, ln_ffn[0:1], ffn_w_gu[0], ffn_w_down[0])

    n_k = d
    w_kt = w_kv[:, :n_k].T
    w_v = w_kv[:, n_k:]
    q, kt, v = _qkv(x2, ln_mix[1:2], ln_kv.reshape(1, d), attn_w_q[0], w_kt, w_v,
                    k_norm.reshape(HEAD_DIM, 1))

    score_bound = (math.sqrt(HEAD_DIM) * jnp.max(jnp.abs(q_norm[0])) * jnp.max(jnp.abs(k_norm))).reshape(1)
    o, moe_gu_bf, moe_down_bf = _attention(score_bound, q, kt, v, jnp.tile(q_norm[0:1], (1, 2)), lam_params[0],
                                           sub_norm[0:1], moe_w_gu[0], moe_w_down[0], batch, seq, _lambda_init(2))

    assert n_experts <= SUBLANES
    rw_t = jnp.pad(router_w[0].T, ((0, SUBLANES - n_experts), (0, 0)))
    x3, h8, gates, route, counts = _oproj_router(x2, o, attn_w_o[0], ln_ffn[1:2], rw_t, n_experts)
    max_tiles = (TOP_K * t) // GROUP_TILE + n_experts
    n_slots = max_tiles * GROUP_TILE
    slot0, slot1, pad_start, pad_len, tile_expert, n_tiles, tile_valid = _routing_tables(
        route, counts[:n_experts, 0], n_experts, max_tiles)
    hs = _dispatch(slot0, slot1, pad_start, pad_len, h8, nc, n_slots, n_experts)
    ys = _moe_group(tile_expert, n_tiles, tile_valid, hs, moe_gu_bf, moe_down_bf, max_tiles)
    out = _combine(slot0, slot1, x3, gates, ys)
    return out.reshape(batch, seq, d)
```

```python
import functools
import math

import jax
import jax.numpy as jnp
from jax import lax
from jax.experimental import pallas as pl
from jax.experimental.pallas import tpu as pltpu

F32 = jnp.float32
BF16 = jnp.bfloat16

EPS = 1e-6
HEAD_DIM = 64
V_DIM = 2 * HEAD_DIM
CONV_WIDTH = 3
TOP_K = 2

LANES = 128
SUBLANES = 8
VMEM_LIMIT = 56 * 1024 * 1024

ROW_TILE = 512
PROJ_TILE = 1024
ATTN_TILE = 256
ATTN_HEADS_PER_STEP = 2
GROUP_TILE = 512
MOE_F_CHUNK = 512
MOE_VMEM_LIMIT = 62 * 1024 * 1024
DISPATCH_BLOCK = 2048
COMBINE_BLOCK = 128
DMA_UNROLL = 16

LOG2E = 1.4426950408889634
SAFE_SHIFT = 40.0


def _rms_scale(x):
    return x * lax.rsqrt(jnp.mean(x * x, axis=-1, keepdims=True) + EPS)


def _dot(a, b):
    return jnp.dot(a, b, preferred_element_type=F32)


def _resident(shape):
    return pl.BlockSpec(shape, lambda *_: (0,) * len(shape), pipeline_mode=pl.Buffered(1))


def _params(*sem):
    return pltpu.CompilerParams(dimension_semantics=sem, vmem_limit_bytes=VMEM_LIMIT)


def _mixer_a_kernel(x_ref, g_ref, win_ref, cw_ref, wout_ref, o_ref, ubuf, *, tm, d, tiles_per_seq):
    i = pl.program_id(0)
    x = x_ref[...]
    h = (_rms_scale(x) * g_ref[...]).astype(BF16)
    c = _dot(h, win_ref[:, d:2 * d].astype(BF16))
    v = _dot(h, win_ref[:, 2 * d:].astype(BF16))
    u = c * v

    @pl.when(i % tiles_per_seq == 0)
    def _():
        ubuf[0:SUBLANES, :] = jnp.zeros((SUBLANES, d), F32)

    ubuf[SUBLANES:tm + SUBLANES, :] = u
    u1 = ubuf[SUBLANES - 1:tm + SUBLANES - 1, :]
    u2 = ubuf[SUBLANES - 2:tm + SUBLANES - 2, :]
    cw = cw_ref[...]
    z = u2 * cw[0:1] + u1 * cw[1:2] + u * cw[2:3]
    b = _dot(h, win_ref[:, 0:d].astype(BF16))
    y = (b * z).astype(BF16)
    o_ref[...] = x + _dot(y, wout_ref[...].astype(BF16))
    ubuf[0:SUBLANES, :] = ubuf[tm:tm + SUBLANES, :]


def _mixer_a(x2d, g, w_in, conv_w, w_out, seq):
    t, d = x2d.shape
    tm = ROW_TILE
    kern = functools.partial(_mixer_a_kernel, tm=tm, d=d, tiles_per_seq=seq // tm)
    return pl.pallas_call(
        kern,
        grid=(t // tm,),
        in_specs=[
            pl.BlockSpec((tm, d), lambda i: (i, 0)),
            pl.BlockSpec((1, d), lambda i: (0, 0)),
            _resident((d, 3 * d)),
            pl.BlockSpec((CONV_WIDTH, d), lambda i: (0, 0)),
            _resident((d, d)),
        ],
        out_specs=pl.BlockSpec((tm, d), lambda i: (i, 0)),
        out_shape=jax.ShapeDtypeStruct((t, d), F32),
        scratch_shapes=[pltpu.VMEM((tm + SUBLANES, d), F32)],
        compiler_params=_params("arbitrary"),
        name="mixer_a",
    )(x2d, g, w_in, conv_w, w_out)


def _ffn_chunks(f):
    step = 768
    return [(lo, min(lo + step, f)) for lo in range(0, f, step)]


def _ffn_kernel(x_ref, g_ref, wgu_ref, wd_ref, o_ref, *, f):
    x = x_ref[...]
    h = (_rms_scale(x) * g_ref[...]).astype(BF16)
    acc = x
    for lo, hi in _ffn_chunks(f):
        gate = _dot(h, wgu_ref[:, lo:hi].astype(BF16))
        up = _dot(h, wgu_ref[:, f + lo:f + hi].astype(BF16))
        a = (gate * jax.nn.sigmoid(gate) * up).astype(BF16)
        acc = acc + _dot(a, wd_ref[lo:hi, :].astype(BF16))
    o_ref[...] = acc


def _ffn_dense(x2d, g, w_gu, w_down):
    t, d = x2d.shape
    f = w_down.shape[0]
    tm = ROW_TILE
    return pl.pallas_call(
        functools.partial(_ffn_kernel, f=f),
        grid=(t // tm,),
        in_specs=[
            pl.BlockSpec((tm, d), lambda i: (i, 0)),
            pl.BlockSpec((1, d), lambda i: (0, 0)),
            _resident((d, 2 * f)),
            _resident((f, d)),
        ],
        out_specs=pl.BlockSpec((tm, d), lambda i: (i, 0)),
        out_shape=jax.ShapeDtypeStruct((t, d), F32),
        compiler_params=_params("parallel"),
        name="ffn_dense",
    )(x2d, g, w_gu, w_down)


def _qkv_kernel(x_ref, gq_ref, gkv_ref, wq_ref, wkt_ref, wv_ref, kn_ref, q_ref, kt_ref, v_ref, *, tm, d):
    y = _rms_scale(x_ref[...])
    hq = (y * gq_ref[...]).astype(BF16)
    hkv = (y * gkv_ref[...]).astype(BF16)
    q_ref[...] = _dot(hq, wq_ref[...].astype(BF16)).astype(BF16)
    v_ref[...] = _dot(hkv, wv_ref[...].astype(BF16)).astype(BF16)
    kt = lax.dot_general(wkt_ref[...].astype(BF16), hkv, (((1,), (1,)), ((), ())), preferred_element_type=F32)
    k3 = kt.reshape(d // HEAD_DIM, HEAD_DIM, tm)
    k3 = k3 * lax.rsqrt(jnp.mean(k3 * k3, axis=1, keepdims=True) + EPS) * kn_ref[...][None]
    kt_ref[...] = k3.reshape(d, tm).astype(BF16)


def _qkv(x2d, g_q, g_kv, w_q, w_kt, w_v, k_norm_col):
    t, d = x2d.shape
    tm = PROJ_TILE
    const = lambda i: (0, 0)
    return pl.pallas_call(
        functools.partial(_qkv_kernel, tm=tm, d=d),
        grid=(t // tm,),
        in_specs=[
            pl.BlockSpec((tm, d), lambda i: (i, 0)),
            pl.BlockSpec((1, d), const),
            pl.BlockSpec((1, d), const),
            _resident((d, d)),
            _resident((d, d)),
            _resident((d, d)),
            pl.BlockSpec((HEAD_DIM, 1), const),
        ],
        out_specs=[
            pl.BlockSpec((tm, d), lambda i: (i, 0)),
            pl.BlockSpec((d, tm), lambda i: (0, i)),
            pl.BlockSpec((tm, d), lambda i: (i, 0)),
        ],
        out_shape=[
            jax.ShapeDtypeStruct((t, d), BF16),
            jax.ShapeDtypeStruct((d, t), BF16),
            jax.ShapeDtypeStruct((t, d), BF16),
        ],
        compiler_params=_params("parallel"),
        name="qkv_proj",
    )(x2d, g_q, g_kv, w_q, w_kt, w_v, k_norm_col)


def _group_mean_sq(x, ones_bd, group):
    return _dot((x * x).astype(BF16), ones_bd) * (1.0 / group)


def _attn_kernel(bound_ref, q_ref, kt_ref, v_ref, qn_ref, lam_ref, sn_ref, wgu_ref, wdn_ref,
                 o_ref, wgu_bf_ref, wdn_bf_ref, v1_ref, qn_s, *, seq, tq, heads, lam_init):
    wgu_bf_ref[...] = wgu_ref[...].astype(BF16)
    wdn_bf_ref[...] = wdn_ref[...].astype(BF16)
    lp = lam_ref[...]
    lam = (jnp.exp(jnp.sum(lp[0:1] * lp[1:2], axis=-1, keepdims=True))
           - jnp.exp(jnp.sum(lp[2:3] * lp[3:4], axis=-1, keepdims=True)) + lam_init)
    row = lax.broadcasted_iota(jnp.int32, (tq, tq), 0)
    col = lax.broadcasted_iota(jnp.int32, (tq, tq), 1)
    causal = col <= row
    gi = lax.broadcasted_iota(jnp.int32, (V_DIM, V_DIM), 0) // HEAD_DIM
    gj = lax.broadcasted_iota(jnp.int32, (V_DIM, V_DIM), 1) // HEAD_DIM
    ones_bd = jnp.where(gi == gj, 1.0, 0.0).astype(BF16)
    qgain = qn_ref[...] * (HEAD_DIM ** -0.5 * LOG2E)
    maps = [slice(c * HEAD_DIM, (c + 1) * HEAD_DIM) for c in range(2)]
    n_q = seq // tq

    v1_ref[:, V_DIM:] = jnp.ones((seq, V_DIM), BF16)
    for hh in range(heads):
        hcols = slice(hh * V_DIM, (hh + 1) * V_DIM)
        hrow = hh * V_DIM
        v1_ref[:, 0:V_DIM] = v_ref[:, hcols]
        q = q_ref[:, hcols].astype(F32)
        ms = _group_mean_sq(q, ones_bd, HEAD_DIM)
        qn_s[...] = (q * lax.rsqrt(ms + EPS) * qgain).astype(BF16)

        def finish(q0, r1, r2, hcols=hcols):
            o = r1[:, 0:V_DIM] / r1[:, V_DIM:] - lam * (r2[:, 0:V_DIM] / r2[:, V_DIM:])
            o = _rms_scale(o) * sn_ref[...] * (1.0 - lam_init)
            o_ref[pl.ds(q0, tq), hcols] = o.astype(BF16)

        def kt_rows(sl, hrow=hrow):
            return slice(hrow + sl.start, hrow + sl.stop)

        @pl.when(bound_ref[0] <= SAFE_SHIFT)
        def _fixed_shift(finish=finish, kt_rows=kt_rows):
            shift = bound_ref[0] * LOG2E

            def scores(qi):
                q0 = qi * tq
                out = []
                for sl in maps:
                    qc = qn_s[q0:q0 + tq, sl]
                    s_diag = _dot(qc, kt_ref[kt_rows(sl), q0:q0 + tq])
                    s_low = _dot(qc, kt_ref[kt_rows(sl), 0:q0]) if q0 > 0 else None
                    out.append((s_diag, s_low))
                return out

            order = list(reversed(range(n_q)))
            pending = scores(order[0])
            for idx, qi in enumerate(order):
                q0 = qi * tq
                cur = pending
                if idx + 1 < n_q:
                    pending = scores(order[idx + 1])
                res = []
                for s_diag, s_low in cur:
                    p = jnp.where(causal, jnp.exp2(s_diag - shift), 0.0).astype(BF16)
                    r = _dot(p, v1_ref[q0:q0 + tq, :])
                    if s_low is not None:
                        r = r + _dot(jnp.exp2(s_low - shift).astype(BF16), v1_ref[0:q0, :])
                    res.append(r)
                finish(q0, *res)

        @pl.when(jnp.logical_not(bound_ref[0] <= SAFE_SHIFT))
        def _running_max(finish=finish, kt_rows=kt_rows):
            def q_body(qi, carry):
                q0 = pl.multiple_of(qi * tq, tq)
                qs = [qn_s[pl.ds(q0, tq), sl] for sl in maps]

                def tile(j, state, masked):
                    k0 = pl.multiple_of(j * tq, tq)
                    out = []
                    for c, sl in enumerate(maps):
                        m, r = state[c]
                        s = _dot(qs[c], kt_ref[kt_rows(sl), pl.ds(k0, tq)])
                        if masked:
                            s = jnp.where(causal, s, -jnp.inf)
                        m_new = jnp.maximum(m, jnp.max(s, axis=-1, keepdims=True))
                        p = jnp.exp2(s - m_new).astype(BF16)
                        r = jnp.exp2(m - m_new) * r + _dot(p, v1_ref[pl.ds(k0, tq), :])
                        out.append((m_new, r))
                    return tuple(out)

                one = (jnp.full((tq, 1), -jnp.inf, F32), jnp.zeros((tq, 2 * V_DIM), F32))
                state = lax.fori_loop(0, qi, lambda j, st: tile(j, st, False), (one, one))
                (_, r1), (_, r2) = tile(qi, state, True)
                finish(q0, r1, r2)
                return carry

            lax.fori_loop(0, n_q, q_body, 0)


def _attention(score_bound, q, kt, v, q_norm2, lam_params, sub_norm, w_gu, w_down, batch, seq, lam_init):
    t, d = q.shape
    n_heads = d // V_DIM
    hps = ATTN_HEADS_PER_STEP
    assert n_heads % hps == 0
    hsteps = n_heads // hps
    steps = batch * hsteps
    wgu2 = w_gu.reshape(-1, w_gu.shape[-1])
    wdn2 = w_down.reshape(-1, w_down.shape[-1])
    gu_rows, dn_rows = wgu2.shape[0] // steps, wdn2.shape[0] // steps
    assert wgu2.shape[0] % steps == 0 and wdn2.shape[0] % steps == 0 and gu_rows % 16 == 0 and dn_rows % 16 == 0
    slab = lambda b, h: (b * hsteps + h, 0)
    hw = hps * V_DIM
    kern = functools.partial(_attn_kernel, seq=seq, tq=ATTN_TILE, heads=hps, lam_init=lam_init)
    o, wgu_bf, wdn_bf = pl.pallas_call(
        kern,
        grid=(batch, hsteps),
        in_specs=[
            pl.BlockSpec(memory_space=pltpu.SMEM),
            pl.BlockSpec((seq, hw), lambda b, h: (b, h)),
            pl.BlockSpec((hw, seq), lambda b, h: (h, b)),
            pl.BlockSpec((seq, hw), lambda b, h: (b, h)),
            pl.BlockSpec((1, V_DIM), lambda b, h: (0, 0)),
            pl.BlockSpec((4, HEAD_DIM), lambda b, h: (0, 0)),
            pl.BlockSpec((1, V_DIM), lambda b, h: (0, 0)),
            pl.BlockSpec((gu_rows, wgu2.shape[1]), slab),
            pl.BlockSpec((dn_rows, wdn2.shape[1]), slab),
        ],
        out_specs=[
            pl.BlockSpec((seq, hw), lambda b, h: (b, h)),
            pl.BlockSpec((gu_rows, wgu2.shape[1]), slab),
            pl.BlockSpec((dn_rows, wdn2.shape[1]), slab),
        ],
        out_shape=[
            jax.ShapeDtypeStruct((t, d), BF16),
            jax.ShapeDtypeStruct(wgu2.shape, BF16),
            jax.ShapeDtypeStruct(wdn2.shape, BF16),
        ],
        scratch_shapes=[pltpu.VMEM((seq, 2 * V_DIM), BF16), pltpu.VMEM((seq, V_DIM), BF16)],
        compiler_params=_params("parallel", "parallel"),
        name="diff_attn",
    )(score_bound, q, kt, v, q_norm2, lam_params, sub_norm, wgu2, wdn2)
    return o, wgu_bf.reshape(w_gu.shape), wdn_bf.reshape(w_down.shape)


def _nt_dot(a, b):
    return lax.dot_general(a, b, (((1,), (1,)), ((), ())), preferred_element_type=F32)


def _oproj_router_kernel(x_ref, o_ref, wo_ref, g_ref, rwt_ref,
                         x3_ref, h8_ref, gate_ref, route_ref, cnt_ref, carry,
                         *, tm, d, n_experts):
    i = pl.program_id(0)

    @pl.when(i == 0)
    def _():
        carry[...] = jnp.zeros_like(carry)

    x3 = x_ref[...] + _dot(o_ref[...], wo_ref[...].astype(BF16))
    x3_ref[...] = x3
    h = _rms_scale(x3) * g_ref[...]
    for c in range(d // LANES):
        h8_ref[pl.ds(c, tm, stride=d // LANES), :] = h[:, c * LANES:(c + 1) * LANES]

    rwt = rwt_ref[...]
    h_hi = h.astype(BF16)
    h_lo = (h - h_hi.astype(F32)).astype(BF16)
    w_hi = rwt.astype(BF16)
    w_lo = (rwt - w_hi.astype(F32)).astype(BF16)
    logits = _nt_dot(w_hi, h_hi) + (_nt_dot(w_lo, h_hi) + _nt_dot(w_hi, h_lo))

    rows = rwt.shape[0]
    sub = lax.broadcasted_iota(jnp.int32, (rows, tm), 0)
    lane = lax.broadcasted_iota(jnp.int32, (rows, tm), 1)
    neg = -jnp.inf
    l1 = jnp.where(sub < n_experts, logits, neg)
    m1 = jnp.max(l1, axis=0, keepdims=True)
    i1 = jnp.min(jnp.where(l1 == m1, sub, rows), axis=0, keepdims=True)
    l2 = jnp.where(sub == i1, neg, l1)
    m2 = jnp.max(l2, axis=0, keepdims=True)
    i2 = jnp.min(jnp.where(l2 == m2, sub, rows), axis=0, keepdims=True)
    e2 = jnp.exp(m2 - m1)
    den = 1.0 + e2
    gate_ref[...] = jnp.where(sub == 0, 1.0 / den, jnp.where(sub == 1, e2 / den, 0.0))

    oh1 = sub == i1
    oh2 = sub == i2
    cnt = jnp.where(oh1, 1.0, 0.0) + jnp.where(oh2, 1.0, 0.0)
    incl = cnt
    shift = 1
    while shift < tm:
        incl = incl + jnp.where(lane >= shift, pltpu.roll(incl, shift, 1), 0.0)
        shift *= 2
    before = incl - cnt + carry[:, 0:1]
    r1 = jnp.sum(jnp.where(oh1, before, 0.0), axis=0, keepdims=True)
    r2 = jnp.sum(jnp.where(oh2, before, 0.0), axis=0, keepdims=True)
    route = jnp.where(sub == 0, i1.astype(F32), jnp.where(sub == 1, i2.astype(F32),
                      jnp.where(sub == 2, r1, jnp.where(sub == 3, r2, 0.0))))
    route_ref[...] = route.astype(jnp.int32)
    total = carry[...] + jnp.sum(cnt, axis=1, keepdims=True)
    carry[...] = total
    cnt_ref[...] = total.astype(jnp.int32)


def _oproj_router(x2d, o, w_o, g, rw_t, n_experts):
    t, d = x2d.shape
    tm = PROJ_TILE
    const = lambda i: (0, 0)
    kern = functools.partial(_oproj_router_kernel, tm=tm, d=d, n_experts=n_experts)
    return pl.pallas_call(
        kern,
        grid=(t // tm,),
        in_specs=[
            pl.BlockSpec((tm, d), lambda i: (i, 0)),
            pl.BlockSpec((tm, d), lambda i: (i, 0)),
            _resident((d, d)),
            pl.BlockSpec((1, d), const),
            pl.BlockSpec((SUBLANES, d), const),
        ],
        out_specs=[
            pl.BlockSpec((tm, d), lambda i: (i, 0)),
            pl.BlockSpec((tm * (d // LANES), LANES), lambda i: (i, 0)),
            pl.BlockSpec((SUBLANES, tm), lambda i: (0, i)),
            pl.BlockSpec((SUBLANES, tm), lambda i: (0, i)),
            pl.BlockSpec((SUBLANES, LANES), const),
        ],
        out_shape=[
            jax.ShapeDtypeStruct((t, d), F32),
            jax.ShapeDtypeStruct((t * (d // LANES), LANES), F32),
            jax.ShapeDtypeStruct((SUBLANES, t), F32),
            jax.ShapeDtypeStruct((SUBLANES, t), jnp.int32),
            jax.ShapeDtypeStruct((SUBLANES, LANES), jnp.int32),
        ],
        scratch_shapes=[pltpu.VMEM((SUBLANES, LANES), F32)],
        compiler_params=_params("arbitrary"),
        name="oproj_router",
    )(x2d, o, w_o, g, rw_t)


def _pad_bits(tile):
    return [1 << b for b in reversed(range(int(math.log2(tile))))]


def _dispatch_kernel(s0_ref, s1_ref, pad_start_ref, pad_len_ref, h_ref, hs_hbm, zeros, sem, zsem,
                     *, tb, nc, n_experts, tile):
    i = pl.program_id(0)
    base = i * tb

    def rows(ref, first, count):
        start = first * nc
        if not isinstance(start, int):
            start = pl.multiple_of(start, nc)
        return ref.at[pl.ds(start, count * nc), :]

    def row_copy(r, slot):
        return pltpu.make_async_copy(rows(h_ref, r, 1), rows(hs_hbm, slot, 1), sem)

    def issue(g, carry):
        for u in range(DMA_UNROLL):
            r = g * DMA_UNROLL + u
            row_copy(r, s0_ref[base + r]).start(priority=0)
            row_copy(r, s1_ref[base + r]).start(priority=1)
        return carry

    lax.fori_loop(0, tb // DMA_UNROLL, issue, 0)

    def pad_copies(do):
        for e in range(n_experts):
            n = pad_len_ref[e]
            pos = pad_start_ref[e]
            for bit in _pad_bits(tile):
                @pl.when((n & bit) != 0)
                def _(pos=pos, bit=bit):
                    do(pltpu.make_async_copy(rows(zeros, 0, bit), rows(hs_hbm, pos, bit), zsem))
                pos = pos + (n & bit)
        for k in range(n_experts):
            @pl.when(k < pad_len_ref[n_experts])
            def _(k=k):
                pos = pad_start_ref[n_experts] + k * tile
                do(pltpu.make_async_copy(zeros, rows(hs_hbm, pos, tile), zsem))

    @pl.when(i == 0)
    def _():
        zeros[...] = jnp.zeros_like(zeros)
        pad_copies(lambda cp: cp.start())
        pad_copies(lambda cp: cp.wait())

    def drain(g, carry):
        for u in range(2 * DMA_UNROLL):
            row_copy(0, 0).wait()
        return carry

    lax.fori_loop(0, tb // DMA_UNROLL, drain, 0)


def _dispatch(slot0, slot1, pad_start, pad_len, h8, nc, n_slots, n_experts):
    t = h8.shape[0] // nc
    tb = DISPATCH_BLOCK
    kern = functools.partial(_dispatch_kernel, tb=tb, nc=nc, n_experts=n_experts, tile=GROUP_TILE)
    grid_spec = pltpu.PrefetchScalarGridSpec(
        num_scalar_prefetch=4,
        grid=(t // tb,),
        in_specs=[pl.BlockSpec((tb * nc, LANES), lambda i, *_: (i, 0))],
        out_specs=pl.BlockSpec(memory_space=pl.ANY),
        scratch_shapes=[
            pltpu.VMEM((GROUP_TILE * nc, LANES), F32),
            pltpu.SemaphoreType.DMA,
            pltpu.SemaphoreType.DMA,
        ],
    )
    return pl.pallas_call(
        kern,
        grid_spec=grid_spec,
        out_shape=jax.ShapeDtypeStruct((n_slots * nc, LANES), F32),
        compiler_params=_params("arbitrary"),
        name="moe_dispatch",
    )(slot0, slot1, pad_start, pad_len, h8)


def _moe_group_kernel(te_ref, nt_ref, valid_ref, hs_ref, wg_ref, wu_ref, wd_ref, ys_ref, *, tm, d, f_exp):
    j = pl.program_id(0)
    nc = d // LANES
    valid = valid_ref[j]

    def expert_ffn(rows):
        h = jnp.concatenate([hs_ref[pl.ds(c, rows, stride=nc), :].astype(BF16) for c in range(nc)], axis=1)
        y = None
        for lo in range(0, f_exp, MOE_F_CHUNK):
            hi = min(lo + MOE_F_CHUNK, f_exp)
            gate = _dot(h, wg_ref[:, lo:hi])
            up = _dot(h, wu_ref[:, lo:hi])
            a = (gate * jax.nn.sigmoid(gate) * up).astype(BF16)
            p = _dot(a, wd_ref[lo:hi, :])
            y = p if y is None else y + p
        for c in range(nc):
            ys_ref[pl.ds(c, rows, stride=nc), :] = y[:, c * LANES:(c + 1) * LANES]

    half = tm // 2

    @pl.when(valid > half)
    def _():
        expert_ffn(tm)

    @pl.when(jnp.logical_and(valid > 0, valid <= half))
    def _():
        expert_ffn(half)
        ys_ref[half * nc:, :] = jnp.zeros(((tm - half) * nc, LANES), F32)

    @pl.when(valid == 0)
    def _():
        ys_ref[...] = jnp.zeros_like(ys_ref)


def _moe_group(tile_expert, n_tiles, tile_valid, hs2d, w_gu, w_down, max_tiles):
    n_e, d, f2 = w_gu.shape
    f_exp = f2 // 2
    tm = GROUP_TILE
    nc = d // LANES

    def row_map(j, te, nt, valid):
        return (jnp.minimum(j, nt[0] - 1), 0)

    kern = functools.partial(_moe_group_kernel, tm=tm, d=d, f_exp=f_exp)
    grid_spec = pltpu.PrefetchScalarGridSpec(
        num_scalar_prefetch=3,
        grid=(max_tiles,),
        in_specs=[
            pl.BlockSpec((tm * nc, LANES), row_map),
            pl.BlockSpec((None, d, f_exp), lambda j, te, nt, valid: (te[j], 0, 0)),
            pl.BlockSpec((None, d, f_exp), lambda j, te, nt, valid: (te[j], 0, 1)),
            pl.BlockSpec((None, f_exp, d), lambda j, te, nt, valid: (te[j], 0, 0)),
        ],
        out_specs=pl.BlockSpec((tm * nc, LANES), lambda j, te, nt, valid: (j, 0)),
    )
    return pl.pallas_call(
        kern,
        grid_spec=grid_spec,
        out_shape=jax.ShapeDtypeStruct(hs2d.shape, F32),
        compiler_params=pltpu.CompilerParams(dimension_semantics=("arbitrary",),
                                             vmem_limit_bytes=MOE_VMEM_LIMIT),
        name="moe_group",
    )(tile_expert, n_tiles, tile_valid, hs2d, w_gu, w_gu, w_down)


def _combine_kernel(s0_ref, s1_ref, x_ref, gate_ref, ys_hbm, o_ref, b00, b01, b10, b11, sems, *, tm, d):
    i = pl.program_id(0)
    n = pl.num_programs(0)
    nc = d // LANES
    bufs = ((b00, b01), (b10, b11))

    def row_copy(slot, par, k, r):
        start = r * nc
        if not isinstance(start, int):
            start = pl.multiple_of(start, nc)
        src = ys_hbm.at[pl.ds(pl.multiple_of(slot * nc, nc), nc), :]
        return pltpu.make_async_copy(src, bufs[par][k].at[pl.ds(start, nc), :], sems.at[par])

    def issue(block, par):
        base = block * tm

        def body(g, carry):
            for u in range(DMA_UNROLL):
                r = g * DMA_UNROLL + u
                row_copy(s0_ref[base + r], par, 0, r).start(priority=0)
                row_copy(s1_ref[base + r], par, 1, r).start(priority=1)
            return carry

        lax.fori_loop(0, tm // DMA_UNROLL, body, 0)

    def drain(par):
        def body(g, carry):
            for u in range(DMA_UNROLL):
                row_copy(0, par, 0, 0).wait()
                row_copy(0, par, 1, 0).wait()
            return carry

        lax.fori_loop(0, tm // DMA_UNROLL, body, 0)

    @pl.when(i == 0)
    def _():
        issue(0, 0)

    for par in range(2):
        @pl.when(i % 2 == par)
        def _(par=par):
            @pl.when(i + 1 < n)
            def _():
                issue(i + 1, 1 - par)

            drain(par)
            gates = jnp.transpose(gate_ref[...])
            g0 = gates[:, 0:1]
            g1 = gates[:, 1:2]
            for c in range(nc):
                y0 = bufs[par][0][pl.ds(c, tm, stride=nc), :]
                y1 = bufs[par][1][pl.ds(c, tm, stride=nc), :]
                cols = slice(c * LANES, (c + 1) * LANES)
                o_ref[:, cols] = x_ref[:, cols] + (g0 * y0 + g1 * y1)


def _combine(slot0, slot1, x3, gates, ys2d):
    t, d = x3.shape
    tm = COMBINE_BLOCK
    nc = d // LANES
    grid_spec = pltpu.PrefetchScalarGridSpec(
        num_scalar_prefetch=2,
        grid=(t // tm,),
        in_specs=[
            pl.BlockSpec((tm, d), lambda i, s0, s1: (i, 0)),
            pl.BlockSpec((SUBLANES, tm), lambda i, s0, s1: (0, i)),
            pl.BlockSpec(memory_space=pl.ANY),
        ],
        out_specs=pl.BlockSpec((tm, d), lambda i, s0, s1: (i, 0)),
        scratch_shapes=[
            pltpu.VMEM((tm * nc, LANES), F32),
            pltpu.VMEM((tm * nc, LANES), F32),
            pltpu.VMEM((tm * nc, LANES), F32),
            pltpu.VMEM((tm * nc, LANES), F32),
            pltpu.SemaphoreType.DMA((2,)),
        ],
    )
    return pl.pallas_call(
        functools.partial(_combine_kernel, tm=tm, d=d),
        grid_spec=grid_spec,
        out_shape=jax.ShapeDtypeStruct((t, d), F32),
        compiler_params=_params("arbitrary"),
        name="moe_combine",
    )(slot0, slot1, x3, gates, ys2d)


def _routing_tables(route, counts, n_experts, max_tiles):
    tile = GROUP_TILE
    padded = ((counts + tile - 1) // tile) * tile
    ends = jnp.cumsum(padded)
    offs = ends - padded
    slots = []
    for k in range(TOP_K):
        expert, rank = route[k], route[TOP_K + k]
        base = jnp.zeros_like(expert)
        for e in range(n_experts):
            base = jnp.where(expert == e, offs[e], base)
        slots.append(base + rank)
    tile_ends = ends // tile
    n_tiles = tile_ends[-1]
    j = jnp.arange(max_tiles, dtype=jnp.int32)
    tile_expert = jnp.sum((jnp.minimum(j, n_tiles - 1)[:, None] >= tile_ends[None, :]).astype(jnp.int32), axis=-1)
    row_end = jnp.sum(jnp.where(tile_expert[:, None] == jnp.arange(n_experts, dtype=jnp.int32)[None, :],
                                (offs + counts)[None, :], 0), axis=-1)
    tile_valid = jnp.where(j < n_tiles, jnp.clip(row_end - j * tile, 0, tile), 0)
    pad_start = jnp.concatenate([offs + counts, ends[-1:]]).astype(jnp.int32)
    pad_len = jnp.concatenate([padded - counts, max_tiles - n_tiles.reshape(1)]).astype(jnp.int32)
    return (slots[0], slots[1], pad_start, pad_len,
            tile_expert.astype(jnp.int32), n_tiles.reshape(1).astype(jnp.int32), tile_valid.astype(jnp.int32))


def _lambda_init(layer_idx_1based):
    return 0.8 - 0.6 * math.exp(-0.3 * (layer_idx_1based - 1))


def kernel(x, ln_mix, ln_ffn, conv_w_in, conv_w, conv_w_out, ln_kv, w_kv, k_norm, attn_w_q, q_norm, lam_params,
           sub_norm, attn_w_o, ffn_w_gu, ffn_w_down, router_w, moe_w_gu, moe_w_down):
    batch, seq, d = x.shape
    t = batch * seq
    n_experts = router_w.shape[-1]
    assert ln_mix.shape[0] == 2 and conv_w_in.shape[0] == 1 and attn_w_q.shape[0] == 1
    assert seq % ROW_TILE == 0 and seq % ATTN_TILE == 0 and d % LANES == 0
    nc = d // LANES
    x2d = x.reshape(t, d)

    x1 = _mixer_a(x2d, ln_mix[0:1], conv_w_in[0], conv_w[0], conv_w_out[0], seq)
    x2 = _ffn_dense(x1, ln_ffn[0:1], ffn_w_gu[0], ffn_w_down[0])

    n_k = d
    w_kt = w_kv[:, :n_k].T
    w_v = w_kv[:, n_k:]
    q, kt, v = _qkv(x2, ln_mix[1:2], ln_kv.reshape(1, d), attn_w_q[0], w_kt, w_v,
                    k_norm.reshape(HEAD_DIM, 1))

    score_bound = (math.sqrt(HEAD_DIM) * jnp.max(jnp.abs(q_norm[0])) * jnp.max(jnp.abs(k_norm))).reshape(1)
    o, moe_gu_bf, moe_down_bf = _attention(score_bound, q, kt, v, jnp.tile(q_norm[0:1], (1, 2)), lam_params[0],
                                           sub_norm[0:1], moe_w_gu[0], moe_w_down[0], batch, seq, _lambda_init(2))

    assert n_experts <= SUBLANES
    rw_t = jnp.pad(router_w[0].T, ((0, SUBLANES - n_experts), (0, 0)))
    x3, h8, gates, route, counts = _oproj_router(x2, o, attn_w_o[0], ln_ffn[1:2], rw_t, n_experts)
    max_tiles = (TOP_K * t) // GROUP_TILE + n_experts
    n_slots = max_tiles * GROUP_TILE
    slot0, slot1, pad_start, pad_len, tile_expert, n_tiles, tile_valid = _routing_tables(
        route, counts[:n_experts, 0], n_experts, max_tiles)
    hs = _dispatch(slot0, slot1, pad_start, pad_len, h8, nc, n_slots, n_experts)
    ys = _moe_group(tile_expert, n_tiles, tile_valid, hs, moe_gu_bf, moe_down_bf, max_tiles)
    out = _combine(slot0, slot1, x3, gates, ys)
    return out.reshape(batch, seq, d)
```

```python
import functools
import math

import jax
import jax.numpy as jnp
from jax import lax
from jax.experimental import pallas as pl
from jax.experimental.pallas import tpu as pltpu

F32 = jnp.float32
BF16 = jnp.bfloat16

EPS = 1e-6
HEAD_DIM = 64
V_DIM = 2 * HEAD_DIM
CONV_WIDTH = 3
TOP_K = 2

LANES = 128
SUBLANES = 8
VMEM_LIMIT = 56 * 1024 * 1024

ROW_TILE = 512
PROJ_TILE = 1024
ATTN_TILE = 256
ATTN_HEADS_PER_STEP = 2
GROUP_TILE = 512
MOE_F_CHUNK = 512
MOE_VMEM_LIMIT = 62 * 1024 * 1024
DISPATCH_BLOCK = 2048
COMBINE_BLOCK = 256
DMA_UNROLL = 16

LOG2E = 1.4426950408889634
SAFE_SHIFT = 40.0


def _rms_scale(x):
    return x * lax.rsqrt(jnp.mean(x * x, axis=-1, keepdims=True) + EPS)


def _dot(a, b):
    return jnp.dot(a, b, preferred_element_type=F32)


def _resident(shape):
    return pl.BlockSpec(shape, lambda *_: (0,) * len(shape), pipeline_mode=pl.Buffered(1))


def _params(*sem):
    return pltpu.CompilerParams(dimension_semantics=sem, vmem_limit_bytes=VMEM_LIMIT)


def _mixer_a_kernel(x_ref, g_ref, win_ref, cw_ref, wout_ref, o_ref, ubuf, *, tm, d, tiles_per_seq):
    i = pl.program_id(0)
    x = x_ref[...]
    h = (_rms_scale(x) * g_ref[...]).astype(BF16)
    c = _dot(h, win_ref[:, d:2 * d].astype(BF16))
    v = _dot(h, win_ref[:, 2 * d:].astype(BF16))
    u = c * v

    @pl.when(i % tiles_per_seq == 0)
    def _():
        ubuf[0:SUBLANES, :] = jnp.zeros((SUBLANES, d), F32)

    ubuf[SUBLANES:tm + SUBLANES, :] = u
    u1 = ubuf[SUBLANES - 1:tm + SUBLANES - 1, :]
    u2 = ubuf[SUBLANES - 2:tm + SUBLANES - 2, :]
    cw = cw_ref[...]
    z = u2 * cw[0:1] + u1 * cw[1:2] + u * cw[2:3]
    b = _dot(h, win_ref[:, 0:d].astype(BF16))
    y = (b * z).astype(BF16)
    o_ref[...] = x + _dot(y, wout_ref[...].astype(BF16))
    ubuf[0:SUBLANES, :] = ubuf[tm:tm + SUBLANES, :]


def _mixer_a(x2d, g, w_in, conv_w, w_out, seq):
    t, d = x2d.shape
    tm = ROW_TILE
    kern = functools.partial(_mixer_a_kernel, tm=tm, d=d, tiles_per_seq=seq // tm)
    return pl.pallas_call(
        kern,
        grid=(t // tm,),
        in_specs=[
            pl.BlockSpec((tm, d), lambda i: (i, 0)),
            pl.BlockSpec((1, d), lambda i: (0, 0)),
            _resident((d, 3 * d)),
            pl.BlockSpec((CONV_WIDTH, d), lambda i: (0, 0)),
            _resident((d, d)),
        ],
        out_specs=pl.BlockSpec((tm, d), lambda i: (i, 0)),
        out_shape=jax.ShapeDtypeStruct((t, d), F32),
        scratch_shapes=[pltpu.VMEM((tm + SUBLANES, d), F32)],
        compiler_params=_params("arbitrary"),
        name="mixer_a",
    )(x2d, g, w_in, conv_w, w_out)


def _ffn_chunks(f):
    step = 768
    return [(lo, min(lo + step, f)) for lo in range(0, f, step)]


def _ffn_kernel(x_ref, g_ref, wgu_ref, wd_ref, o_ref, *, f):
    x = x_ref[...]
    h = (_rms_scale(x) * g_ref[...]).astype(BF16)
    acc = x
    for lo, hi in _ffn_chunks(f):
        gate = _dot(h, wgu_ref[:, lo:hi].astype(BF16))
        up = _dot(h, wgu_ref[:, f + lo:f + hi].astype(BF16))
        a = (gate * jax.nn.sigmoid(gate) * up).astype(BF16)
        acc = acc + _dot(a, wd_ref[lo:hi, :].astype(BF16))
    o_ref[...] = acc


def _ffn_dense(x2d, g, w_gu, w_down):
    t, d = x2d.shape
    f = w_down.shape[0]
    tm = ROW_TILE
    return pl.pallas_call(
        functools.partial(_ffn_kernel, f=f),
        grid=(t // tm,),
        in_specs=[
            pl.BlockSpec((tm, d), lambda i: (i, 0)),
            pl.BlockSpec((1, d), lambda i: (0, 0)),
            _resident((d, 2 * f)),
            _resident((f, d)),
        ],
        out_specs=pl.BlockSpec((tm, d), lambda i: (i, 0)),
        out_shape=jax.ShapeDtypeStruct((t, d), F32),
        compiler_params=_params("parallel"),
        name="ffn_dense",
    )(x2d, g, w_gu, w_down)


def _qkv_kernel(x_ref, gq_ref, gkv_ref, wq_ref, wkt_ref, wv_ref, kn_ref, q_ref, kt_ref, v_ref, *, tm, d):
    y = _rms_scale(x_ref[...])
    hq = (y * gq_ref[...]).astype(BF16)
    hkv = (y * gkv_ref[...]).astype(BF16)
    q_ref[...] = _dot(hq, wq_ref[...].astype(BF16)).astype(BF16)
    v_ref[...] = _dot(hkv, wv_ref[...].astype(BF16)).astype(BF16)
    kt = lax.dot_general(wkt_ref[...].astype(BF16), hkv, (((1,), (1,)), ((), ())), preferred_element_type=F32)
    k3 = kt.reshape(d // HEAD_DIM, HEAD_DIM, tm)
    k3 = k3 * lax.rsqrt(jnp.mean(k3 * k3, axis=1, keepdims=True) + EPS) * kn_ref[...][None]
    kt_ref[...] = k3.reshape(d, tm).astype(BF16)


def _qkv(x2d, g_q, g_kv, w_q, w_kt, w_v, k_norm_col):
    t, d = x2d.shape
    tm = PROJ_TILE
    const = lambda i: (0, 0)
    return pl.pallas_call(
        functools.partial(_qkv_kernel, tm=tm, d=d),
        grid=(t // tm,),
        in_specs=[
            pl.BlockSpec((tm, d), lambda i: (i, 0)),
            pl.BlockSpec((1, d), const),
            pl.BlockSpec((1, d), const),
            _resident((d, d)),
            _resident((d, d)),
            _resident((d, d)),
            pl.BlockSpec((HEAD_DIM, 1), const),
        ],
        out_specs=[
            pl.BlockSpec((tm, d), lambda i: (i, 0)),
            pl.BlockSpec((d, tm), lambda i: (0, i)),
            pl.BlockSpec((tm, d), lambda i: (i, 0)),
        ],
        out_shape=[
            jax.ShapeDtypeStruct((t, d), BF16),
            jax.ShapeDtypeStruct((d, t), BF16),
            jax.ShapeDtypeStruct((t, d), BF16),
        ],
        compiler_params=_params("parallel"),
        name="qkv_proj",
    )(x2d, g_q, g_kv, w_q, w_kt, w_v, k_norm_col)


def _group_mean_sq(x, ones_bd, group):
    return _dot((x * x).astype(BF16), ones_bd) * (1.0 / group)


def _attn_kernel(bound_ref, q_ref, kt_ref, v_ref, qn_ref, lam_ref, sn_ref, wgu_ref, wdn_ref,
                 o_ref, wgu_bf_ref, wdn_bf_ref, v1_ref, qn_s, *, seq, tq, heads, lam_init):
    wgu_bf_ref[...] = wgu_ref[...].astype(BF16)
    wdn_bf_ref[...] = wdn_ref[...].astype(BF16)
    lp = lam_ref[...]
    lam = (jnp.exp(jnp.sum(lp[0:1] * lp[1:2], axis=-1, keepdims=True))
           - jnp.exp(jnp.sum(lp[2:3] * lp[3:4], axis=-1, keepdims=True)) + lam_init)
    row = lax.broadcasted_iota(jnp.int32, (tq, tq), 0)
    col = lax.broadcasted_iota(jnp.int32, (tq, tq), 1)
    causal = col <= row
    gi = lax.broadcasted_iota(jnp.int32, (V_DIM, V_DIM), 0) // HEAD_DIM
    gj = lax.broadcasted_iota(jnp.int32, (V_DIM, V_DIM), 1) // HEAD_DIM
    ones_bd = jnp.where(gi == gj, 1.0, 0.0).astype(BF16)
    qgain = qn_ref[...] * (HEAD_DIM ** -0.5 * LOG2E)
    maps = [slice(c * HEAD_DIM, (c + 1) * HEAD_DIM) for c in range(2)]
    n_q = seq // tq

    v1_ref[:, V_DIM:] = jnp.ones((seq, V_DIM), BF16)
    for hh in range(heads):
        hcols = slice(hh * V_DIM, (hh + 1) * V_DIM)
        hrow = hh * V_DIM
        v1_ref[:, 0:V_DIM] = v_ref[:, hcols]
        q = q_ref[:, hcols].astype(F32)
        ms = _group_mean_sq(q, ones_bd, HEAD_DIM)
        qn_s[...] = (q * lax.rsqrt(ms + EPS) * qgain).astype(BF16)

        def finish(q0, r1, r2, hcols=hcols):
            o = r1[:, 0:V_DIM] / r1[:, V_DIM:] - lam * (r2[:, 0:V_DIM] / r2[:, V_DIM:])
            o = _rms_scale(o) * sn_ref[...] * (1.0 - lam_init)
            o_ref[pl.ds(q0, tq), hcols] = o.astype(BF16)

        def kt_rows(sl, hrow=hrow):
            return slice(hrow + sl.start, hrow + sl.stop)

        @pl.when(bound_ref[0] <= SAFE_SHIFT)
        def _fixed_shift(finish=finish, kt_rows=kt_rows):
            shift = bound_ref[0] * LOG2E

            def scores(qi):
                q0 = qi * tq
                out = []
                for sl in maps:
                    qc = qn_s[q0:q0 + tq, sl]
                    s_diag = _dot(qc, kt_ref[kt_rows(sl), q0:q0 + tq])
                    s_low = _dot(qc, kt_ref[kt_rows(sl), 0:q0]) if q0 > 0 else None
                    out.append((s_diag, s_low))
                return out

            order = list(reversed(range(n_q)))
            pending = scores(order[0])
            for idx, qi in enumerate(order):
                q0 = qi * tq
                cur = pending
                if idx + 1 < n_q:
                    pending = scores(order[idx + 1])
                res = []
                for s_diag, s_low in cur:
                    p = jnp.where(causal, jnp.exp2(s_diag - shift), 0.0).astype(BF16)
                    r = _dot(p, v1_ref[q0:q0 + tq, :])
                    if s_low is not None:
                        r = r + _dot(jnp.exp2(s_low - shift).astype(BF16), v1_ref[0:q0, :])
                    res.append(r)
                finish(q0, *res)

        @pl.when(jnp.logical_not(bound_ref[0] <= SAFE_SHIFT))
        def _running_max(finish=finish, kt_rows=kt_rows):
            def q_body(qi, carry):
                q0 = pl.multiple_of(qi * tq, tq)
                qs = [qn_s[pl.ds(q0, tq), sl] for sl in maps]

                def tile(j, state, masked):
                    k0 = pl.multiple_of(j * tq, tq)
                    out = []
                    for c, sl in enumerate(maps):
                        m, r = state[c]
                        s = _dot(qs[c], kt_ref[kt_rows(sl), pl.ds(k0, tq)])
                        if masked:
                            s = jnp.where(causal, s, -jnp.inf)
                        m_new = jnp.maximum(m, jnp.max(s, axis=-1, keepdims=True))
                        p = jnp.exp2(s - m_new).astype(BF16)
                        r = jnp.exp2(m - m_new) * r + _dot(p, v1_ref[pl.ds(k0, tq), :])
                        out.append((m_new, r))
                    return tuple(out)

                one = (jnp.full((tq, 1), -jnp.inf, F32), jnp.zeros((tq, 2 * V_DIM), F32))
                state = lax.fori_loop(0, qi, lambda j, st: tile(j, st, False), (one, one))
                (_, r1), (_, r2) = tile(qi, state, True)
                finish(q0, r1, r2)
                return carry

            lax.fori_loop(0, n_q, q_body, 0)


def _attention(score_bound, q, kt, v, q_norm2, lam_params, sub_norm, w_gu, w_down, batch, seq, lam_init):
    t, d = q.shape
    n_heads = d // V_DIM
    hps = ATTN_HEADS_PER_STEP
    assert n_heads % hps == 0
    hsteps = n_heads // hps
    steps = batch * hsteps
    wgu2 = w_gu.reshape(-1, w_gu.shape[-1])
    wdn2 = w_down.reshape(-1, w_down.shape[-1])
    gu_rows, dn_rows = wgu2.shape[0] // steps, wdn2.shape[0] // steps
    assert wgu2.shape[0] % steps == 0 and wdn2.shape[0] % steps == 0 and gu_rows % 16 == 0 and dn_rows % 16 == 0
    slab = lambda b, h: (b * hsteps + h, 0)
    hw = hps * V_DIM
    kern = functools.partial(_attn_kernel, seq=seq, tq=ATTN_TILE, heads=hps, lam_init=lam_init)
    o, wgu_bf, wdn_bf = pl.pallas_call(
        kern,
        grid=(batch, hsteps),
        in_specs=[
            pl.BlockSpec(memory_space=pltpu.SMEM),
            pl.BlockSpec((seq, hw), lambda b, h: (b, h)),
            pl.BlockSpec((hw, seq), lambda b, h: (h, b)),
            pl.BlockSpec((seq, hw), lambda b, h: (b, h)),
            pl.BlockSpec((1, V_DIM), lambda b, h: (0, 0)),
            pl.BlockSpec((4, HEAD_DIM), lambda b, h: (0, 0)),
            pl.BlockSpec((1, V_DIM), lambda b, h: (0, 0)),
            pl.BlockSpec((gu_rows, wgu2.shape[1]), slab),
            pl.BlockSpec((dn_rows, wdn2.shape[1]), slab),
        ],
        out_specs=[
            pl.BlockSpec((seq, hw), lambda b, h: (b, h)),
            pl.BlockSpec((gu_rows, wgu2.shape[1]), slab),
            pl.BlockSpec((dn_rows, wdn2.shape[1]), slab),
        ],
        out_shape=[
            jax.ShapeDtypeStruct((t, d), BF16),
            jax.ShapeDtypeStruct(wgu2.shape, BF16),
            jax.ShapeDtypeStruct(wdn2.shape, BF16),
        ],
        scratch_shapes=[pltpu.VMEM((seq, 2 * V_DIM), BF16), pltpu.VMEM((seq, V_DIM), BF16)],
        compiler_params=_params("parallel", "parallel"),
        name="diff_attn",
    )(score_bound, q, kt, v, q_norm2, lam_params, sub_norm, wgu2, wdn2)
    return o, wgu_bf.reshape(w_gu.shape), wdn_bf.reshape(w_down.shape)


def _nt_dot(a, b):
    return lax.dot_general(a, b, (((1,), (1,)), ((), ())), preferred_element_type=F32)


def _oproj_router_kernel(x_ref, o_ref, wo_ref, g_ref, rwt_ref,
                         x3_ref, h8_ref, gate_ref, route_ref, cnt_ref, carry,
                         *, tm, d, n_experts):
    i = pl.program_id(0)

    @pl.when(i == 0)
    def _():
        carry[...] = jnp.zeros_like(carry)

    x3 = x_ref[...] + _dot(o_ref[...], wo_ref[...].astype(BF16))
    x3_ref[...] = x3
    h = _rms_scale(x3) * g_ref[...]
    for c in range(d // LANES):
        h8_ref[pl.ds(c, tm, stride=d // LANES), :] = h[:, c * LANES:(c + 1) * LANES]

    rwt = rwt_ref[...]
    h_hi = h.astype(BF16)
    h_lo = (h - h_hi.astype(F32)).astype(BF16)
    w_hi = rwt.astype(BF16)
    w_lo = (rwt - w_hi.astype(F32)).astype(BF16)
    logits = _nt_dot(w_hi, h_hi) + (_nt_dot(w_lo, h_hi) + _nt_dot(w_hi, h_lo))

    rows = rwt.shape[0]
    sub = lax.broadcasted_iota(jnp.int32, (rows, tm), 0)
    lane = lax.broadcasted_iota(jnp.int32, (rows, tm), 1)
    neg = -jnp.inf
    l1 = jnp.where(sub < n_experts, logits, neg)
    m1 = jnp.max(l1, axis=0, keepdims=True)
    i1 = jnp.min(jnp.where(l1 == m1, sub, rows), axis=0, keepdims=True)
    l2 = jnp.where(sub == i1, neg, l1)
    m2 = jnp.max(l2, axis=0, keepdims=True)
    i2 = jnp.min(jnp.where(l2 == m2, sub, rows), axis=0, keepdims=True)
    e2 = jnp.exp(m2 - m1)
    den = 1.0 + e2
    gate_ref[...] = jnp.where(sub == 0, 1.0 / den, jnp.where(sub == 1, e2 / den, 0.0))

    oh1 = sub == i1
    oh2 = sub == i2
    cnt = jnp.where(oh1, 1.0, 0.0) + jnp.where(oh2, 1.0, 0.0)
    incl = cnt
    shift = 1
    while shift < tm:
        incl = incl + jnp.where(lane >= shift, pltpu.roll(incl, shift, 1), 0.0)
        shift *= 2
    before = incl - cnt + carry[:, 0:1]
    r1 = jnp.sum(jnp.where(oh1, before, 0.0), axis=0, keepdims=True)
    r2 = jnp.sum(jnp.where(oh2, before, 0.0), axis=0, keepdims=True)
    route = jnp.where(sub == 0, i1.astype(F32), jnp.where(sub == 1, i2.astype(F32),
                      jnp.where(sub == 2, r1, jnp.where(sub == 3, r2, 0.0))))
    route_ref[...] = route.astype(jnp.int32)
    total = carry[...] + jnp.sum(cnt, axis=1, keepdims=True)
    carry[...] = total
    cnt_ref[...] = total.astype(jnp.int32)


def _oproj_router(x2d, o, w_o, g, rw_t, n_experts):
    t, d = x2d.shape
    tm = PROJ_TILE
    const = lambda i: (0, 0)
    kern = functools.partial(_oproj_router_kernel, tm=tm, d=d, n_experts=n_experts)
    return pl.pallas_call(
        kern,
        grid=(t // tm,),
        in_specs=[
            pl.BlockSpec((tm, d), lambda i: (i, 0)),
            pl.BlockSpec((tm, d), lambda i: (i, 0)),
            _resident((d, d)),
            pl.BlockSpec((1, d), const),
            pl.BlockSpec((SUBLANES, d), const),
        ],
        out_specs=[
            pl.BlockSpec((tm, d), lambda i: (i, 0)),
            pl.BlockSpec((tm * (d // LANES), LANES), lambda i: (i, 0)),
            pl.BlockSpec((SUBLANES, tm), lambda i: (0, i)),
            pl.BlockSpec((SUBLANES, tm), lambda i: (0, i)),
            pl.BlockSpec((SUBLANES, LANES), const),
        ],
        out_shape=[
            jax.ShapeDtypeStruct((t, d), F32),
            jax.ShapeDtypeStruct((t * (d // LANES), LANES), F32),
            jax.ShapeDtypeStruct((SUBLANES, t), F32),
            jax.ShapeDtypeStruct((SUBLANES, t), jnp.int32),
            jax.ShapeDtypeStruct((SUBLANES, LANES), jnp.int32),
        ],
        scratch_shapes=[pltpu.VMEM((SUBLANES, LANES), F32)],
        compiler_params=_params("arbitrary"),
        name="oproj_router",
    )(x2d, o, w_o, g, rw_t)


def _pad_bits(tile):
    return [1 << b for b in reversed(range(int(math.log2(tile))))]


def _dispatch_kernel(s0_ref, s1_ref, pad_start_ref, pad_len_ref, h_ref, hs_hbm, zeros, sem, zsem,
                     *, tb, nc, n_experts, tile):
    i = pl.program_id(0)
    base = i * tb

    def rows(ref, first, count):
        start = first * nc
        if not isinstance(start, int):
            start = pl.multiple_of(start, nc)
        return ref.at[pl.ds(start, count * nc), :]

    def row_copy(r, slot):
        return pltpu.make_async_copy(rows(h_ref, r, 1), rows(hs_hbm, slot, 1), sem)

    def issue(g, carry):
        for u in range(DMA_UNROLL):
            r = g * DMA_UNROLL + u
            row_copy(r, s0_ref[base + r]).start(priority=0)
            row_copy(r, s1_ref[base + r]).start(priority=1)
        return carry

    lax.fori_loop(0, tb // DMA_UNROLL, issue, 0)

    def pad_copies(do):
        for e in range(n_experts):
            n = pad_len_ref[e]
            pos = pad_start_ref[e]
            for bit in _pad_bits(tile):
                @pl.when((n & bit) != 0)
                def _(pos=pos, bit=bit):
                    do(pltpu.make_async_copy(rows(zeros, 0, bit), rows(hs_hbm, pos, bit), zsem))
                pos = pos + (n & bit)
        for k in range(n_experts):
            @pl.when(k < pad_len_ref[n_experts])
            def _(k=k):
                pos = pad_start_ref[n_experts] + k * tile
                do(pltpu.make_async_copy(zeros, rows(hs_hbm, pos, tile), zsem))

    @pl.when(i == 0)
    def _():
        zeros[...] = jnp.zeros_like(zeros)
        pad_copies(lambda cp: cp.start())
        pad_copies(lambda cp: cp.wait())

    def drain(g, carry):
        for u in range(2 * DMA_UNROLL):
            row_copy(0, 0).wait()
        return carry

    lax.fori_loop(0, tb // DMA_UNROLL, drain, 0)


def _dispatch(slot0, slot1, pad_start, pad_len, h8, nc, n_slots, n_experts):
    t = h8.shape[0] // nc
    tb = DISPATCH_BLOCK
    kern = functools.partial(_dispatch_kernel, tb=tb, nc=nc, n_experts=n_experts, tile=GROUP_TILE)
    grid_spec = pltpu.PrefetchScalarGridSpec(
        num_scalar_prefetch=4,
        grid=(t // tb,),
        in_specs=[pl.BlockSpec((tb * nc, LANES), lambda i, *_: (i, 0))],
        out_specs=pl.BlockSpec(memory_space=pl.ANY),
        scratch_shapes=[
            pltpu.VMEM((GROUP_TILE * nc, LANES), F32),
            pltpu.SemaphoreType.DMA,
            pltpu.SemaphoreType.DMA,
        ],
    )
    return pl.pallas_call(
        kern,
        grid_spec=grid_spec,
        out_shape=jax.ShapeDtypeStruct((n_slots * nc, LANES), F32),
        compiler_params=_params("arbitrary"),
        name="moe_dispatch",
    )(slot0, slot1, pad_start, pad_len, h8)


def _moe_group_kernel(te_ref, nt_ref, valid_ref, hs_ref, wg_ref, wu_ref, wd_ref, ys_ref, *, tm, d, f_exp):
    j = pl.program_id(0)
    nc = d // LANES
    valid = valid_ref[j]

    def expert_ffn(rows):
        h = jnp.concatenate([hs_ref[pl.ds(c, rows, stride=nc), :].astype(BF16) for c in range(nc)], axis=1)
        y = None
        for lo in range(0, f_exp, MOE_F_CHUNK):
            hi = min(lo + MOE_F_CHUNK, f_exp)
            gate = _dot(h, wg_ref[:, lo:hi])
            up = _dot(h, wu_ref[:, lo:hi])
            a = (gate * jax.nn.sigmoid(gate) * up).astype(BF16)
            p = _dot(a, wd_ref[lo:hi, :])
            y = p if y is None else y + p
        for c in range(nc):
            ys_ref[pl.ds(c, rows, stride=nc), :] = y[:, c * LANES:(c + 1) * LANES]

    half = tm // 2

    @pl.when(valid > half)
    def _():
        expert_ffn(tm)

    @pl.when(jnp.logical_and(valid > 0, valid <= half))
    def _():
        expert_ffn(half)
        ys_ref[half * nc:, :] = jnp.zeros(((tm - half) * nc, LANES), F32)

    @pl.when(valid == 0)
    def _():
        ys_ref[...] = jnp.zeros_like(ys_ref)


def _moe_group(tile_expert, n_tiles, tile_valid, hs2d, w_gu, w_down, max_tiles):
    n_e, d, f2 = w_gu.shape
    f_exp = f2 // 2
    tm = GROUP_TILE
    nc = d // LANES

    def row_map(j, te, nt, valid):
        return (jnp.minimum(j, nt[0] - 1), 0)

    kern = functools.partial(_moe_group_kernel, tm=tm, d=d, f_exp=f_exp)
    grid_spec = pltpu.PrefetchScalarGridSpec(
        num_scalar_prefetch=3,
        grid=(max_tiles,),
        in_specs=[
            pl.BlockSpec((tm * nc, LANES), row_map),
            pl.BlockSpec((None, d, f_exp), lambda j, te, nt, valid: (te[j], 0, 0)),
            pl.BlockSpec((None, d, f_exp), lambda j, te, nt, valid: (te[j], 0, 1)),
            pl.BlockSpec((None, f_exp, d), lambda j, te, nt, valid: (te[j], 0, 0)),
        ],
        out_specs=pl.BlockSpec((tm * nc, LANES), lambda j, te, nt, valid: (j, 0)),
    )
    return pl.pallas_call(
        kern,
        grid_spec=grid_spec,
        out_shape=jax.ShapeDtypeStruct(hs2d.shape, F32),
        compiler_params=pltpu.CompilerParams(dimension_semantics=("arbitrary",),
                                             vmem_limit_bytes=MOE_VMEM_LIMIT),
        name="moe_group",
    )(tile_expert, n_tiles, tile_valid, hs2d, w_gu, w_gu, w_down)


def _combine_kernel(s0_ref, s1_ref, x_ref, gate_ref, ys_hbm, o_ref, b00, b01, b10, b11, sems, *, tm, d):
    i = pl.program_id(0)
    n = pl.num_programs(0)
    nc = d // LANES
    bufs = ((b00, b01), (b10, b11))

    def row_copy(slot, par, k, r):
        start = r * nc
        if not isinstance(start, int):
            start = pl.multiple_of(start, nc)
        src = ys_hbm.at[pl.ds(pl.multiple_of(slot * nc, nc), nc), :]
        return pltpu.make_async_copy(src, bufs[par][k].at[pl.ds(start, nc), :], sems.at[par])

    def issue(block, par):
        base = block * tm

        def body(g, carry):
            for u in range(DMA_UNROLL):
                r = g * DMA_UNROLL + u
                row_copy(s0_ref[base + r], par, 0, r).start(priority=0)
                row_copy(s1_ref[base + r], par, 1, r).start(priority=1)
            return carry

        lax.fori_loop(0, tm // DMA_UNROLL, body, 0)

    def drain(par):
        def body(g, carry):
            for u in range(DMA_UNROLL):
                row_copy(0, par, 0, 0).wait()
                row_copy(0, par, 1, 0).wait()
            return carry

        lax.fori_loop(0, tm // DMA_UNROLL, body, 0)

    @pl.when(i == 0)
    def _():
        issue(0, 0)

    for par in range(2):
        @pl.when(i % 2 == par)
        def _(par=par):
            @pl.when(i + 1 < n)
            def _():
                issue(i + 1, 1 - par)

            drain(par)
            gates = jnp.transpose(gate_ref[...])
            g0 = gates[:, 0:1]
            g1 = gates[:, 1:2]
            for c in range(nc):
                y0 = bufs[par][0][pl.ds(c, tm, stride=nc), :]
                y1 = bufs[par][1][pl.ds(c, tm, stride=nc), :]
                cols = slice(c * LANES, (c + 1) * LANES)
                o_ref[:, cols] = x_ref[:, cols] + (g0 * y0 + g1 * y1)


def _combine(slot0, slot1, x3, gates, ys2d):
    t, d = x3.shape
    tm = COMBINE_BLOCK
    nc = d // LANES
    grid_spec = pltpu.PrefetchScalarGridSpec(
        num_scalar_prefetch=2,
        grid=(t // tm,),
        in_specs=[
            pl.BlockSpec((tm, d), lambda i, s0, s1: (i, 0)),
            pl.BlockSpec((SUBLANES, tm), lambda i, s0, s1: (0, i)),
            pl.BlockSpec(memory_space=pl.ANY),
        ],
        out_specs=pl.BlockSpec((tm, d), lambda i, s0, s1: (i, 0)),
        scratch_shapes=[
            pltpu.VMEM((tm * nc, LANES), F32),
            pltpu.VMEM((tm * nc, LANES), F32),
            pltpu.VMEM((tm * nc, LANES), F32),
            pltpu.VMEM((tm * nc, LANES), F32),
            pltpu.SemaphoreType.DMA((2,)),
        ],
    )
    return pl.pallas_call(
        functools.partial(_combine_kernel, tm=tm, d=d),
        grid_spec=grid_spec,
        out_shape=jax.ShapeDtypeStruct((t, d), F32),
        compiler_params=_params("arbitrary"),
        name="moe_combine",
    )(slot0, slot1, x3, gates, ys2d)


def _routing_tables(route, counts, n_experts, max_tiles):
    tile = GROUP_TILE
    padded = ((counts + tile - 1) // tile) * tile
    ends = jnp.cumsum(padded)
    offs = ends - padded
    slots = []
    for k in range(TOP_K):
        expert, rank = route[k], route[TOP_K + k]
        base = jnp.zeros_like(expert)
        for e in range(n_experts):
            base = jnp.where(expert == e, offs[e], base)
        slots.append(base + rank)
    tile_ends = ends // tile
    n_tiles = tile_ends[-1]
    j = jnp.arange(max_tiles, dtype=jnp.int32)
    tile_expert = jnp.sum((jnp.minimum(j, n_tiles - 1)[:, None] >= tile_ends[None, :]).astype(jnp.int32), axis=-1)
    row_end = jnp.sum(jnp.where(tile_expert[:, None] == jnp.arange(n_experts, dtype=jnp.int32)[None, :],
                                (offs + counts)[None, :], 0), axis=-1)
    tile_valid = jnp.where(j < n_tiles, jnp.clip(row_end - j * tile, 0, tile), 0)
    pad_start = jnp.concatenate([offs + counts, ends[-1:]]).astype(jnp.int32)
    pad_len = jnp.concatenate([padded - counts, max_tiles - n_tiles.reshape(1)]).astype(jnp.int32)
    return (slots[0], slots[1], pad_start, pad_len,
            tile_expert.astype(jnp.int32), n_tiles.reshape(1).astype(jnp.int32), tile_valid.astype(jnp.int32))


def _lambda_init(layer_idx_1based):
    return 0.8 - 0.6 * math.exp(-0.3 * (layer_idx_1based - 1))


def kernel(x, ln_mix, ln_ffn, conv_w_in, conv_w, conv_w_out, ln_kv, w_kv, k_norm, attn_w_q, q_norm, lam_params,
           sub_norm, attn_w_o, ffn_w_gu, ffn_w_down, router_w, moe_w_gu, moe_w_down):
    batch, seq, d = x.shape
    t = batch * seq
    n_experts = router_w.shape[-1]
    assert ln_mix.shape[0] == 2 and conv_w_in.shape[0] == 1 and attn_w_q.shape[0] == 1
    assert seq % ROW_TILE == 0 and seq % ATTN_TILE == 0 and d % LANES == 0
    nc = d // LANES
    x2d = x.reshape(t, d)

    x1 = _mixer_a(x2d, ln_mix[0:1], conv_w_in[0], conv_w[0], conv_w_out[0], seq)
    x2 = _ffn_dense(x1, ln_ffn[0:1], ffn_w_gu[0], ffn_w_down[0])

    n_k = d
    w_kt = w_kv[:, :n_k].T
    w_v = w_kv[:, n_k:]
    q, kt, v = _qkv(x2, ln_mix[1:2], ln_kv.reshape(1, d), attn_w_q[0], w_kt, w_v,
                    k_norm.reshape(HEAD_DIM, 1))

    score_bound = (math.sqrt(HEAD_DIM) * jnp.max(jnp.abs(q_norm[0])) * jnp.max(jnp.abs(k_norm))).reshape(1)
    o, moe_gu_bf, moe_down_bf = _attention(score_bound, q, kt, v, jnp.tile(q_norm[0:1], (1, 2)), lam_params[0],
                                           sub_norm[0:1], moe_w_gu[0], moe_w_down[0], batch, seq, _lambda_init(2))

    assert n_experts <= SUBLANES
    rw_t = jnp.pad(router_w[0].T, ((0, SUBLANES - n_experts), (0, 0)))
    x3, h8, gates, route, counts = _oproj_router(x2, o, attn_w_o[0], ln_ffn[1:2], rw_t, n_experts)
    max_tiles = (TOP_K * t) // GROUP_TILE + n_experts
    n_slots = max_tiles * GROUP_TILE
    slot0, slot1, pad_start, pad_len, tile_expert, n_tiles, tile_valid = _routing_tables(
        route, counts[:n_experts, 0], n_experts, max_tiles)
    hs = _dispatch(slot0, slot1, pad_start, pad_len, h8, nc, n_slots, n_experts)
    ys = _moe_group(tile_expert, n_tiles, tile_valid, hs, moe_gu_bf, moe_down_bf, max_tiles)
    out = _combine(slot0, slot1, x3, gates, ys)
    return out.reshape(batch, seq, d)
```

```python
import functools
import math

import jax
import jax.numpy as jnp
from jax import lax
from jax.experimental import pallas as pl
from jax.experimental.pallas import tpu as pltpu

F32 = jnp.float32
BF16 = jnp.bfloat16

EPS = 1e-6
HEAD_DIM = 64
V_DIM = 2 * HEAD_DIM
CONV_WIDTH = 3
TOP_K = 2

LANES = 128
SUBLANES = 8
VMEM_LIMIT = 56 * 1024 * 1024

ROW_TILE = 512
PROJ_TILE = 1024
ATTN_TILE = 256
ATTN_HEADS_PER_STEP = 2
GROUP_TILE = 512
MOE_F_CHUNK = 1792
MOE_VMEM_LIMIT = 62 * 1024 * 1024
DISPATCH_BLOCK = 2048
COMBINE_BLOCK = 256
DMA_UNROLL = 16

LOG2E = 1.4426950408889634
SAFE_SHIFT = 40.0


def _rms_scale(x):
    return x * lax.rsqrt(jnp.mean(x * x, axis=-1, keepdims=True) + EPS)


def _dot(a, b):
    return jnp.dot(a, b, preferred_element_type=F32)


def _resident(shape):
    return pl.BlockSpec(shape, lambda *_: (0,) * len(shape), pipeline_mode=pl.Buffered(1))


def _params(*sem):
    return pltpu.CompilerParams(dimension_semantics=sem, vmem_limit_bytes=VMEM_LIMIT)


def _mixer_a_kernel(x_ref, g_ref, win_ref, cw_ref, wout_ref, o_ref, ubuf, *, tm, d, tiles_per_seq):
    i = pl.program_id(0)
    x = x_ref[...]
    h = (_rms_scale(x) * g_ref[...]).astype(BF16)
    c = _dot(h, win_ref[:, d:2 * d].astype(BF16))
    v = _dot(h, win_ref[:, 2 * d:].astype(BF16))
    u = c * v

    @pl.when(i % tiles_per_seq == 0)
    def _():
        ubuf[0:SUBLANES, :] = jnp.zeros((SUBLANES, d), F32)

    ubuf[SUBLANES:tm + SUBLANES, :] = u
    u1 = ubuf[SUBLANES - 1:tm + SUBLANES - 1, :]
    u2 = ubuf[SUBLANES - 2:tm + SUBLANES - 2, :]
    cw = cw_ref[...]
    z = u2 * cw[0:1] + u1 * cw[1:2] + u * cw[2:3]
    b = _dot(h, win_ref[:, 0:d].astype(BF16))
    y = (b * z).astype(BF16)
    o_ref[...] = x + _dot(y, wout_ref[...].astype(BF16))
    ubuf[0:SUBLANES, :] = ubuf[tm:tm + SUBLANES, :]


def _mixer_a(x2d, g, w_in, conv_w, w_out, seq):
    t, d = x2d.shape
    tm = ROW_TILE
    kern = functools.partial(_mixer_a_kernel, tm=tm, d=d, tiles_per_seq=seq // tm)
    return pl.pallas_call(
        kern,
        grid=(t // tm,),
        in_specs=[
            pl.BlockSpec((tm, d), lambda i: (i, 0)),
            pl.BlockSpec((1, d), lambda i: (0, 0)),
            _resident((d, 3 * d)),
            pl.BlockSpec((CONV_WIDTH, d), lambda i: (0, 0)),
            _resident((d, d)),
        ],
        out_specs=pl.BlockSpec((tm, d), lambda i: (i, 0)),
        out_shape=jax.ShapeDtypeStruct((t, d), F32),
        scratch_shapes=[pltpu.VMEM((tm + SUBLANES, d), F32)],
        compiler_params=_params("arbitrary"),
        name="mixer_a",
    )(x2d, g, w_in, conv_w, w_out)


def _ffn_chunks(f):
    step = 768
    return [(lo, min(lo + step, f)) for lo in range(0, f, step)]


def _ffn_kernel(x_ref, g_ref, wgu_ref, wd_ref, o_ref, *, f):
    x = x_ref[...]
    h = (_rms_scale(x) * g_ref[...]).astype(BF16)
    acc = x
    for lo, hi in _ffn_chunks(f):
        gate = _dot(h, wgu_ref[:, lo:hi].astype(BF16))
        up = _dot(h, wgu_ref[:, f + lo:f + hi].astype(BF16))
        a = (gate * jax.nn.sigmoid(gate) * up).astype(BF16)
        acc = acc + _dot(a, wd_ref[lo:hi, :].astype(BF16))
    o_ref[...] = acc


def _ffn_dense(x2d, g, w_gu, w_down):
    t, d = x2d.shape
    f = w_down.shape[0]
    tm = ROW_TILE
    return pl.pallas_call(
        functools.partial(_ffn_kernel, f=f),
        grid=(t // tm,),
        in_specs=[
            pl.BlockSpec((tm, d), lambda i: (i, 0)),
            pl.BlockSpec((1, d), lambda i: (0, 0)),
            _resident((d, 2 * f)),
            _resident((f, d)),
        ],
        out_specs=pl.BlockSpec((tm, d), lambda i: (i, 0)),
        out_shape=jax.ShapeDtypeStruct((t, d), F32),
        compiler_params=_params("parallel"),
        name="ffn_dense",
    )(x2d, g, w_gu, w_down)


def _qkv_kernel(x_ref, gq_ref, gkv_ref, wq_ref, wkt_ref, wv_ref, kn_ref, q_ref, kt_ref, v_ref, *, tm, d):
    y = _rms_scale(x_ref[...])
    hq = (y * gq_ref[...]).astype(BF16)
    hkv = (y * gkv_ref[...]).astype(BF16)
    q_ref[...] = _dot(hq, wq_ref[...].astype(BF16)).astype(BF16)
    v_ref[...] = _dot(hkv, wv_ref[...].astype(BF16)).astype(BF16)
    kt = lax.dot_general(wkt_ref[...].astype(BF16), hkv, (((1,), (1,)), ((), ())), preferred_element_type=F32)
    k3 = kt.reshape(d // HEAD_DIM, HEAD_DIM, tm)
    k3 = k3 * lax.rsqrt(jnp.mean(k3 * k3, axis=1, keepdims=True) + EPS) * kn_ref[...][None]
    kt_ref[...] = k3.reshape(d, tm).astype(BF16)


def _qkv(x2d, g_q, g_kv, w_q, w_kt, w_v, k_norm_col):
    t, d = x2d.shape
    tm = PROJ_TILE
    const = lambda i: (0, 0)
    return pl.pallas_call(
        functools.partial(_qkv_kernel, tm=tm, d=d),
        grid=(t // tm,),
        in_specs=[
            pl.BlockSpec((tm, d), lambda i: (i, 0)),
            pl.BlockSpec((1, d), const),
            pl.BlockSpec((1, d), const),
            _resident((d, d)),
            _resident((d, d)),
            _resident((d, d)),
            pl.BlockSpec((HEAD_DIM, 1), const),
        ],
        out_specs=[
            pl.BlockSpec((tm, d), lambda i: (i, 0)),
            pl.BlockSpec((d, tm), lambda i: (0, i)),
            pl.BlockSpec((tm, d), lambda i: (i, 0)),
        ],
        out_shape=[
            jax.ShapeDtypeStruct((t, d), BF16),
            jax.ShapeDtypeStruct((d, t), BF16),
            jax.ShapeDtypeStruct((t, d), BF16),
        ],
        compiler_params=_params("parallel"),
        name="qkv_proj",
    )(x2d, g_q, g_kv, w_q, w_kt, w_v, k_norm_col)


def _group_mean_sq(x, ones_bd, group):
    return _dot((x * x).astype(BF16), ones_bd) * (1.0 / group)


def _attn_kernel(bound_ref, q_ref, kt_ref, v_ref, qn_ref, lam_ref, sn_ref, wgu_ref, wdn_ref,
                 o_ref, wgu_bf_ref, wdn_bf_ref, v1_ref, qn_s, *, seq, tq, heads, lam_init):
    wgu_bf_ref[...] = wgu_ref[...].astype(BF16)
    wdn_bf_ref[...] = wdn_ref[...].astype(BF16)
    lp = lam_ref[...]
    lam = (jnp.exp(jnp.sum(lp[0:1] * lp[1:2], axis=-1, keepdims=True))
           - jnp.exp(jnp.sum(lp[2:3] * lp[3:4], axis=-1, keepdims=True)) + lam_init)
    row = lax.broadcasted_iota(jnp.int32, (tq, tq), 0)
    col = lax.broadcasted_iota(jnp.int32, (tq, tq), 1)
    causal = col <= row
    gi = lax.broadcasted_iota(jnp.int32, (V_DIM, V_DIM), 0) // HEAD_DIM
    gj = lax.broadcasted_iota(jnp.int32, (V_DIM, V_DIM), 1) // HEAD_DIM
    ones_bd = jnp.where(gi == gj, 1.0, 0.0).astype(BF16)
    qgain = qn_ref[...] * (HEAD_DIM ** -0.5 * LOG2E)
    maps = [slice(c * HEAD_DIM, (c + 1) * HEAD_DIM) for c in range(2)]
    n_q = seq // tq

    v1_ref[:, V_DIM:] = jnp.ones((seq, V_DIM), BF16)
    for hh in range(heads):
        hcols = slice(hh * V_DIM, (hh + 1) * V_DIM)
        hrow = hh * V_DIM
        v1_ref[:, 0:V_DIM] = v_ref[:, hcols]
        q = q_ref[:, hcols].astype(F32)
        ms = _group_mean_sq(q, ones_bd, HEAD_DIM)
        qn_s[...] = (q * lax.rsqrt(ms + EPS) * qgain).astype(BF16)

        def finish(q0, r1, r2, hcols=hcols):
            o = r1[:, 0:V_DIM] / r1[:, V_DIM:] - lam * (r2[:, 0:V_DIM] / r2[:, V_DIM:])
            o = _rms_scale(o) * sn_ref[...] * (1.0 - lam_init)
            o_ref[pl.ds(q0, tq), hcols] = o.astype(BF16)

        def kt_rows(sl, hrow=hrow):
            return slice(hrow + sl.start, hrow + sl.stop)

        @pl.when(bound_ref[0] <= SAFE_SHIFT)
        def _fixed_shift(finish=finish, kt_rows=kt_rows):
            shift = bound_ref[0] * LOG2E

            def scores(qi):
                q0 = qi * tq
                out = []
                for sl in maps:
                    qc = qn_s[q0:q0 + tq, sl]
                    s_diag = _dot(qc, kt_ref[kt_rows(sl), q0:q0 + tq])
                    s_low = _dot(qc, kt_ref[kt_rows(sl), 0:q0]) if q0 > 0 else None
                    out.append((s_diag, s_low))
                return out

            order = list(reversed(range(n_q)))
            pending = scores(order[0])
            for idx, qi in enumerate(order):
                q0 = qi * tq
                cur = pending
                if idx + 1 < n_q:
                    pending = scores(order[idx + 1])
                res = []
                for s_diag, s_low in cur:
                    p = jnp.where(causal, jnp.exp2(s_diag - shift), 0.0).astype(BF16)
                    r = _dot(p, v1_ref[q0:q0 + tq, :])
                    if s_low is not None:
                        r = r + _dot(jnp.exp2(s_low - shift).astype(BF16), v1_ref[0:q0, :])
                    res.append(r)
                finish(q0, *res)

        @pl.when(jnp.logical_not(bound_ref[0] <= SAFE_SHIFT))
        def _running_max(finish=finish, kt_rows=kt_rows):
            def q_body(qi, carry):
                q0 = pl.multiple_of(qi * tq, tq)
                qs = [qn_s[pl.ds(q0, tq), sl] for sl in maps]

                def tile(j, state, masked):
                    k0 = pl.multiple_of(j * tq, tq)
                    out = []
                    for c, sl in enumerate(maps):
                        m, r = state[c]
                        s = _dot(qs[c], kt_ref[kt_rows(sl), pl.ds(k0, tq)])
                        if masked:
                            s = jnp.where(causal, s, -jnp.inf)
                        m_new = jnp.maximum(m, jnp.max(s, axis=-1, keepdims=True))
                        p = jnp.exp2(s - m_new).astype(BF16)
                        r = jnp.exp2(m - m_new) * r + _dot(p, v1_ref[pl.ds(k0, tq), :])
                        out.append((m_new, r))
                    return tuple(out)

                one = (jnp.full((tq, 1), -jnp.inf, F32), jnp.zeros((tq, 2 * V_DIM), F32))
                state = lax.fori_loop(0, qi, lambda j, st: tile(j, st, False), (one, one))
                (_, r1), (_, r2) = tile(qi, state, True)
                finish(q0, r1, r2)
                return carry

            lax.fori_loop(0, n_q, q_body, 0)


def _attention(score_bound, q, kt, v, q_norm2, lam_params, sub_norm, w_gu, w_down, batch, seq, lam_init):
    t, d = q.shape
    n_heads = d // V_DIM
    hps = ATTN_HEADS_PER_STEP
    assert n_heads % hps == 0
    hsteps = n_heads // hps
    steps = batch * hsteps
    wgu2 = w_gu.reshape(-1, w_gu.shape[-1])
    wdn2 = w_down.reshape(-1, w_down.shape[-1])
    gu_rows, dn_rows = wgu2.shape[0] // steps, wdn2.shape[0] // steps
    assert wgu2.shape[0] % steps == 0 and wdn2.shape[0] % steps == 0 and gu_rows % 16 == 0 and dn_rows % 16 == 0
    slab = lambda b, h: (b * hsteps + h, 0)
    hw = hps * V_DIM
    kern = functools.partial(_attn_kernel, seq=seq, tq=ATTN_TILE, heads=hps, lam_init=lam_init)
    o, wgu_bf, wdn_bf = pl.pallas_call(
        kern,
        grid=(batch, hsteps),
        in_specs=[
            pl.BlockSpec(memory_space=pltpu.SMEM),
            pl.BlockSpec((seq, hw), lambda b, h: (b, h)),
            pl.BlockSpec((hw, seq), lambda b, h: (h, b)),
            pl.BlockSpec((seq, hw), lambda b, h: (b, h)),
            pl.BlockSpec((1, V_DIM), lambda b, h: (0, 0)),
            pl.BlockSpec((4, HEAD_DIM), lambda b, h: (0, 0)),
            pl.BlockSpec((1, V_DIM), lambda b, h: (0, 0)),
            pl.BlockSpec((gu_rows, wgu2.shape[1]), slab),
            pl.BlockSpec((dn_rows, wdn2.shape[1]), slab),
        ],
        out_specs=[
            pl.BlockSpec((seq, hw), lambda b, h: (b, h)),
            pl.BlockSpec((gu_rows, wgu2.shape[1]), slab),
            pl.BlockSpec((dn_rows, wdn2.shape[1]), slab),
        ],
        out_shape=[
            jax.ShapeDtypeStruct((t, d), BF16),
            jax.ShapeDtypeStruct(wgu2.shape, BF16),
            jax.ShapeDtypeStruct(wdn2.shape, BF16),
        ],
        scratch_shapes=[pltpu.VMEM((seq, 2 * V_DIM), BF16), pltpu.VMEM((seq, V_DIM), BF16)],
        compiler_params=_params("parallel", "parallel"),
        name="diff_attn",
    )(score_bound, q, kt, v, q_norm2, lam_params, sub_norm, wgu2, wdn2)
    return o, wgu_bf.reshape(w_gu.shape), wdn_bf.reshape(w_down.shape)


def _nt_dot(a, b):
    return lax.dot_general(a, b, (((1,), (1,)), ((), ())), preferred_element_type=F32)


def _oproj_router_kernel(x_ref, o_ref, wo_ref, g_ref, rwt_ref,
                         x3_ref, h8_ref, gate_ref, route_ref, cnt_ref, carry,
                         *, tm, d, n_experts):
    i = pl.program_id(0)

    @pl.when(i == 0)
    def _():
        carry[...] = jnp.zeros_like(carry)

    x3 = x_ref[...] + _dot(o_ref[...], wo_ref[...].astype(BF16))
    x3_ref[...] = x3
    h = _rms_scale(x3) * g_ref[...]
    for c in range(d // LANES):
        h8_ref[pl.ds(c, tm, stride=d // LANES), :] = h[:, c * LANES:(c + 1) * LANES]

    rwt = rwt_ref[...]
    h_hi = h.astype(BF16)
    h_lo = (h - h_hi.astype(F32)).astype(BF16)
    w_hi = rwt.astype(BF16)
    w_lo = (rwt - w_hi.astype(F32)).astype(BF16)
    logits = _nt_dot(w_hi, h_hi) + (_nt_dot(w_lo, h_hi) + _nt_dot(w_hi, h_lo))

    rows = rwt.shape[0]
    sub = lax.broadcasted_iota(jnp.int32, (rows, tm), 0)
    lane = lax.broadcasted_iota(jnp.int32, (rows, tm), 1)
    neg = -jnp.inf
    l1 = jnp.where(sub < n_experts, logits, neg)
    m1 = jnp.max(l1, axis=0, keepdims=True)
    i1 = jnp.min(jnp.where(l1 == m1, sub, rows), axis=0, keepdims=True)
    l2 = jnp.where(sub == i1, neg, l1)
    m2 = jnp.max(l2, axis=0, keepdims=True)
    i2 = jnp.min(jnp.where(l2 == m2, sub, rows), axis=0, keepdims=True)
    e2 = jnp.exp(m2 - m1)
    den = 1.0 + e2
    gate_ref[...] = jnp.where(sub == 0, 1.0 / den, jnp.where(sub == 1, e2 / den, 0.0))

    oh1 = sub == i1
    oh2 = sub == i2
    cnt = jnp.where(oh1, 1.0, 0.0) + jnp.where(oh2, 1.0, 0.0)
    incl = cnt
    shift = 1
    while shift < tm:
        incl = incl + jnp.where(lane >= shift, pltpu.roll(incl, shift, 1), 0.0)
        shift *= 2
    before = incl - cnt + carry[:, 0:1]
    r1 = jnp.sum(jnp.where(oh1, before, 0.0), axis=0, keepdims=True)
    r2 = jnp.sum(jnp.where(oh2, before, 0.0), axis=0, keepdims=True)
    route = jnp.where(sub == 0, i1.astype(F32), jnp.where(sub == 1, i2.astype(F32),
                      jnp.where(sub == 2, r1, jnp.where(sub == 3, r2, 0.0))))
    route_ref[...] = route.astype(jnp.int32)
    total = carry[...] + jnp.sum(cnt, axis=1, keepdims=True)
    carry[...] = total
    cnt_ref[...] = total.astype(jnp.int32)


def _oproj_router(x2d, o, w_o, g, rw_t, n_experts):
    t, d = x2d.shape
    tm = PROJ_TILE
    const = lambda i: (0, 0)
    kern = functools.partial(_oproj_router_kernel, tm=tm, d=d, n_experts=n_experts)
    return pl.pallas_call(
        kern,
        grid=(t // tm,),
        in_specs=[
            pl.BlockSpec((tm, d), lambda i: (i, 0)),
            pl.BlockSpec((tm, d), lambda i: (i, 0)),
            _resident((d, d)),
            pl.BlockSpec((1, d), const),
            pl.BlockSpec((SUBLANES, d), const),
        ],
        out_specs=[
            pl.BlockSpec((tm, d), lambda i: (i, 0)),
            pl.BlockSpec((tm * (d // LANES), LANES), lambda i: (i, 0)),
            pl.BlockSpec((SUBLANES, tm), lambda i: (0, i)),
            pl.BlockSpec((SUBLANES, tm), lambda i: (0, i)),
            pl.BlockSpec((SUBLANES, LANES), const),
        ],
        out_shape=[
            jax.ShapeDtypeStruct((t, d), F32),
            jax.ShapeDtypeStruct((t * (d // LANES), LANES), F32),
            jax.ShapeDtypeStruct((SUBLANES, t), F32),
            jax.ShapeDtypeStruct((SUBLANES, t), jnp.int32),
            jax.ShapeDtypeStruct((SUBLANES, LANES), jnp.int32),
        ],
        scratch_shapes=[pltpu.VMEM((SUBLANES, LANES), F32)],
        compiler_params=_params("arbitrary"),
        name="oproj_router",
    )(x2d, o, w_o, g, rw_t)


def _pad_bits(tile):
    return [1 << b for b in reversed(range(int(math.log2(tile))))]


def _dispatch_kernel(s0_ref, s1_ref, pad_start_ref, pad_len_ref, h_ref, hs_hbm, zeros, sem, zsem,
                     *, tb, nc, n_experts, tile):
    i = pl.program_id(0)
    base = i * tb

    def rows(ref, first, count):
        start = first * nc
        if not isinstance(start, int):
            start = pl.multiple_of(start, nc)
        return ref.at[pl.ds(start, count * nc), :]

    def row_copy(r, slot):
        return pltpu.make_async_copy(rows(h_ref, r, 1), rows(hs_hbm, slot, 1), sem)

    def issue(g, carry):
        for u in range(DMA_UNROLL):
            r = g * DMA_UNROLL + u
            row_copy(r, s0_ref[base + r]).start(priority=0)
            row_copy(r, s1_ref[base + r]).start(priority=1)
        return carry

    lax.fori_loop(0, tb // DMA_UNROLL, issue, 0)

    def pad_copies(do):
        for e in range(n_experts):
            n = pad_len_ref[e]
            pos = pad_start_ref[e]
            for bit in _pad_bits(tile):
                @pl.when((n & bit) != 0)
                def _(pos=pos, bit=bit):
                    do(pltpu.make_async_copy(rows(zeros, 0, bit), rows(hs_hbm, pos, bit), zsem))
                pos = pos + (n & bit)
        for k in range(n_experts):
            @pl.when(k < pad_len_ref[n_experts])
            def _(k=k):
                pos = pad_start_ref[n_experts] + k * tile
                do(pltpu.make_async_copy(zeros, rows(hs_hbm, pos, tile), zsem))

    @pl.when(i == 0)
    def _():
        zeros[...] = jnp.zeros_like(zeros)
        pad_copies(lambda cp: cp.start())
        pad_copies(lambda cp: cp.wait())

    def drain(g, carry):
        for u in range(2 * DMA_UNROLL):
            row_copy(0, 0).wait()
        return carry

    lax.fori_loop(0, tb // DMA_UNROLL, drain, 0)


def _dispatch(slot0, slot1, pad_start, pad_len, h8, nc, n_slots, n_experts):
    t = h8.shape[0] // nc
    tb = DISPATCH_BLOCK
    kern = functools.partial(_dispatch_kernel, tb=tb, nc=nc, n_experts=n_experts, tile=GROUP_TILE)
    grid_spec = pltpu.PrefetchScalarGridSpec(
        num_scalar_prefetch=4,
        grid=(t // tb,),
        in_specs=[pl.BlockSpec((tb * nc, LANES), lambda i, *_: (i, 0))],
        out_specs=pl.BlockSpec(memory_space=pl.ANY),
        scratch_shapes=[
            pltpu.VMEM((GROUP_TILE * nc, LANES), F32),
            pltpu.SemaphoreType.DMA,
            pltpu.SemaphoreType.DMA,
        ],
    )
    return pl.pallas_call(
        kern,
        grid_spec=grid_spec,
        out_shape=jax.ShapeDtypeStruct((n_slots * nc, LANES), F32),
        compiler_params=_params("arbitrary"),
        name="moe_dispatch",
    )(slot0, slot1, pad_start, pad_len, h8)


def _moe_group_kernel(te_ref, nt_ref, valid_ref, hs_ref, wg_ref, wu_ref, wd_ref, ys_ref, *, tm, d, f_exp):
    j = pl.program_id(0)
    nc = d // LANES
    valid = valid_ref[j]

    def expert_ffn(rows):
        h = jnp.concatenate([hs_ref[pl.ds(c, rows, stride=nc), :].astype(BF16) for c in range(nc)], axis=1)
        y = None
        for lo in range(0, f_exp, MOE_F_CHUNK):
            hi = min(lo + MOE_F_CHUNK, f_exp)
            gate = _dot(h, wg_ref[:, lo:hi])
            up = _dot(h, wu_ref[:, lo:hi])
            a = (gate * jax.nn.sigmoid(gate) * up).astype(BF16)
            p = _dot(a, wd_ref[lo:hi, :])
            y = p if y is None else y + p
        for c in range(nc):
            ys_ref[pl.ds(c, rows, stride=nc), :] = y[:, c * LANES:(c + 1) * LANES]

    half = tm // 2

    @pl.when(valid > half)
    def _():
        expert_ffn(tm)

    @pl.when(jnp.logical_and(valid > 0, valid <= half))
    def _():
        expert_ffn(half)
        ys_ref[half * nc:, :] = jnp.zeros(((tm - half) * nc, LANES), F32)

    @pl.when(valid == 0)
    def _():
        ys_ref[...] = jnp.zeros_like(ys_ref)


def _moe_group(tile_expert, n_tiles, tile_valid, hs2d, w_gu, w_down, max_tiles):
    n_e, d, f2 = w_gu.shape
    f_exp = f2 // 2
    tm = GROUP_TILE
    nc = d // LANES

    def row_map(j, te, nt, valid):
        return (jnp.minimum(j, nt[0] - 1), 0)

    kern = functools.partial(_moe_group_kernel, tm=tm, d=d, f_exp=f_exp)
    grid_spec = pltpu.PrefetchScalarGridSpec(
        num_scalar_prefetch=3,
        grid=(max_tiles,),
        in_specs=[
            pl.BlockSpec((tm * nc, LANES), row_map),
            pl.BlockSpec((None, d, f_exp), lambda j, te, nt, valid: (te[j], 0, 0)),
            pl.BlockSpec((None, d, f_exp), lambda j, te, nt, valid: (te[j], 0, 1)),
            pl.BlockSpec((None, f_exp, d), lambda j, te, nt, valid: (te[j], 0, 0)),
        ],
        out_specs=pl.BlockSpec((tm * nc, LANES), lambda j, te, nt, valid: (j, 0)),
    )
    return pl.pallas_call(
        kern,
        grid_spec=grid_spec,
        out_shape=jax.ShapeDtypeStruct(hs2d.shape, F32),
        compiler_params=pltpu.CompilerParams(dimension_semantics=("arbitrary",),
                                             vmem_limit_bytes=MOE_VMEM_LIMIT),
        name="moe_group",
    )(tile_expert, n_tiles, tile_valid, hs2d, w_gu, w_gu, w_down)


def _combine_kernel(s0_ref, s1_ref, x_ref, gate_ref, ys_hbm, o_ref, b00, b01, b10, b11, sems, *, tm, d):
    i = pl.program_id(0)
    n = pl.num_programs(0)
    nc = d // LANES
    bufs = ((b00, b01), (b10, b11))

    def row_copy(slot, par, k, r):
        start = r * nc
        if not isinstance(start, int):
            start = pl.multiple_of(start, nc)
        src = ys_hbm.at[pl.ds(pl.multiple_of(slot * nc, nc), nc), :]
        return pltpu.make_async_copy(src, bufs[par][k].at[pl.ds(start, nc), :], sems.at[par])

    def issue(block, par):
        base = block * tm

        def body(g, carry):
            for u in range(DMA_UNROLL):
                r = g * DMA_UNROLL + u
                row_copy(s0_ref[base + r], par, 0, r).start(priority=0)
                row_copy(s1_ref[base + r], par, 1, r).start(priority=1)
            return carry

        lax.fori_loop(0, tm // DMA_UNROLL, body, 0)

    def drain(par):
        def body(g, carry):
            for u in range(DMA_UNROLL):
                row_copy(0, par, 0, 0).wait()
                row_copy(0, par, 1, 0).wait()
            return carry

        lax.fori_loop(0, tm // DMA_UNROLL, body, 0)

    @pl.when(i == 0)
    def _():
        issue(0, 0)

    for par in range(2):
        @pl.when(i % 2 == par)
        def _(par=par):
            @pl.when(i + 1 < n)
            def _():
                issue(i + 1, 1 - par)

            drain(par)
            gates = jnp.transpose(gate_ref[...])
            g0 = gates[:, 0:1]
            g1 = gates[:, 1:2]
            for c in range(nc):
                y0 = bufs[par][0][pl.ds(c, tm, stride=nc), :]
                y1 = bufs[par][1][pl.ds(c, tm, stride=nc), :]
                cols = slice(c * LANES, (c + 1) * LANES)
                o_ref[:, cols] = x_ref[:, cols] + (g0 * y0 + g1 * y1)


def _combine(slot0, slot1, x3, gates, ys2d):
    t, d = x3.shape
    tm = COMBINE_BLOCK
    nc = d // LANES
    grid_spec = pltpu.PrefetchScalarGridSpec(
        num_scalar_prefetch=2,
        grid=(t // tm,),
        in_specs=[
            pl.BlockSpec((tm, d), lambda i, s0, s1: (i, 0)),
            pl.BlockSpec((SUBLANES, tm), lambda i, s0, s1: (0, i)),
            pl.BlockSpec(memory_space=pl.ANY),
        ],
        out_specs=pl.BlockSpec((tm, d), lambda i, s0, s1: (i, 0)),
        scratch_shapes=[
            pltpu.VMEM((tm * nc, LANES), F32),
            pltpu.VMEM((tm * nc, LANES), F32),
            pltpu.VMEM((tm * nc, LANES), F32),
            pltpu.VMEM((tm * nc, LANES), F32),
            pltpu.SemaphoreType.DMA((2,)),
        ],
    )
    return pl.pallas_call(
        functools.partial(_combine_kernel, tm=tm, d=d),
        grid_spec=grid_spec,
        out_shape=jax.ShapeDtypeStruct((t, d), F32),
        compiler_params=_params("arbitrary"),
        name="moe_combine",
    )(slot0, slot1, x3, gates, ys2d)


def _routing_tables(route, counts, n_experts, max_tiles):
    tile = GROUP_TILE
    padded = ((counts + tile - 1) // tile) * tile
    ends = jnp.cumsum(padded)
    offs = ends - padded
    slots = []
    for k in range(TOP_K):
        expert, rank = route[k], route[TOP_K + k]
        base = jnp.zeros_like(expert)
        for e in range(n_experts):
            base = jnp.where(expert == e, offs[e], base)
        slots.append(base + rank)
    tile_ends = ends // tile
    n_tiles = tile_ends[-1]
    j = jnp.arange(max_tiles, dtype=jnp.int32)
    tile_expert = jnp.sum((jnp.minimum(j, n_tiles - 1)[:, None] >= tile_ends[None, :]).astype(jnp.int32), axis=-1)
    row_end = jnp.sum(jnp.where(tile_expert[:, None] == jnp.arange(n_experts, dtype=jnp.int32)[None, :],
                                (offs + counts)[None, :], 0), axis=-1)
    tile_valid = jnp.where(j < n_tiles, jnp.clip(row_end - j * tile, 0, tile), 0)
    pad_start = jnp.concatenate([offs + counts, ends[-1:]]).astype(jnp.int32)
    pad_len = jnp.concatenate([padded - counts, max_tiles - n_tiles.reshape(1)]).astype(jnp.int32)
    return (slots[0], slots[1], pad_start, pad_len,
            tile_expert.astype(jnp.int32), n_tiles.reshape(1).astype(jnp.int32), tile_valid.astype(jnp.int32))


def _lambda_init(layer_idx_1based):
    return 0.8 - 0.6 * math.exp(-0.3 * (layer_idx_1based - 1))


def kernel(x, ln_mix, ln_ffn, conv_w_in, conv_w, conv_w_out, ln_kv, w_kv, k_norm, attn_w_q, q_norm, lam_params,
           sub_norm, attn_w_o, ffn_w_gu, ffn_w_down, router_w, moe_w_gu, moe_w_down):
    batch, seq, d = x.shape
    t = batch * seq
    n_experts = router_w.shape[-1]
    assert ln_mix.shape[0] == 2 and conv_w_in.shape[0] == 1 and attn_w_q.shape[0] == 1
    assert seq % ROW_TILE == 0 and seq % ATTN_TILE == 0 and d % LANES == 0
    nc = d // LANES
    x2d = x.reshape(t, d)

    x1 = _mixer_a(x2d, ln_mix[0:1], conv_w_in[0], conv_w[0], conv_w_out[0], seq)
    x2 = _ffn_dense(x1, ln_ffn[0:1], ffn_w_gu[0], ffn_w_down[0])

    n_k = d
    w_kt = w_kv[:, :n_k].T
    w_v = w_kv[:, n_k:]
    q, kt, v = _qkv(x2, ln_mix[1:2], ln_kv.reshape(1, d), attn_w_q[0], w_kt, w_v,
                    k_norm.reshape(HEAD_DIM, 1))

    score_bound = (math.sqrt(HEAD_DIM) * jnp.max(jnp.abs(q_norm[0])) * jnp.max(jnp.abs(k_norm))).reshape(1)
    o, moe_gu_bf, moe_down_bf = _attention(score_bound, q, kt, v, jnp.tile(q_norm[0:1], (1, 2)), lam_params[0],
                                           sub_norm[0:1], moe_w_gu[0], moe_w_down[0], batch, seq, _lambda_init(2))

    assert n_experts <= SUBLANES
    rw_t = jnp.pad(router_w[0].T, ((0, SUBLANES - n_experts), (0, 0)))
    x3, h8, gates, route, counts = _oproj_router(x2, o, attn_w_o[0], ln_ffn[1:2], rw_t, n_experts)
    max_tiles = (TOP_K * t) // GROUP_TILE + n_experts
    n_slots = max_tiles * GROUP_TILE
    slot0, slot1, pad_start, pad_len, tile_expert, n_tiles, tile_valid = _routing_tables(
        route, counts[:n_experts, 0], n_experts, max_tiles)
    hs = _dispatch(slot0, slot1, pad_start, pad_len, h8, nc, n_slots, n_experts)
    ys = _moe_group(tile_expert, n_tiles, tile_valid, hs, moe_gu_bf, moe_down_bf, max_tiles)
    out = _combine(slot0, slot1, x3, gates, ys)
    return out.reshape(batch, seq, d)
```

```python
import functools
import math

import jax
import jax.numpy as jnp
from jax import lax
from jax.experimental import pallas as pl
from jax.experimental.pallas import tpu as pltpu

F32 = jnp.float32
BF16 = jnp.bfloat16

EPS = 1e-6
HEAD_DIM = 64
V_DIM = 2 * HEAD_DIM
CONV_WIDTH = 3
TOP_K = 2

LANES = 128
SUBLANES = 8
VMEM_LIMIT = 56 * 1024 * 1024

ROW_TILE = 512
PROJ_TILE = 1024
ATTN_TILE = 256
ATTN_HEADS_PER_STEP = 2
GROUP_TILE = 512
MOE_F_CHUNK = 1792
MOE_VMEM_LIMIT = 62 * 1024 * 1024
DISPATCH_BLOCK = 2048
COMBINE_BLOCK = 256
DMA_UNROLL = 16

LOG2E = 1.4426950408889634
SAFE_SHIFT = 40.0


def _rms_scale(x):
    return x * lax.rsqrt(jnp.mean(x * x, axis=-1, keepdims=True) + EPS)


def _dot(a, b):
    return jnp.dot(a, b, preferred_element_type=F32)


def _resident(shape):
    return pl.BlockSpec(shape, lambda *_: (0,) * len(shape), pipeline_mode=pl.Buffered(1))


def _params(*sem):
    return pltpu.CompilerParams(dimension_semantics=sem, vmem_limit_bytes=VMEM_LIMIT)


def _mixer_a_kernel(x_ref, g_ref, win_ref, cw_ref, wout_ref, o_ref, ubuf, *, tm, d, tiles_per_seq):
    i = pl.program_id(0)
    x = x_ref[...]
    h = (_rms_scale(x) * g_ref[...]).astype(BF16)
    c = _dot(h, win_ref[:, d:2 * d].astype(BF16))
    v = _dot(h, win_ref[:, 2 * d:].astype(BF16))
    u = c * v

    @pl.when(i % tiles_per_seq == 0)
    def _():
        ubuf[0:SUBLANES, :] = jnp.zeros((SUBLANES, d), F32)

    ubuf[SUBLANES:tm + SUBLANES, :] = u
    u1 = ubuf[SUBLANES - 1:tm + SUBLANES - 1, :]
    u2 = ubuf[SUBLANES - 2:tm + SUBLANES - 2, :]
    cw = cw_ref[...]
    z = u2 * cw[0:1] + u1 * cw[1:2] + u * cw[2:3]
    b = _dot(h, win_ref[:, 0:d].astype(BF16))
    y = (b * z).astype(BF16)
    o_ref[...] = x + _dot(y, wout_ref[...].astype(BF16))
    ubuf[0:SUBLANES, :] = ubuf[tm:tm + SUBLANES, :]


def _mixer_a(x2d, g, w_in, conv_w, w_out, seq):
    t, d = x2d.shape
    tm = ROW_TILE
    kern = functools.partial(_mixer_a_kernel, tm=tm, d=d, tiles_per_seq=seq // tm)
    return pl.pallas_call(
        kern,
        grid=(t // tm,),
        in_specs=[
            pl.BlockSpec((tm, d), lambda i: (i, 0)),
            pl.BlockSpec((1, d), lambda i: (0, 0)),
            _resident((d, 3 * d)),
            pl.BlockSpec((CONV_WIDTH, d), lambda i: (0, 0)),
            _resident((d, d)),
        ],
        out_specs=pl.BlockSpec((tm, d), lambda i: (i, 0)),
        out_shape=jax.ShapeDtypeStruct((t, d), F32),
        scratch_shapes=[pltpu.VMEM((tm + SUBLANES, d), F32)],
        compiler_params=_params("arbitrary"),
        name="mixer_a",
    )(x2d, g, w_in, conv_w, w_out)


def _ffn_chunks(f):
    step = 768
    return [(lo, min(lo + step, f)) for lo in range(0, f, step)]


def _ffn_kernel(x_ref, g_ref, wgu_ref, wd_ref, o_ref, *, f):
    x = x_ref[...]
    h = (_rms_scale(x) * g_ref[...]).astype(BF16)
    acc = x
    for lo, hi in _ffn_chunks(f):
        gate = _dot(h, wgu_ref[:, lo:hi].astype(BF16))
        up = _dot(h, wgu_ref[:, f + lo:f + hi].astype(BF16))
        a = (gate * jax.nn.sigmoid(gate) * up).astype(BF16)
        acc = acc + _dot(a, wd_ref[lo:hi, :].astype(BF16))
    o_ref[...] = acc


def _ffn_dense(x2d, g, w_gu, w_down):
    t, d = x2d.shape
    f = w_down.shape[0]
    tm = ROW_TILE
    return pl.pallas_call(
        functools.partial(_ffn_kernel, f=f),
        grid=(t // tm,),
        in_specs=[
            pl.BlockSpec((tm, d), lambda i: (i, 0)),
            pl.BlockSpec((1, d), lambda i: (0, 0)),
            _resident((d, 2 * f)),
            _resident((f, d)),
        ],
        out_specs=pl.BlockSpec((tm, d), lambda i: (i, 0)),
        out_shape=jax.ShapeDtypeStruct((t, d), F32),
        compiler_params=_params("parallel"),
        name="ffn_dense",
    )(x2d, g, w_gu, w_down)


def _qkv_kernel(x_ref, gq_ref, gkv_ref, wq_ref, wkt_ref, wv_ref, kn_ref, q_ref, kt_ref, v_ref, *, tm, d):
    y = _rms_scale(x_ref[...])
    hq = (y * gq_ref[...]).astype(BF16)
    hkv = (y * gkv_ref[...]).astype(BF16)
    q_ref[...] = _dot(hq, wq_ref[...].astype(BF16)).astype(BF16)
    v_ref[...] = _dot(hkv, wv_ref[...].astype(BF16)).astype(BF16)
    kt = lax.dot_general(wkt_ref[...].astype(BF16), hkv, (((1,), (1,)), ((), ())), preferred_element_type=F32)
    k3 = kt.reshape(d // HEAD_DIM, HEAD_DIM, tm)
    k3 = k3 * lax.rsqrt(jnp.mean(k3 * k3, axis=1, keepdims=True) + EPS) * kn_ref[...][None]
    kt_ref[...] = k3.reshape(d, tm).astype(BF16)


def _qkv(x2d, g_q, g_kv, w_q, w_kt, w_v, k_norm_col):
    t, d = x2d.shape
    tm = PROJ_TILE
    const = lambda i: (0, 0)
    return pl.pallas_call(
        functools.partial(_qkv_kernel, tm=tm, d=d),
        grid=(t // tm,),
        in_specs=[
            pl.BlockSpec((tm, d), lambda i: (i, 0)),
            pl.BlockSpec((1, d), const),
            pl.BlockSpec((1, d), const),
            _resident((d, d)),
            _resident((d, d)),
            _resident((d, d)),
            pl.BlockSpec((HEAD_DIM, 1), const),
        ],
        out_specs=[
            pl.BlockSpec((tm, d), lambda i: (i, 0)),
            pl.BlockSpec((d, tm), lambda i: (0, i)),
            pl.BlockSpec((tm, d), lambda i: (i, 0)),
        ],
        out_shape=[
            jax.ShapeDtypeStruct((t, d), BF16),
            jax.ShapeDtypeStruct((d, t), BF16),
            jax.ShapeDtypeStruct((t, d), BF16),
        ],
        compiler_params=_params("parallel"),
        name="qkv_proj",
    )(x2d, g_q, g_kv, w_q, w_kt, w_v, k_norm_col)


def _group_mean_sq(x, ones_bd, group):
    return _dot((x * x).astype(BF16), ones_bd) * (1.0 / group)


def _attn_kernel(bound_ref, q_ref, kt_ref, v_ref, qn_ref, lam_ref, sn_ref, wgu_ref, wdn_ref,
                 o_ref, wgu_bf_ref, wdn_bf_ref, v1_ref, qn_s, *, seq, tq, heads, lam_init):
    wgu_bf_ref[...] = wgu_ref[...].astype(BF16)
    wdn_bf_ref[...] = wdn_ref[...].astype(BF16)
    lp = lam_ref[...]
    lam = (jnp.exp(jnp.sum(lp[0:1] * lp[1:2], axis=-1, keepdims=True))
           - jnp.exp(jnp.sum(lp[2:3] * lp[3:4], axis=-1, keepdims=True)) + lam_init)
    row = lax.broadcasted_iota(jnp.int32, (tq, tq), 0)
    col = lax.broadcasted_iota(jnp.int32, (tq, tq), 1)
    causal = col <= row
    gi = lax.broadcasted_iota(jnp.int32, (V_DIM, V_DIM), 0) // HEAD_DIM
    gj = lax.broadcasted_iota(jnp.int32, (V_DIM, V_DIM), 1) // HEAD_DIM
    ones_bd = jnp.where(gi == gj, 1.0, 0.0).astype(BF16)
    qgain = qn_ref[...] * (HEAD_DIM ** -0.5 * LOG2E)
    maps = [slice(c * HEAD_DIM, (c + 1) * HEAD_DIM) for c in range(2)]
    n_q = seq // tq

    v1_ref[:, V_DIM:] = jnp.ones((seq, V_DIM), BF16)
    for hh in range(heads):
        hcols = slice(hh * V_DIM, (hh + 1) * V_DIM)
        hrow = hh * V_DIM
        v1_ref[:, 0:V_DIM] = v_ref[:, hcols]
        q = q_ref[:, hcols].astype(F32)
        ms = _group_mean_sq(q, ones_bd, HEAD_DIM)
        qn_s[...] = (q * lax.rsqrt(ms + EPS) * qgain).astype(BF16)

        def finish(q0, r1, r2, hcols=hcols):
            o = r1[:, 0:V_DIM] / r1[:, V_DIM:] - lam * (r2[:, 0:V_DIM] / r2[:, V_DIM:])
            o = _rms_scale(o) * sn_ref[...] * (1.0 - lam_init)
            o_ref[pl.ds(q0, tq), hcols] = o.astype(BF16)

        def kt_rows(sl, hrow=hrow):
            return slice(hrow + sl.start, hrow + sl.stop)

        @pl.when(bound_ref[0] <= SAFE_SHIFT)
        def _fixed_shift(finish=finish, kt_rows=kt_rows):
            shift = bound_ref[0] * LOG2E

            def scores(qi):
                q0 = qi * tq
                out = []
                for sl in maps:
                    qc = qn_s[q0:q0 + tq, sl]
                    s_diag = _dot(qc, kt_ref[kt_rows(sl), q0:q0 + tq])
                    s_low = _dot(qc, kt_ref[kt_rows(sl), 0:q0]) if q0 > 0 else None
                    out.append((s_diag, s_low))
                return out

            order = list(reversed(range(n_q)))
            pending = scores(order[0])
            for idx, qi in enumerate(order):
                q0 = qi * tq
                cur = pending
                if idx + 1 < n_q:
                    pending = scores(order[idx + 1])
                res = []
                for s_diag, s_low in cur:
                    p = jnp.where(causal, jnp.exp2(s_diag - shift), 0.0).astype(BF16)
                    r = _dot(p, v1_ref[q0:q0 + tq, :])
                    if s_low is not None:
                        r = r + _dot(jnp.exp2(s_low - shift).astype(BF16), v1_ref[0:q0, :])
                    res.append(r)
                finish(q0, *res)

        @pl.when(jnp.logical_not(bound_ref[0] <= SAFE_SHIFT))
        def _running_max(finish=finish, kt_rows=kt_rows):
            def q_body(qi, carry):
                q0 = pl.multiple_of(qi * tq, tq)
                qs = [qn_s[pl.ds(q0, tq), sl] for sl in maps]

                def tile(j, state, masked):
                    k0 = pl.multiple_of(j * tq, tq)
                    out = []
                    for c, sl in enumerate(maps):
                        m, r = state[c]
                        s = _dot(qs[c], kt_ref[kt_rows(sl), pl.ds(k0, tq)])
                        if masked:
                            s = jnp.where(causal, s, -jnp.inf)
                        m_new = jnp.maximum(m, jnp.max(s, axis=-1, keepdims=True))
                        p = jnp.exp2(s - m_new).astype(BF16)
                        r = jnp.exp2(m - m_new) * r + _dot(p, v1_ref[pl.ds(k0, tq), :])
                        out.append((m_new, r))
                    return tuple(out)

                one = (jnp.full((tq, 1), -jnp.inf, F32), jnp.zeros((tq, 2 * V_DIM), F32))
                state = lax.fori_loop(0, qi, lambda j, st: tile(j, st, False), (one, one))
                (_, r1), (_, r2) = tile(qi, state, True)
                finish(q0, r1, r2)
                return carry

            lax.fori_loop(0, n_q, q_body, 0)


def _attention(score_bound, q, kt, v, q_norm2, lam_params, sub_norm, w_gu, w_down, batch, seq, lam_init):
    t, d = q.shape
    n_heads = d // V_DIM
    hps = ATTN_HEADS_PER_STEP
    assert n_heads % hps == 0
    hsteps = n_heads // hps
    steps = batch * hsteps
    wgu2 = w_gu.reshape(-1, w_gu.shape[-1])
    wdn2 = w_down.reshape(-1, w_down.shape[-1])
    gu_rows, dn_rows = wgu2.shape[0] // steps, wdn2.shape[0] // steps
    assert wgu2.shape[0] % steps == 0 and wdn2.shape[0] % steps == 0 and gu_rows % 16 == 0 and dn_rows % 16 == 0
    slab = lambda b, h: (b * hsteps + h, 0)
    hw = hps * V_DIM
    kern = functools.partial(_attn_kernel, seq=seq, tq=ATTN_TILE, heads=hps, lam_init=lam_init)
    o, wgu_bf, wdn_bf = pl.pallas_call(
        kern,
        grid=(batch, hsteps),
        in_specs=[
            pl.BlockSpec(memory_space=pltpu.SMEM),
            pl.BlockSpec((seq, hw), lambda b, h: (b, h)),
            pl.BlockSpec((hw, seq), lambda b, h: (h, b)),
            pl.BlockSpec((seq, hw), lambda b, h: (b, h)),
            pl.BlockSpec((1, V_DIM), lambda b, h: (0, 0)),
            pl.BlockSpec((4, HEAD_DIM), lambda b, h: (0, 0)),
            pl.BlockSpec((1, V_DIM), lambda b, h: (0, 0)),
            pl.BlockSpec((gu_rows, wgu2.shape[1]), slab),
            pl.BlockSpec((dn_rows, wdn2.shape[1]), slab),
        ],
        out_specs=[
            pl.BlockSpec((seq, hw), lambda b, h: (b, h)),
            pl.BlockSpec((gu_rows, wgu2.shape[1]), slab),
            pl.BlockSpec((dn_rows, wdn2.shape[1]), slab),
        ],
        out_shape=[
            jax.ShapeDtypeStruct((t, d), BF16),
            jax.ShapeDtypeStruct(wgu2.shape, BF16),
            jax.ShapeDtypeStruct(wdn2.shape, BF16),
        ],
        scratch_shapes=[pltpu.VMEM((seq, 2 * V_DIM), BF16), pltpu.VMEM((seq, V_DIM), BF16)],
        compiler_params=_params("parallel", "parallel"),
        name="diff_attn",
    )(score_bound, q, kt, v, q_norm2, lam_params, sub_norm, wgu2, wdn2)
    return o, wgu_bf.reshape(w_gu.shape), wdn_bf.reshape(w_down.shape)


def _nt_dot(a, b):
    return lax.dot_general(a, b, (((1,), (1,)), ((), ())), preferred_element_type=F32)


def _oproj_router_kernel(x_ref, o_ref, wo_ref, g_ref, rwt_ref,
                         x3_ref, h8_ref, gate_ref, route_ref, cnt_ref, carry,
                         *, tm, d, n_experts):
    i = pl.program_id(0)

    @pl.when(i == 0)
    def _():
        carry[...] = jnp.zeros_like(carry)

    x3 = x_ref[...] + _dot(o_ref[...], wo_ref[...].astype(BF16))
    x3_ref[...] = x3
    h = _rms_scale(x3) * g_ref[...]
    for c in range(d // LANES):
        h8_ref[pl.ds(c, tm, stride=d // LANES), :] = h[:, c * LANES:(c + 1) * LANES]

    rwt = rwt_ref[...]
    h_hi = h.astype(BF16)
    h_lo = (h - h_hi.astype(F32)).astype(BF16)
    w_hi = rwt.astype(BF16)
    w_lo = (rwt - w_hi.astype(F32)).astype(BF16)
    logits = _nt_dot(w_hi, h_hi) + (_nt_dot(w_lo, h_hi) + _nt_dot(w_hi, h_lo))

    rows = rwt.shape[0]
    sub = lax.broadcasted_iota(jnp.int32, (rows, tm), 0)
    lane = lax.broadcasted_iota(jnp.int32, (rows, tm), 1)
    neg = -jnp.inf
    l1 = jnp.where(sub < n_experts, logits, neg)
    m1 = jnp.max(l1, axis=0, keepdims=True)
    i1 = jnp.min(jnp.where(l1 == m1, sub, rows), axis=0, keepdims=True)
    l2 = jnp.where(sub == i1, neg, l1)
    m2 = jnp.max(l2, axis=0, keepdims=True)
    i2 = jnp.min(jnp.where(l2 == m2, sub, rows), axis=0, keepdims=True)
    e2 = jnp.exp(m2 - m1)
    den = 1.0 + e2
    gate_ref[...] = jnp.where(sub == 0, 1.0 / den, jnp.where(sub == 1, e2 / den, 0.0))

    oh1 = sub == i1
    oh2 = sub == i2
    cnt = jnp.where(oh1, 1.0, 0.0) + jnp.where(oh2, 1.0, 0.0)
    incl = cnt
    shift = 1
    while shift < tm:
        incl = incl + jnp.where(lane >= shift, pltpu.roll(incl, shift, 1), 0.0)
        shift *= 2
    before = incl - cnt + carry[:, 0:1]
    r1 = jnp.sum(jnp.where(oh1, before, 0.0), axis=0, keepdims=True)
    r2 = jnp.sum(jnp.where(oh2, before, 0.0), axis=0, keepdims=True)
    route = jnp.where(sub == 0, i1.astype(F32), jnp.where(sub == 1, i2.astype(F32),
                      jnp.where(sub == 2, r1, jnp.where(sub == 3, r2, 0.0))))
    route_ref[...] = route.astype(jnp.int32)
    total = carry[...] + jnp.sum(cnt, axis=1, keepdims=True)
    carry[...] = total
    cnt_ref[...] = total.astype(jnp.int32)


def _oproj_router(x2d, o, w_o, g, rw_t, n_experts):
    t, d = x2d.shape
    tm = PROJ_TILE
    const = lambda i: (0, 0)
    kern = functools.partial(_oproj_router_kernel, tm=tm, d=d, n_experts=n_experts)
    return pl.pallas_call(
        kern,
        grid=(t // tm,),
        in_specs=[
            pl.BlockSpec((tm, d), lambda i: (i, 0)),
            pl.BlockSpec((tm, d), lambda i: (i, 0)),
            _resident((d, d)),
            pl.BlockSpec((1, d), const),
            pl.BlockSpec((SUBLANES, d), const),
        ],
        out_specs=[
            pl.BlockSpec((tm, d), lambda i: (i, 0)),
            pl.BlockSpec((tm * (d // LANES), LANES), lambda i: (i, 0)),
            pl.BlockSpec((SUBLANES, tm), lambda i: (0, i)),
            pl.BlockSpec((SUBLANES, tm), lambda i: (0, i)),
            pl.BlockSpec((SUBLANES, LANES), const),
        ],
        out_shape=[
            jax.ShapeDtypeStruct((t, d), F32),
            jax.ShapeDtypeStruct((t * (d // LANES), LANES), F32),
            jax.ShapeDtypeStruct((SUBLANES, t), F32),
            jax.ShapeDtypeStruct((SUBLANES, t), jnp.int32),
            jax.ShapeDtypeStruct((SUBLANES, LANES), jnp.int32),
        ],
        scratch_shapes=[pltpu.VMEM((SUBLANES, LANES), F32)],
        compiler_params=_params("arbitrary"),
        name="oproj_router",
    )(x2d, o, w_o, g, rw_t)


def _pad_bits(tile):
    return [1 << b for b in reversed(range(int(math.log2(tile))))]


def _dispatch_kernel(s0_ref, s1_ref, pad_start_ref, pad_len_ref, h_ref, hs_hbm, zeros, sem, zsem,
                     *, tb, nc, n_experts, tile):
    i = pl.program_id(0)
    base = i * tb

    def rows(ref, first, count):
        start = first * nc
        if not isinstance(start, int):
            start = pl.multiple_of(start, nc)
        return ref.at[pl.ds(start, count * nc), :]

    def row_copy(r, slot):
        return pltpu.make_async_copy(rows(h_ref, r, 1), rows(hs_hbm, slot, 1), sem)

    def issue(g, carry):
        for u in range(DMA_UNROLL):
            r = g * DMA_UNROLL + u
            row_copy(r, s0_ref[base + r]).start(priority=0)
            row_copy(r, s1_ref[base + r]).start(priority=1)
        return carry

    lax.fori_loop(0, tb // DMA_UNROLL, issue, 0)

    def pad_copies(do):
        for e in range(n_experts):
            n = pad_len_ref[e]
            pos = pad_start_ref[e]
            for bit in _pad_bits(tile):
                @pl.when((n & bit) != 0)
                def _(pos=pos, bit=bit):
                    do(pltpu.make_async_copy(rows(zeros, 0, bit), rows(hs_hbm, pos, bit), zsem))
                pos = pos + (n & bit)
        for k in range(n_experts):
            @pl.when(k < pad_len_ref[n_experts])
            def _(k=k):
                pos = pad_start_ref[n_experts] + k * tile
                do(pltpu.make_async_copy(zeros, rows(hs_hbm, pos, tile), zsem))

    @pl.when(i == 0)
    def _():
        zeros[...] = jnp.zeros_like(zeros)
        pad_copies(lambda cp: cp.start())
        pad_copies(lambda cp: cp.wait())

    def drain(g, carry):
        for u in range(2 * DMA_UNROLL):
            row_copy(0, 0).wait()
        return carry

    lax.fori_loop(0, tb // DMA_UNROLL, drain, 0)


def _dispatch(slot0, slot1, pad_start, pad_len, h8, nc, n_slots, n_experts):
    t = h8.shape[0] // nc
    tb = DISPATCH_BLOCK
    kern = functools.partial(_dispatch_kernel, tb=tb, nc=nc, n_experts=n_experts, tile=GROUP_TILE)
    grid_spec = pltpu.PrefetchScalarGridSpec(
        num_scalar_prefetch=4,
        grid=(t // tb,),
        in_specs=[pl.BlockSpec((tb * nc, LANES), lambda i, *_: (i, 0))],
        out_specs=pl.BlockSpec(memory_space=pl.ANY),
        scratch_shapes=[
            pltpu.VMEM((GROUP_TILE * nc, LANES), F32),
            pltpu.SemaphoreType.DMA,
            pltpu.SemaphoreType.DMA,
        ],
    )
    return pl.pallas_call(
        kern,
        grid_spec=grid_spec,
        out_shape=jax.ShapeDtypeStruct((n_slots * nc, LANES), F32),
        compiler_params=_params("arbitrary"),
        name="moe_dispatch",
    )(slot0, slot1, pad_start, pad_len, h8)


def _moe_group_kernel(te_ref, nt_ref, valid_ref, hs_ref, wg_ref, wu_ref, wd_ref, ys_ref, *, tm, d, f_exp):
    j = pl.program_id(0)
    nc = d // LANES
    valid = valid_ref[j]

    def expert_ffn(rows):
        h = jnp.concatenate([hs_ref[pl.ds(c, rows, stride=nc), :].astype(BF16) for c in range(nc)], axis=1)
        y = None
        for lo in range(0, f_exp, MOE_F_CHUNK):
            hi = min(lo + MOE_F_CHUNK, f_exp)
            gate = _dot(h, wg_ref[:, lo:hi])
            up = _dot(h, wu_ref[:, lo:hi])
            a = (gate * jax.nn.sigmoid(gate) * up).astype(BF16)
            p = _dot(a, wd_ref[lo:hi, :])
            y = p if y is None else y + p
        for c in range(nc):
            ys_ref[pl.ds(c, rows, stride=nc), :] = y[:, c * LANES:(c + 1) * LANES]

    half = tm // 2

    @pl.when(valid > half)
    def _():
        expert_ffn(tm)

    @pl.when(jnp.logical_and(valid > 0, valid <= half))
    def _():
        expert_ffn(half)
        ys_ref[half * nc:, :] = jnp.zeros(((tm - half) * nc, LANES), F32)

    @pl.when(valid == 0)
    def _():
        ys_ref[...] = jnp.zeros_like(ys_ref)


def _moe_group(tile_expert, n_tiles, tile_valid, hs2d, w_gu, w_down, max_tiles):
    n_e, d, f2 = w_gu.shape
    f_exp = f2 // 2
    tm = GROUP_TILE
    nc = d // LANES

    def row_map(j, te, nt, valid):
        return (jnp.minimum(j, nt[0] - 1), 0)

    kern = functools.partial(_moe_group_kernel, tm=tm, d=d, f_exp=f_exp)
    grid_spec = pltpu.PrefetchScalarGridSpec(
        num_scalar_prefetch=3,
        grid=(max_tiles,),
        in_specs=[
            pl.BlockSpec((tm * nc, LANES), row_map),
            pl.BlockSpec((None, d, f_exp), lambda j, te, nt, valid: (te[j], 0, 0)),
            pl.BlockSpec((None, d, f_exp), lambda j, te, nt, valid: (te[j], 0, 1)),
            pl.BlockSpec((None, f_exp, d), lambda j, te, nt, valid: (te[j], 0, 0)),
        ],
        out_specs=pl.BlockSpec((tm * nc, LANES), lambda j, te, nt, valid: (j, 0)),
    )
    return pl.pallas_call(
        kern,
        grid_spec=grid_spec,
        out_shape=jax.ShapeDtypeStruct(hs2d.shape, F32),
        compiler_params=pltpu.CompilerParams(dimension_semantics=("arbitrary",),
                                             vmem_limit_bytes=MOE_VMEM_LIMIT),
        name="moe_group",
    )(tile_expert, n_tiles, tile_valid, hs2d, w_gu, w_gu, w_down)


def _combine_kernel(s0_ref, s1_ref, x_ref, gate_ref, ys_hbm, o_ref, b00, b01, b10, b11, sems, *, tm, d):
    i = pl.program_id(0)
    n = pl.num_programs(0)
    nc = d // LANES
    bufs = ((b00, b01), (b10, b11))

    def row_copy(slot, par, k, r):
        start = r * nc
        if not isinstance(start, int):
            start = pl.multiple_of(start, nc)
        src = ys_hbm.at[pl.ds(pl.multiple_of(slot * nc, nc), nc), :]
        return pltpu.make_async_copy(src, bufs[par][k].at[pl.ds(start, nc), :], sems.at[par])

    def issue(block, par):
        base = block * tm

        def body(g, carry):
            for u in range(DMA_UNROLL):
                r = g * DMA_UNROLL + u
                row_copy(s0_ref[base + r], par, 0, r).start(priority=0)
                row_copy(s1_ref[base + r], par, 1, r).start(priority=1)
            return carry

        lax.fori_loop(0, tm // DMA_UNROLL, body, 0)

    def drain(par):
        def body(g, carry):
            for u in range(DMA_UNROLL):
                row_copy(0, par, 0, 0).wait()
                row_copy(0, par, 1, 0).wait()
            return carry

        lax.fori_loop(0, tm // DMA_UNROLL, body, 0)

    @pl.when(i == 0)
    def _():
        issue(0, 0)

    for par in range(2):
        @pl.when(i % 2 == par)
        def _(par=par):
            @pl.when(i + 1 < n)
            def _():
                issue(i + 1, 1 - par)

            drain(par)
            gates = jnp.transpose(gate_ref[...])
            g0 = gates[:, 0:1]
            g1 = gates[:, 1:2]
            for c in range(nc):
                y0 = bufs[par][0][pl.ds(c, tm, stride=nc), :]
                y1 = bufs[par][1][pl.ds(c, tm, stride=nc), :]
                cols = slice(c * LANES, (c + 1) * LANES)
                o_ref[:, cols] = x_ref[:, cols] + (g0 * y0 + g1 * y1)


def _combine(slot0, slot1, x3, gates, ys2d):
    t, d = x3.shape
    tm = COMBINE_BLOCK
    nc = d // LANES
    grid_spec = pltpu.PrefetchScalarGridSpec(
        num_scalar_prefetch=2,
        grid=(t // tm,),
        in_specs=[
            pl.BlockSpec((tm, d), lambda i, s0, s1: (i, 0)),
            pl.BlockSpec((SUBLANES, tm), lambda i, s0, s1: (0, i)),
            pl.BlockSpec(memory_space=pl.ANY),
        ],
        out_specs=pl.BlockSpec((tm, d), lambda i, s0, s1: (i, 0)),
        scratch_shapes=[
            pltpu.VMEM((tm * nc, LANES), F32),
            pltpu.VMEM((tm * nc, LANES), F32),
            pltpu.VMEM((tm * nc, LANES), F32),
            pltpu.VMEM((tm * nc, LANES), F32),
            pltpu.SemaphoreType.DMA((2,)),
        ],
    )
    return pl.pallas_call(
        functools.partial(_combine_kernel, tm=tm, d=d),
        grid_spec=grid_spec,
        out_shape=jax.ShapeDtypeStruct((t, d), F32),
        compiler_params=_params("arbitrary"),
        name="moe_combine",
    )(slot0, slot1, x3, gates, ys2d)


def _routing_tables(route, counts, n_experts, max_tiles):
    tile = GROUP_TILE
    padded = ((counts + tile - 1) // tile) * tile
    ends = jnp.cumsum(padded)
    offs = ends - padded
    expert, rank = route[0:TOP_K], route[TOP_K:2 * TOP_K]
    base = jnp.zeros_like(expert)
    for e in range(n_experts):
        base = jnp.where(expert == e, offs[e], base)
    slots = base + rank
    tile_ends = ends // tile
    n_tiles = tile_ends[-1]
    j = jnp.arange(max_tiles, dtype=jnp.int32)
    tile_expert = jnp.sum((jnp.minimum(j, n_tiles - 1)[:, None] >= tile_ends[None, :]).astype(jnp.int32), axis=-1)
    row_end = jnp.sum(jnp.where(tile_expert[:, None] == jnp.arange(n_experts, dtype=jnp.int32)[None, :],
                                (offs + counts)[None, :], 0), axis=-1)
    tile_valid = jnp.where(j < n_tiles, jnp.clip(row_end - j * tile, 0, tile), 0)
    pad_start = jnp.concatenate([offs + counts, ends[-1:]]).astype(jnp.int32)
    pad_len = jnp.concatenate([padded - counts, max_tiles - n_tiles.reshape(1)]).astype(jnp.int32)
    return (slots[0], slots[1], pad_start, pad_len,
            tile_expert.astype(jnp.int32), n_tiles.reshape(1).astype(jnp.int32), tile_valid.astype(jnp.int32))


def _lambda_init(layer_idx_1based):
    return 0.8 - 0.6 * math.exp(-0.3 * (layer_idx_1based - 1))


def kernel(x, ln_mix, ln_ffn, conv_w_in, conv_w, conv_w_out, ln_kv, w_kv, k_norm, attn_w_q, q_norm, lam_params,
           sub_norm, attn_w_o, ffn_w_gu, ffn_w_down, router_w, moe_w_gu, moe_w_down):
    batch, seq, d = x.shape
    t = batch * seq
    n_experts = router_w.shape[-1]
    assert ln_mix.shape[0] == 2 and conv_w_in.shape[0] == 1 and attn_w_q.shape[0] == 1
    assert seq % ROW_TILE == 0 and seq % ATTN_TILE == 0 and d % LANES == 0
    nc = d // LANES
    x2d = x.reshape(t, d)

    x1 = _mixer_a(x2d, ln_mix[0:1], conv_w_in[0], conv_w[0], conv_w_out[0], seq)
    x2 = _ffn_dense(x1, ln_ffn[0:1], ffn_w_gu[0], ffn_w_down[0])

    n_k = d
    w_kt = w_kv[:, :n_k].T
    w_v = w_kv[:, n_k:]
    q, kt, v = _qkv(x2, ln_mix[1:2], ln_kv.reshape(1, d), attn_w_q[0], w_kt, w_v,
                    k_norm.reshape(HEAD_DIM, 1))

    score_bound = (math.sqrt(HEAD_DIM) * jnp.max(jnp.abs(q_norm[0])) * jnp.max(jnp.abs(k_norm))).reshape(1)
    o, moe_gu_bf, moe_down_bf = _attention(score_bound, q, kt, v, jnp.tile(q_norm[0:1], (1, 2)), lam_params[0],
                                           sub_norm[0:1], moe_w_gu[0], moe_w_down[0], batch, seq, _lambda_init(2))

    assert n_experts <= SUBLANES
    rw_t = jnp.pad(router_w[0].T, ((0, SUBLANES - n_experts), (0, 0)))
    x3, h8, gates, route, counts = _oproj_router(x2, o, attn_w_o[0], ln_ffn[1:2], rw_t, n_experts)
    max_tiles = (TOP_K * t) // GROUP_TILE + n_experts
    n_slots = max_tiles * GROUP_TILE
    slot0, slot1, pad_start, pad_len, tile_expert, n_tiles, tile_valid = _routing_tables(
        route, counts[:n_experts, 0], n_experts, max_tiles)
    hs = _dispatch(slot0, slot1, pad_start, pad_len, h8, nc, n_slots, n_experts)
    ys = _moe_group(tile_expert, n_tiles, tile_valid, hs, moe_gu_bf, moe_down_bf, max_tiles)
    out = _combine(slot0, slot1, x3, gates, ys)
    return out.reshape(batch, seq, d)
```

```python
import functools
import math

import jax
import jax.numpy as jnp
from jax import lax
from jax.experimental import pallas as pl
from jax.experimental.pallas import tpu as pltpu

F32 = jnp.float32
BF16 = jnp.bfloat16

EPS = 1e-6
HEAD_DIM = 64
V_DIM = 2 * HEAD_DIM
CONV_WIDTH = 3
TOP_K = 2

LANES = 128
SUBLANES = 8
VMEM_LIMIT = 56 * 1024 * 1024

ROW_TILE = 512
PROJ_TILE = 1024
ATTN_TILE = 256
ATTN_HEADS_PER_STEP = 2
GROUP_TILE = 512
MOE_F_CHUNK = 1792
MOE_VMEM_LIMIT = 62 * 1024 * 1024
DISPATCH_BLOCK = 2048
COMBINE_BLOCK = 256
DMA_UNROLL = 16

LOG2E = 1.4426950408889634
SAFE_SHIFT = 40.0


def _rms_scale(x):
    return x * lax.rsqrt(jnp.mean(x * x, axis=-1, keepdims=True) + EPS)


def _dot(a, b):
    return jnp.dot(a, b, preferred_element_type=F32)


def _resident(shape):
    return pl.BlockSpec(shape, lambda *_: (0,) * len(shape), pipeline_mode=pl.Buffered(1))


def _params(*sem):
    return pltpu.CompilerParams(dimension_semantics=sem, vmem_limit_bytes=VMEM_LIMIT)


def _mixer_a_kernel(x_ref, g_ref, win_ref, cw_ref, wout_ref, o_ref, ubuf, *, tm, d, tiles_per_seq):
    i = pl.program_id(0)
    x = x_ref[...]
    h = (_rms_scale(x) * g_ref[...]).astype(BF16)
    c = _dot(h, win_ref[:, d:2 * d].astype(BF16))
    v = _dot(h, win_ref[:, 2 * d:].astype(BF16))
    u = c * v

    @pl.when(i % tiles_per_seq == 0)
    def _():
        ubuf[0:SUBLANES, :] = jnp.zeros((SUBLANES, d), F32)

    ubuf[SUBLANES:tm + SUBLANES, :] = u
    u1 = ubuf[SUBLANES - 1:tm + SUBLANES - 1, :]
    u2 = ubuf[SUBLANES - 2:tm + SUBLANES - 2, :]
    cw = cw_ref[...]
    z = u2 * cw[0:1] + u1 * cw[1:2] + u * cw[2:3]
    b = _dot(h, win_ref[:, 0:d].astype(BF16))
    y = (b * z).astype(BF16)
    o_ref[...] = x + _dot(y, wout_ref[...].astype(BF16))
    ubuf[0:SUBLANES, :] = ubuf[tm:tm + SUBLANES, :]


def _mixer_a(x2d, g, w_in, conv_w, w_out, seq):
    t, d = x2d.shape
    tm = ROW_TILE
    kern = functools.partial(_mixer_a_kernel, tm=tm, d=d, tiles_per_seq=seq // tm)
    return pl.pallas_call(
        kern,
        grid=(t // tm,),
        in_specs=[
            pl.BlockSpec((tm, d), lambda i: (i, 0)),
            pl.BlockSpec((1, d), lambda i: (0, 0)),
            _resident((d, 3 * d)),
            pl.BlockSpec((CONV_WIDTH, d), lambda i: (0, 0)),
            _resident((d, d)),
        ],
        out_specs=pl.BlockSpec((tm, d), lambda i: (i, 0)),
        out_shape=jax.ShapeDtypeStruct((t, d), F32),
        scratch_shapes=[pltpu.VMEM((tm + SUBLANES, d), F32)],
        compiler_params=_params("arbitrary"),
        name="mixer_a",
    )(x2d, g, w_in, conv_w, w_out)


def _ffn_chunks(f):
    step = 768
    return [(lo, min(lo + step, f)) for lo in range(0, f, step)]


def _ffn_kernel(x_ref, g_ref, wgu_ref, wd_ref, o_ref, *, f):
    x = x_ref[...]
    h = (_rms_scale(x) * g_ref[...]).astype(BF16)
    acc = x
    for lo, hi in _ffn_chunks(f):
        gate = _dot(h, wgu_ref[:, lo:hi].astype(BF16))
        up = _dot(h, wgu_ref[:, f + lo:f + hi].astype(BF16))
        a = (gate * jax.nn.sigmoid(gate) * up).astype(BF16)
        acc = acc + _dot(a, wd_ref[lo:hi, :].astype(BF16))
    o_ref[...] = acc


def _ffn_dense(x2d, g, w_gu, w_down):
    t, d = x2d.shape
    f = w_down.shape[0]
    tm = ROW_TILE
    return pl.pallas_call(
        functools.partial(_ffn_kernel, f=f),
        grid=(t // tm,),
        in_specs=[
            pl.BlockSpec((tm, d), lambda i: (i, 0)),
            pl.BlockSpec((1, d), lambda i: (0, 0)),
            _resident((d, 2 * f)),
            _resident((f, d)),
        ],
        out_specs=pl.BlockSpec((tm, d), lambda i: (i, 0)),
        out_shape=jax.ShapeDtypeStruct((t, d), F32),
        compiler_params=_params("parallel"),
        name="ffn_dense",
    )(x2d, g, w_gu, w_down)


def _qkv_kernel(x_ref, gq_ref, gkv_ref, wq_ref, wkv_ref, kn_ref, q_ref, kt_ref, v_ref, wkt_s, *, tm, d):
    @pl.when(pl.program_id(0) == 0)
    def _():
        wkt_s[...] = jnp.transpose(wkv_ref[:, 0:d]).astype(BF16)

    y = _rms_scale(x_ref[...])
    hq = (y * gq_ref[...]).astype(BF16)
    hkv = (y * gkv_ref[...]).astype(BF16)
    q_ref[...] = _dot(hq, wq_ref[...].astype(BF16)).astype(BF16)
    v_ref[...] = _dot(hkv, wkv_ref[:, d:].astype(BF16)).astype(BF16)
    kt = lax.dot_general(wkt_s[...], hkv, (((1,), (1,)), ((), ())), preferred_element_type=F32)
    k3 = kt.reshape(d // HEAD_DIM, HEAD_DIM, tm)
    k3 = k3 * lax.rsqrt(jnp.mean(k3 * k3, axis=1, keepdims=True) + EPS) * kn_ref[...][None]
    kt_ref[...] = k3.reshape(d, tm).astype(BF16)


def _qkv(x2d, g_q, g_kv, w_q, w_kv, k_norm_col):
    t, d = x2d.shape
    tm = PROJ_TILE
    const = lambda i: (0, 0)
    return pl.pallas_call(
        functools.partial(_qkv_kernel, tm=tm, d=d),
        grid=(t // tm,),
        in_specs=[
            pl.BlockSpec((tm, d), lambda i: (i, 0)),
            pl.BlockSpec((1, d), const),
            pl.BlockSpec((1, d), const),
            _resident((d, d)),
            _resident((d, 2 * d)),
            pl.BlockSpec((HEAD_DIM, 1), const),
        ],
        out_specs=[
            pl.BlockSpec((tm, d), lambda i: (i, 0)),
            pl.BlockSpec((d, tm), lambda i: (0, i)),
            pl.BlockSpec((tm, d), lambda i: (i, 0)),
        ],
        out_shape=[
            jax.ShapeDtypeStruct((t, d), BF16),
            jax.ShapeDtypeStruct((d, t), BF16),
            jax.ShapeDtypeStruct((t, d), BF16),
        ],
        scratch_shapes=[pltpu.VMEM((d, d), BF16)],
        compiler_params=_params("arbitrary"),
        name="qkv_proj",
    )(x2d, g_q, g_kv, w_q, w_kv, k_norm_col)


def _group_mean_sq(x, ones_bd, group):
    return _dot((x * x).astype(BF16), ones_bd) * (1.0 / group)


def _attn_kernel(bound_ref, q_ref, kt_ref, v_ref, qn_ref, lam_ref, sn_ref, wgu_ref, wdn_ref,
                 o_ref, wgu_bf_ref, wdn_bf_ref, v1_ref, qn_s, *, seq, tq, heads, lam_init):
    wgu_bf_ref[...] = wgu_ref[...].astype(BF16)
    wdn_bf_ref[...] = wdn_ref[...].astype(BF16)
    lp = lam_ref[...]
    lam = (jnp.exp(jnp.sum(lp[0:1] * lp[1:2], axis=-1, keepdims=True))
           - jnp.exp(jnp.sum(lp[2:3] * lp[3:4], axis=-1, keepdims=True)) + lam_init)
    row = lax.broadcasted_iota(jnp.int32, (tq, tq), 0)
    col = lax.broadcasted_iota(jnp.int32, (tq, tq), 1)
    causal = col <= row
    gi = lax.broadcasted_iota(jnp.int32, (V_DIM, V_DIM), 0) // HEAD_DIM
    gj = lax.broadcasted_iota(jnp.int32, (V_DIM, V_DIM), 1) // HEAD_DIM
    ones_bd = jnp.where(gi == gj, 1.0, 0.0).astype(BF16)
    qgain = qn_ref[...] * (HEAD_DIM ** -0.5 * LOG2E)
    maps = [slice(c * HEAD_DIM, (c + 1) * HEAD_DIM) for c in range(2)]
    n_q = seq // tq

    v1_ref[:, V_DIM:] = jnp.ones((seq, V_DIM), BF16)
    for hh in range(heads):
        hcols = slice(hh * V_DIM, (hh + 1) * V_DIM)
        hrow = hh * V_DIM
        v1_ref[:, 0:V_DIM] = v_ref[:, hcols]
        q = q_ref[:, hcols].astype(F32)
        ms = _group_mean_sq(q, ones_bd, HEAD_DIM)
        qn_s[...] = (q * lax.rsqrt(ms + EPS) * qgain).astype(BF16)

        def finish(q0, r1, r2, hcols=hcols):
            o = r1[:, 0:V_DIM] / r1[:, V_DIM:] - lam * (r2[:, 0:V_DIM] / r2[:, V_DIM:])
            o = _rms_scale(o) * sn_ref[...] * (1.0 - lam_init)
            o_ref[pl.ds(q0, tq), hcols] = o.astype(BF16)

        def kt_rows(sl, hrow=hrow):
            return slice(hrow + sl.start, hrow + sl.stop)

        @pl.when(bound_ref[0] <= SAFE_SHIFT)
        def _fixed_shift(finish=finish, kt_rows=kt_rows):
            shift = bound_ref[0] * LOG2E

            def scores(qi):
                q0 = qi * tq
                out = []
                for sl in maps:
                    qc = qn_s[q0:q0 + tq, sl]
                    s_diag = _dot(qc, kt_ref[kt_rows(sl), q0:q0 + tq])
                    s_low = _dot(qc, kt_ref[kt_rows(sl), 0:q0]) if q0 > 0 else None
                    out.append((s_diag, s_low))
                return out

            order = list(reversed(range(n_q)))
            pending = scores(order[0])
            for idx, qi in enumerate(order):
                q0 = qi * tq
                cur = pending
                if idx + 1 < n_q:
                    pending = scores(order[idx + 1])
                res = []
                for s_diag, s_low in cur:
                    p = jnp.where(causal, jnp.exp2(s_diag - shift), 0.0).astype(BF16)
                    r = _dot(p, v1_ref[q0:q0 + tq, :])
                    if s_low is not None:
                        r = r + _dot(jnp.exp2(s_low - shift).astype(BF16), v1_ref[0:q0, :])
                    res.append(r)
                finish(q0, *res)

        @pl.when(jnp.logical_not(bound_ref[0] <= SAFE_SHIFT))
        def _running_max(finish=finish, kt_rows=kt_rows):
            def q_body(qi, carry):
                q0 = pl.multiple_of(qi * tq, tq)
                qs = [qn_s[pl.ds(q0, tq), sl] for sl in maps]

                def tile(j, state, masked):
                    k0 = pl.multiple_of(j * tq, tq)
                    out = []
                    for c, sl in enumerate(maps):
                        m, r = state[c]
                        s = _dot(qs[c], kt_ref[kt_rows(sl), pl.ds(k0, tq)])
                        if masked:
                            s = jnp.where(causal, s, -jnp.inf)
                        m_new = jnp.maximum(m, jnp.max(s, axis=-1, keepdims=True))
                        p = jnp.exp2(s - m_new).astype(BF16)
                        r = jnp.exp2(m - m_new) * r + _dot(p, v1_ref[pl.ds(k0, tq), :])
                        out.append((m_new, r))
                    return tuple(out)

                one = (jnp.full((tq, 1), -jnp.inf, F32), jnp.zeros((tq, 2 * V_DIM), F32))
                state = lax.fori_loop(0, qi, lambda j, st: tile(j, st, False), (one, one))
                (_, r1), (_, r2) = tile(qi, state, True)
                finish(q0, r1, r2)
                return carry

            lax.fori_loop(0, n_q, q_body, 0)


def _attention(score_bound, q, kt, v, q_norm2, lam_params, sub_norm, w_gu, w_down, batch, seq, lam_init):
    t, d = q.shape
    n_heads = d // V_DIM
    hps = ATTN_HEADS_PER_STEP
    assert n_heads % hps == 0
    hsteps = n_heads // hps
    steps = batch * hsteps
    wgu2 = w_gu.reshape(-1, w_gu.shape[-1])
    wdn2 = w_down.reshape(-1, w_down.shape[-1])
    gu_rows, dn_rows = wgu2.shape[0] // steps, wdn2.shape[0] // steps
    assert wgu2.shape[0] % steps == 0 and wdn2.shape[0] % steps == 0 and gu_rows % 16 == 0 and dn_rows % 16 == 0
    slab = lambda b, h: (b * hsteps + h, 0)
    hw = hps * V_DIM
    kern = functools.partial(_attn_kernel, seq=seq, tq=ATTN_TILE, heads=hps, lam_init=lam_init)
    o, wgu_bf, wdn_bf = pl.pallas_call(
        kern,
        grid=(batch, hsteps),
        in_specs=[
            pl.BlockSpec(memory_space=pltpu.SMEM),
            pl.BlockSpec((seq, hw), lambda b, h: (b, h)),
            pl.BlockSpec((hw, seq), lambda b, h: (h, b)),
            pl.BlockSpec((seq, hw), lambda b, h: (b, h)),
            pl.BlockSpec((1, V_DIM), lambda b, h: (0, 0)),
            pl.BlockSpec((4, HEAD_DIM), lambda b, h: (0, 0)),
            pl.BlockSpec((1, V_DIM), lambda b, h: (0, 0)),
            pl.BlockSpec((gu_rows, wgu2.shape[1]), slab),
            pl.BlockSpec((dn_rows, wdn2.shape[1]), slab),
        ],
        out_specs=[
            pl.BlockSpec((seq, hw), lambda b, h: (b, h)),
            pl.BlockSpec((gu_rows, wgu2.shape[1]), slab),
            pl.BlockSpec((dn_rows, wdn2.shape[1]), slab),
        ],
        out_shape=[
            jax.ShapeDtypeStruct((t, d), BF16),
            jax.ShapeDtypeStruct(wgu2.shape, BF16),
            jax.ShapeDtypeStruct(wdn2.shape, BF16),
        ],
        scratch_shapes=[pltpu.VMEM((seq, 2 * V_DIM), BF16), pltpu.VMEM((seq, V_DIM), BF16)],
        compiler_params=_params("parallel", "parallel"),
        name="diff_attn",
    )(score_bound, q, kt, v, q_norm2, lam_params, sub_norm, wgu2, wdn2)
    return o, wgu_bf.reshape(w_gu.shape), wdn_bf.reshape(w_down.shape)


def _nt_dot(a, b):
    return lax.dot_general(a, b, (((1,), (1,)), ((), ())), preferred_element_type=F32)


def _oproj_router_kernel(x_ref, o_ref, wo_ref, g_ref, rwt_ref,
                         x3_ref, h8_ref, gate_ref, route_ref, cnt_ref, carry,
                         *, tm, d, n_experts):
    i = pl.program_id(0)

    @pl.when(i == 0)
    def _():
        carry[...] = jnp.zeros_like(carry)

    x3 = x_ref[...] + _dot(o_ref[...], wo_ref[...].astype(BF16))
    x3_ref[...] = x3
    h = _rms_scale(x3) * g_ref[...]
    for c in range(d // LANES):
        h8_ref[pl.ds(c, tm, stride=d // LANES), :] = h[:, c * LANES:(c + 1) * LANES]

    rwt = rwt_ref[...]
    h_hi = h.astype(BF16)
    h_lo = (h - h_hi.astype(F32)).astype(BF16)
    w_hi = rwt.astype(BF16)
    w_lo = (rwt - w_hi.astype(F32)).astype(BF16)
    logits = _nt_dot(w_hi, h_hi) + (_nt_dot(w_lo, h_hi) + _nt_dot(w_hi, h_lo))

    rows = rwt.shape[0]
    sub = lax.broadcasted_iota(jnp.int32, (rows, tm), 0)
    lane = lax.broadcasted_iota(jnp.int32, (rows, tm), 1)
    neg = -jnp.inf
    l1 = jnp.where(sub < n_experts, logits, neg)
    m1 = jnp.max(l1, axis=0, keepdims=True)
    i1 = jnp.min(jnp.where(l1 == m1, sub, rows), axis=0, keepdims=True)
    l2 = jnp.where(sub == i1, neg, l1)
    m2 = jnp.max(l2, axis=0, keepdims=True)
    i2 = jnp.min(jnp.where(l2 == m2, sub, rows), axis=0, keepdims=True)
    e2 = jnp.exp(m2 - m1)
    den = 1.0 + e2
    gate_ref[...] = jnp.where(sub == 0, 1.0 / den, jnp.where(sub == 1, e2 / den, 0.0))

    oh1 = sub == i1
    oh2 = sub == i2
    cnt = jnp.where(oh1, 1.0, 0.0) + jnp.where(oh2, 1.0, 0.0)
    incl = cnt
    shift = 1
    while shift < tm:
        incl = incl + jnp.where(lane >= shift, pltpu.roll(incl, shift, 1), 0.0)
        shift *= 2
    before = incl - cnt + carry[:, 0:1]
    r1 = jnp.sum(jnp.where(oh1, before, 0.0), axis=0, keepdims=True)
    r2 = jnp.sum(jnp.where(oh2, before, 0.0), axis=0, keepdims=True)
    route = jnp.where(sub == 0, i1.astype(F32), jnp.where(sub == 1, i2.astype(F32),
                      jnp.where(sub == 2, r1, jnp.where(sub == 3, r2, 0.0))))
    route_ref[...] = route.astype(jnp.int32)
    total = carry[...] + jnp.sum(cnt, axis=1, keepdims=True)
    carry[...] = total
    cnt_ref[...] = total.astype(jnp.int32)


def _oproj_router(x2d, o, w_o, g, rw_t, n_experts):
    t, d = x2d.shape
    tm = PROJ_TILE
    const = lambda i: (0, 0)
    kern = functools.partial(_oproj_router_kernel, tm=tm, d=d, n_experts=n_experts)
    return pl.pallas_call(
        kern,
        grid=(t // tm,),
        in_specs=[
            pl.BlockSpec((tm, d), lambda i: (i, 0)),
            pl.BlockSpec((tm, d), lambda i: (i, 0)),
            _resident((d, d)),
            pl.BlockSpec((1, d), const),
            pl.BlockSpec((SUBLANES, d), const),
        ],
        out_specs=[
            pl.BlockSpec((tm, d), lambda i: (i, 0)),
            pl.BlockSpec((tm * (d // LANES), LANES), lambda i: (i, 0)),
            pl.BlockSpec((SUBLANES, tm), lambda i: (0, i)),
            pl.BlockSpec((SUBLANES, tm), lambda i: (0, i)),
            pl.BlockSpec((SUBLANES, LANES), const),
        ],
        out_shape=[
            jax.ShapeDtypeStruct((t, d), F32),
            jax.ShapeDtypeStruct((t * (d // LANES), LANES), F32),
            jax.ShapeDtypeStruct((SUBLANES, t), F32),
            jax.ShapeDtypeStruct((SUBLANES, t), jnp.int32),
            jax.ShapeDtypeStruct((SUBLANES, LANES), jnp.int32),
        ],
        scratch_shapes=[pltpu.VMEM((SUBLANES, LANES), F32)],
        compiler_params=_params("arbitrary"),
        name="oproj_router",
    )(x2d, o, w_o, g, rw_t)


def _pad_bits(tile):
    return [1 << b for b in reversed(range(int(math.log2(tile))))]


def _dispatch_kernel(s0_ref, s1_ref, pad_start_ref, pad_len_ref, h_ref, hs_hbm, zeros, sem, zsem,
                     *, tb, nc, n_experts, tile):
    i = pl.program_id(0)
    base = i * tb

    def rows(ref, first, count):
        start = first * nc
        if not isinstance(start, int):
            start = pl.multiple_of(start, nc)
        return ref.at[pl.ds(start, count * nc), :]

    def row_copy(r, slot):
        return pltpu.make_async_copy(rows(h_ref, r, 1), rows(hs_hbm, slot, 1), sem)

    def issue(g, carry):
        for u in range(DMA_UNROLL):
            r = g * DMA_UNROLL + u
            row_copy(r, s0_ref[base + r]).start(priority=0)
            row_copy(r, s1_ref[base + r]).start(priority=1)
        return carry

    lax.fori_loop(0, tb // DMA_UNROLL, issue, 0)

    def pad_copies(do):
        for e in range(n_experts):
            n = pad_len_ref[e]
            pos = pad_start_ref[e]
            for bit in _pad_bits(tile):
                @pl.when((n & bit) != 0)
                def _(pos=pos, bit=bit):
                    do(pltpu.make_async_copy(rows(zeros, 0, bit), rows(hs_hbm, pos, bit), zsem))
                pos = pos + (n & bit)
        for k in range(n_experts):
            @pl.when(k < pad_len_ref[n_experts])
            def _(k=k):
                pos = pad_start_ref[n_experts] + k * tile
                do(pltpu.make_async_copy(zeros, rows(hs_hbm, pos, tile), zsem))

    @pl.when(i == 0)
    def _():
        zeros[...] = jnp.zeros_like(zeros)
        pad_copies(lambda cp: cp.start())
        pad_copies(lambda cp: cp.wait())

    def drain(g, carry):
        for u in range(2 * DMA_UNROLL):
            row_copy(0, 0).wait()
        return carry

    lax.fori_loop(0, tb // DMA_UNROLL, drain, 0)


def _dispatch(slot0, slot1, pad_start, pad_len, h8, nc, n_slots, n_experts):
    t = h8.shape[0] // nc
    tb = DISPATCH_BLOCK
    kern = functools.partial(_dispatch_kernel, tb=tb, nc=nc, n_experts=n_experts, tile=GROUP_TILE)
    grid_spec = pltpu.PrefetchScalarGridSpec(
        num_scalar_prefetch=4,
        grid=(t // tb,),
        in_specs=[pl.BlockSpec((tb * nc, LANES), lambda i, *_: (i, 0))],
        out_specs=pl.BlockSpec(memory_space=pl.ANY),
        scratch_shapes=[
            pltpu.VMEM((GROUP_TILE * nc, LANES), F32),
            pltpu.SemaphoreType.DMA,
            pltpu.SemaphoreType.DMA,
        ],
    )
    return pl.pallas_call(
        kern,
        grid_spec=grid_spec,
        out_shape=jax.ShapeDtypeStruct((n_slots * nc, LANES), F32),
        compiler_params=_params("arbitrary"),
        name="moe_dispatch",
    )(slot0, slot1, pad_start, pad_len, h8)


def _moe_group_kernel(te_ref, nt_ref, valid_ref, hs_ref, wg_ref, wu_ref, wd_ref, ys_ref, *, tm, d, f_exp):
    j = pl.program_id(0)
    nc = d // LANES
    valid = valid_ref[j]

    def expert_ffn(rows):
        h = jnp.concatenate([hs_ref[pl.ds(c, rows, stride=nc), :].astype(BF16) for c in range(nc)], axis=1)
        y = None
        for lo in range(0, f_exp, MOE_F_CHUNK):
            hi = min(lo + MOE_F_CHUNK, f_exp)
            gate = _dot(h, wg_ref[:, lo:hi])
            up = _dot(h, wu_ref[:, lo:hi])
            a = (gate * jax.nn.sigmoid(gate) * up).astype(BF16)
            p = _dot(a, wd_ref[lo:hi, :])
            y = p if y is None else y + p
        for c in range(nc):
            ys_ref[pl.ds(c, rows, stride=nc), :] = y[:, c * LANES:(c + 1) * LANES]

    half = tm // 2

    @pl.when(valid > half)
    def _():
        expert_ffn(tm)

    @pl.when(jnp.logical_and(valid > 0, valid <= half))
    def _():
        expert_ffn(half)
        ys_ref[half * nc:, :] = jnp.zeros(((tm - half) * nc, LANES), F32)

    @pl.when(valid == 0)
    def _():
        ys_ref[...] = jnp.zeros_like(ys_ref)


def _moe_group(tile_expert, n_tiles, tile_valid, hs2d, w_gu, w_down, max_tiles):
    n_e, d, f2 = w_gu.shape
    f_exp = f2 // 2
    tm = GROUP_TILE
    nc = d // LANES

    def row_map(j, te, nt, valid):
        return (jnp.minimum(j, nt[0] - 1), 0)

    kern = functools.partial(_moe_group_kernel, tm=tm, d=d, f_exp=f_exp)
    grid_spec = pltpu.PrefetchScalarGridSpec(
        num_scalar_prefetch=3,
        grid=(max_tiles,),
        in_specs=[
            pl.BlockSpec((tm * nc, LANES), row_map),
            pl.BlockSpec((None, d, f_exp), lambda j, te, nt, valid: (te[j], 0, 0)),
            pl.BlockSpec((None, d, f_exp), lambda j, te, nt, valid: (te[j], 0, 1)),
            pl.BlockSpec((None, f_exp, d), lambda j, te, nt, valid: (te[j], 0, 0)),
        ],
        out_specs=pl.BlockSpec((tm * nc, LANES), lambda j, te, nt, valid: (j, 0)),
    )
    return pl.pallas_call(
        kern,
        grid_spec=grid_spec,
        out_shape=jax.ShapeDtypeStruct(hs2d.shape, F32),
        compiler_params=pltpu.CompilerParams(dimension_semantics=("arbitrary",),
                                             vmem_limit_bytes=MOE_VMEM_LIMIT),
        name="moe_group",
    )(tile_expert, n_tiles, tile_valid, hs2d, w_gu, w_gu, w_down)


def _combine_kernel(s0_ref, s1_ref, x_ref, gate_ref, ys_hbm, o_ref, b00, b01, b10, b11, sems, *, tm, d):
    i = pl.program_id(0)
    n = pl.num_programs(0)
    nc = d // LANES
    bufs = ((b00, b01), (b10, b11))

    def row_copy(slot, par, k, r):
        start = r * nc
        if not isinstance(start, int):
            start = pl.multiple_of(start, nc)
        src = ys_hbm.at[pl.ds(pl.multiple_of(slot * nc, nc), nc), :]
        return pltpu.make_async_copy(src, bufs[par][k].at[pl.ds(start, nc), :], sems.at[par])

    def issue(block, par):
        base = block * tm

        def body(g, carry):
            for u in range(DMA_UNROLL):
                r = g * DMA_UNROLL + u
                row_copy(s0_ref[base + r], par, 0, r).start(priority=0)
                row_copy(s1_ref[base + r], par, 1, r).start(priority=1)
            return carry

        lax.fori_loop(0, tm // DMA_UNROLL, body, 0)

    def drain(par):
        def body(g, carry):
            for u in range(DMA_UNROLL):
                row_copy(0, par, 0, 0).wait()
                row_copy(0, par, 1, 0).wait()
            return carry

        lax.fori_loop(0, tm // DMA_UNROLL, body, 0)

    @pl.when(i == 0)
    def _():
        issue(0, 0)

    for par in range(2):
        @pl.when(i % 2 == par)
        def _(par=par):
            @pl.when(i + 1 < n)
            def _():
                issue(i + 1, 1 - par)

            drain(par)
            gates = jnp.transpose(gate_ref[...])
            g0 = gates[:, 0:1]
            g1 = gates[:, 1:2]
            for c in range(nc):
                y0 = bufs[par][0][pl.ds(c, tm, stride=nc), :]
                y1 = bufs[par][1][pl.ds(c, tm, stride=nc), :]
                cols = slice(c * LANES, (c + 1) * LANES)
                o_ref[:, cols] = x_ref[:, cols] + (g0 * y0 + g1 * y1)


def _combine(slot0, slot1, x3, gates, ys2d):
    t, d = x3.shape
    tm = COMBINE_BLOCK
    nc = d // LANES
    grid_spec = pltpu.PrefetchScalarGridSpec(
        num_scalar_prefetch=2,
        grid=(t // tm,),
        in_specs=[
            pl.BlockSpec((tm, d), lambda i, s0, s1: (i, 0)),
            pl.BlockSpec((SUBLANES, tm), lambda i, s0, s1: (0, i)),
            pl.BlockSpec(memory_space=pl.ANY),
        ],
        out_specs=pl.BlockSpec((tm, d), lambda i, s0, s1: (i, 0)),
        scratch_shapes=[
            pltpu.VMEM((tm * nc, LANES), F32),
            pltpu.VMEM((tm * nc, LANES), F32),
            pltpu.VMEM((tm * nc, LANES), F32),
            pltpu.VMEM((tm * nc, LANES), F32),
            pltpu.SemaphoreType.DMA((2,)),
        ],
    )
    return pl.pallas_call(
        functools.partial(_combine_kernel, tm=tm, d=d),
        grid_spec=grid_spec,
        out_shape=jax.ShapeDtypeStruct((t, d), F32),
        compiler_params=_params("arbitrary"),
        name="moe_combine",
    )(slot0, slot1, x3, gates, ys2d)


def _routing_tables(route, counts, n_experts, max_tiles):
    tile = GROUP_TILE
    padded = ((counts + tile - 1) // tile) * tile
    ends = jnp.cumsum(padded)
    offs = ends - padded
    slots = []
    for k in range(TOP_K):
        expert, rank = route[k], route[TOP_K + k]
        base = jnp.zeros_like(expert)
        for e in range(n_experts):
            base = jnp.where(expert == e, offs[e], base)
        slots.append(base + rank)
    tile_ends = ends // tile
    n_tiles = tile_ends[-1]
    j = jnp.arange(max_tiles, dtype=jnp.int32)
    tile_expert = jnp.sum((jnp.minimum(j, n_tiles - 1)[:, None] >= tile_ends[None, :]).astype(jnp.int32), axis=-1)
    row_end = jnp.sum(jnp.where(tile_expert[:, None] == jnp.arange(n_experts, dtype=jnp.int32)[None, :],
                                (offs + counts)[None, :], 0), axis=-1)
    tile_valid = jnp.where(j < n_tiles, jnp.clip(row_end - j * tile, 0, tile), 0)
    pad_start = jnp.concatenate([offs + counts, ends[-1:]]).astype(jnp.int32)
    pad_len = jnp.concatenate([padded - counts, max_tiles - n_tiles.reshape(1)]).astype(jnp.int32)
    return (slots[0], slots[1], pad_start, pad_len,
            tile_expert.astype(jnp.int32), n_tiles.reshape(1).astype(jnp.int32), tile_valid.astype(jnp.int32))


def _lambda_init(layer_idx_1based):
    return 0.8 - 0.6 * math.exp(-0.3 * (layer_idx_1based - 1))


def kernel(x, ln_mix, ln_ffn, conv_w_in, conv_w, conv_w_out, ln_kv, w_kv, k_norm, attn_w_q, q_norm, lam_params,
           sub_norm, attn_w_o, ffn_w_gu, ffn_w_down, router_w, moe_w_gu, moe_w_down):
    batch, seq, d = x.shape
    t = batch * seq
    n_experts = router_w.shape[-1]
    assert ln_mix.shape[0] == 2 and conv_w_in.shape[0] == 1 and attn_w_q.shape[0] == 1
    assert seq % ROW_TILE == 0 and seq % ATTN_TILE == 0 and d % LANES == 0
    nc = d // LANES
    x2d = x.reshape(t, d)

    x1 = _mixer_a(x2d, ln_mix[0:1], conv_w_in[0], conv_w[0], conv_w_out[0], seq)
    x2 = _ffn_dense(x1, ln_ffn[0:1], ffn_w_gu[0], ffn_w_down[0])

    assert w_kv.shape == (d, 2 * d)
    q, kt, v = _qkv(x2, ln_mix[1:2], ln_kv.reshape(1, d), attn_w_q[0], w_kv, k_norm.reshape(HEAD_DIM, 1))

    score_bound = (math.sqrt(HEAD_DIM) * jnp.max(jnp.abs(q_norm[0])) * jnp.max(jnp.abs(k_norm))).reshape(1)
    o, moe_gu_bf, moe_down_bf = _attention(score_bound, q, kt, v, jnp.tile(q_norm[0:1], (1, 2)), lam_params[0],
                                           sub_norm[0:1], moe_w_gu[0], moe_w_down[0], batch, seq, _lambda_init(2))

    assert n_experts <= SUBLANES
    rw_t = jnp.pad(router_w[0].T, ((0, SUBLANES - n_experts), (0, 0)))
    x3, h8, gates, route, counts = _oproj_router(x2, o, attn_w_o[0], ln_ffn[1:2], rw_t, n_experts)
    max_tiles = (TOP_K * t) // GROUP_TILE + n_experts
    n_slots = max_tiles * GROUP_TILE
    slot0, slot1, pad_start, pad_len, tile_expert, n_tiles, tile_valid = _routing_tables(
        route, counts[:n_experts, 0], n_experts, max_tiles)
    hs = _dispatch(slot0, slot1, pad_start, pad_len, h8, nc, n_slots, n_experts)
    ys = _moe_group(tile_expert, n_tiles, tile_valid, hs, moe_gu_bf, moe_down_bf, max_tiles)
    out = _combine(slot0, slot1, x3, gates, ys)
    return out.reshape(batch, seq, d)
```

```python
import functools
import math

import jax
import jax.numpy as jnp
from jax import lax
from jax.experimental import pallas as pl
from jax.experimental.pallas import tpu as pltpu

F32 = jnp.float32
BF16 = jnp.bfloat16

EPS = 1e-6
HEAD_DIM = 64
V_DIM = 2 * HEAD_DIM
CONV_WIDTH = 3
TOP_K = 2

LANES = 128
SUBLANES = 8
VMEM_LIMIT = 56 * 1024 * 1024

ROW_TILE = 512
PROJ_TILE = 1024
ATTN_TILE = 256
ATTN_HEADS_PER_STEP = 2
GROUP_TILE = 512
MOE_F_CHUNK = 1792
MOE_VMEM_LIMIT = 62 * 1024 * 1024
DISPATCH_BLOCK = 2048
COMBINE_BLOCK = 256
DMA_UNROLL = 16

LOG2E = 1.4426950408889634
SAFE_SHIFT = 40.0


def _rms_scale(x):
    return x * lax.rsqrt(jnp.mean(x * x, axis=-1, keepdims=True) + EPS)


def _dot(a, b):
    return jnp.dot(a, b, preferred_element_type=F32)


def _resident(shape):
    return pl.BlockSpec(shape, lambda *_: (0,) * len(shape), pipeline_mode=pl.Buffered(1))


def _params(*sem):
    return pltpu.CompilerParams(dimension_semantics=sem, vmem_limit_bytes=VMEM_LIMIT)


def _mixer_a_kernel(x_ref, g_ref, win_ref, cw_ref, wout_ref, o_ref, ubuf, *, tm, d, tiles_per_seq):
    i = pl.program_id(0)
    x = x_ref[...]
    h = (_rms_scale(x) * g_ref[...]).astype(BF16)
    c = _dot(h, win_ref[:, d:2 * d].astype(BF16))
    v = _dot(h, win_ref[:, 2 * d:].astype(BF16))
    u = c * v

    @pl.when(i % tiles_per_seq == 0)
    def _():
        ubuf[0:SUBLANES, :] = jnp.zeros((SUBLANES, d), F32)

    ubuf[SUBLANES:tm + SUBLANES, :] = u
    u1 = ubuf[SUBLANES - 1:tm + SUBLANES - 1, :]
    u2 = ubuf[SUBLANES - 2:tm + SUBLANES - 2, :]
    cw = cw_ref[...]
    z = u2 * cw[0:1] + u1 * cw[1:2] + u * cw[2:3]
    b = _dot(h, win_ref[:, 0:d].astype(BF16))
    y = (b * z).astype(BF16)
    o_ref[...] = x + _dot(y, wout_ref[...].astype(BF16))
    ubuf[0:SUBLANES, :] = ubuf[tm:tm + SUBLANES, :]


def _mixer_a(x2d, g, w_in, conv_w, w_out, seq):
    t, d = x2d.shape
    tm = PROJ_TILE
    kern = functools.partial(_mixer_a_kernel, tm=tm, d=d, tiles_per_seq=seq // tm)
    return pl.pallas_call(
        kern,
        grid=(t // tm,),
        in_specs=[
            pl.BlockSpec((tm, d), lambda i: (i, 0)),
            pl.BlockSpec((1, d), lambda i: (0, 0)),
            _resident((d, 3 * d)),
            pl.BlockSpec((CONV_WIDTH, d), lambda i: (0, 0)),
            _resident((d, d)),
        ],
        out_specs=pl.BlockSpec((tm, d), lambda i: (i, 0)),
        out_shape=jax.ShapeDtypeStruct((t, d), F32),
        scratch_shapes=[pltpu.VMEM((tm + SUBLANES, d), F32)],
        compiler_params=_params("arbitrary"),
        name="mixer_a",
    )(x2d, g, w_in, conv_w, w_out)


def _ffn_chunks(f):
    step = 768
    return [(lo, min(lo + step, f)) for lo in range(0, f, step)]


def _ffn_kernel(x_ref, g_ref, wgu_ref, wd_ref, o_ref, *, f):
    x = x_ref[...]
    h = (_rms_scale(x) * g_ref[...]).astype(BF16)
    acc = x
    for lo, hi in _ffn_chunks(f):
        gate = _dot(h, wgu_ref[:, lo:hi].astype(BF16))
        up = _dot(h, wgu_ref[:, f + lo:f + hi].astype(BF16))
        a = (gate * jax.nn.sigmoid(gate) * up).astype(BF16)
        acc = acc + _dot(a, wd_ref[lo:hi, :].astype(BF16))
    o_ref[...] = acc


def _ffn_dense(x2d, g, w_gu, w_down):
    t, d = x2d.shape
    f = w_down.shape[0]
    tm = ROW_TILE
    return pl.pallas_call(
        functools.partial(_ffn_kernel, f=f),
        grid=(t // tm,),
        in_specs=[
            pl.BlockSpec((tm, d), lambda i: (i, 0)),
            pl.BlockSpec((1, d), lambda i: (0, 0)),
            _resident((d, 2 * f)),
            _resident((f, d)),
        ],
        out_specs=pl.BlockSpec((tm, d), lambda i: (i, 0)),
        out_shape=jax.ShapeDtypeStruct((t, d), F32),
        compiler_params=_params("parallel"),
        name="ffn_dense",
    )(x2d, g, w_gu, w_down)


def _qkv_kernel(x_ref, gq_ref, gkv_ref, wq_ref, wkv_ref, kn_ref, q_ref, kt_ref, v_ref, wkt_s, *, tm, d):
    @pl.when(pl.program_id(0) == 0)
    def _():
        wkt_s[...] = jnp.transpose(wkv_ref[:, 0:d]).astype(BF16)

    y = _rms_scale(x_ref[...])
    hq = (y * gq_ref[...]).astype(BF16)
    hkv = (y * gkv_ref[...]).astype(BF16)
    q_ref[...] = _dot(hq, wq_ref[...].astype(BF16)).astype(BF16)
    v_ref[...] = _dot(hkv, wkv_ref[:, d:].astype(BF16)).astype(BF16)
    kt = lax.dot_general(wkt_s[...], hkv, (((1,), (1,)), ((), ())), preferred_element_type=F32)
    k3 = kt.reshape(d // HEAD_DIM, HEAD_DIM, tm)
    k3 = k3 * lax.rsqrt(jnp.mean(k3 * k3, axis=1, keepdims=True) + EPS) * kn_ref[...][None]
    kt_ref[...] = k3.reshape(d, tm).astype(BF16)


def _qkv(x2d, g_q, g_kv, w_q, w_kv, k_norm_col):
    t, d = x2d.shape
    tm = PROJ_TILE
    const = lambda i: (0, 0)
    return pl.pallas_call(
        functools.partial(_qkv_kernel, tm=tm, d=d),
        grid=(t // tm,),
        in_specs=[
            pl.BlockSpec((tm, d), lambda i: (i, 0)),
            pl.BlockSpec((1, d), const),
            pl.BlockSpec((1, d), const),
            _resident((d, d)),
            _resident((d, 2 * d)),
            pl.BlockSpec((HEAD_DIM, 1), const),
        ],
        out_specs=[
            pl.BlockSpec((tm, d), lambda i: (i, 0)),
            pl.BlockSpec((d, tm), lambda i: (0, i)),
            pl.BlockSpec((tm, d), lambda i: (i, 0)),
        ],
        out_shape=[
            jax.ShapeDtypeStruct((t, d), BF16),
            jax.ShapeDtypeStruct((d, t), BF16),
            jax.ShapeDtypeStruct((t, d), BF16),
        ],
        scratch_shapes=[pltpu.VMEM((d, d), BF16)],
        compiler_params=_params("arbitrary"),
        name="qkv_proj",
    )(x2d, g_q, g_kv, w_q, w_kv, k_norm_col)


def _group_mean_sq(x, ones_bd, group):
    return _dot((x * x).astype(BF16), ones_bd) * (1.0 / group)


def _attn_kernel(bound_ref, q_ref, kt_ref, v_ref, qn_ref, lam_ref, sn_ref, wgu_ref, wdn_ref,
                 o_ref, wgu_bf_ref, wdn_bf_ref, v1_ref, qn_s, *, seq, tq, heads, lam_init):
    wgu_bf_ref[...] = wgu_ref[...].astype(BF16)
    wdn_bf_ref[...] = wdn_ref[...].astype(BF16)
    lp = lam_ref[...]
    lam = (jnp.exp(jnp.sum(lp[0:1] * lp[1:2], axis=-1, keepdims=True))
           - jnp.exp(jnp.sum(lp[2:3] * lp[3:4], axis=-1, keepdims=True)) + lam_init)
    row = lax.broadcasted_iota(jnp.int32, (tq, tq), 0)
    col = lax.broadcasted_iota(jnp.int32, (tq, tq), 1)
    causal = col <= row
    gi = lax.broadcasted_iota(jnp.int32, (V_DIM, V_DIM), 0) // HEAD_DIM
    gj = lax.broadcasted_iota(jnp.int32, (V_DIM, V_DIM), 1) // HEAD_DIM
    ones_bd = jnp.where(gi == gj, 1.0, 0.0).astype(BF16)
    qgain = qn_ref[...] * (HEAD_DIM ** -0.5 * LOG2E)
    maps = [slice(c * HEAD_DIM, (c + 1) * HEAD_DIM) for c in range(2)]
    n_q = seq // tq

    v1_ref[:, V_DIM:] = jnp.ones((seq, V_DIM), BF16)
    for hh in range(heads):
        hcols = slice(hh * V_DIM, (hh + 1) * V_DIM)
        hrow = hh * V_DIM
        v1_ref[:, 0:V_DIM] = v_ref[:, hcols]
        q = q_ref[:, hcols].astype(F32)
        ms = _group_mean_sq(q, ones_bd, HEAD_DIM)
        qn_s[...] = (q * lax.rsqrt(ms + EPS) * qgain).astype(BF16)

        def finish(q0, r1, r2, hcols=hcols):
            o = r1[:, 0:V_DIM] / r1[:, V_DIM:] - lam * (r2[:, 0:V_DIM] / r2[:, V_DIM:])
            o = _rms_scale(o) * sn_ref[...] * (1.0 - lam_init)
            o_ref[pl.ds(q0, tq), hcols] = o.astype(BF16)

        def kt_rows(sl, hrow=hrow):
            return slice(hrow + sl.start, hrow + sl.stop)

        @pl.when(bound_ref[0] <= SAFE_SHIFT)
        def _fixed_shift(finish=finish, kt_rows=kt_rows):
            shift = bound_ref[0] * LOG2E

            def scores(qi):
                q0 = qi * tq
                out = []
                for sl in maps:
                    qc = qn_s[q0:q0 + tq, sl]
                    s_diag = _dot(qc, kt_ref[kt_rows(sl), q0:q0 + tq])
                    s_low = _dot(qc, kt_ref[kt_rows(sl), 0:q0]) if q0 > 0 else None
                    out.append((s_diag, s_low))
                return out

            order = list(reversed(range(n_q)))
            pending = scores(order[0])
            for idx, qi in enumerate(order):
                q0 = qi * tq
                cur = pending
                if idx + 1 < n_q:
                    pending = scores(order[idx + 1])
                res = []
                for s_diag, s_low in cur:
                    p = jnp.where(causal, jnp.exp2(s_diag - shift), 0.0).astype(BF16)
                    r = _dot(p, v1_ref[q0:q0 + tq, :])
                    if s_low is not None:
                        r = r + _dot(jnp.exp2(s_low - shift).astype(BF16), v1_ref[0:q0, :])
                    res.append(r)
                finish(q0, *res)

        @pl.when(jnp.logical_not(bound_ref[0] <= SAFE_SHIFT))
        def _running_max(finish=finish, kt_rows=kt_rows):
            def q_body(qi, carry):
                q0 = pl.multiple_of(qi * tq, tq)
                qs = [qn_s[pl.ds(q0, tq), sl] for sl in maps]

                def tile(j, state, masked):
                    k0 = pl.multiple_of(j * tq, tq)
                    out = []
                    for c, sl in enumerate(maps):
                        m, r = state[c]
                        s = _dot(qs[c], kt_ref[kt_rows(sl), pl.ds(k0, tq)])
                        if masked:
                            s = jnp.where(causal, s, -jnp.inf)
                        m_new = jnp.maximum(m, jnp.max(s, axis=-1, keepdims=True))
                        p = jnp.exp2(s - m_new).astype(BF16)
                        r = jnp.exp2(m - m_new) * r + _dot(p, v1_ref[pl.ds(k0, tq), :])
                        out.append((m_new, r))
                    return tuple(out)

                one = (jnp.full((tq, 1), -jnp.inf, F32), jnp.zeros((tq, 2 * V_DIM), F32))
                state = lax.fori_loop(0, qi, lambda j, st: tile(j, st, False), (one, one))
                (_, r1), (_, r2) = tile(qi, state, True)
                finish(q0, r1, r2)
                return carry

            lax.fori_loop(0, n_q, q_body, 0)


def _attention(score_bound, q, kt, v, q_norm2, lam_params, sub_norm, w_gu, w_down, batch, seq, lam_init):
    t, d = q.shape
    n_heads = d // V_DIM
    hps = ATTN_HEADS_PER_STEP
    assert n_heads % hps == 0
    hsteps = n_heads // hps
    steps = batch * hsteps
    wgu2 = w_gu.reshape(-1, w_gu.shape[-1])
    wdn2 = w_down.reshape(-1, w_down.shape[-1])
    gu_rows, dn_rows = wgu2.shape[0] // steps, wdn2.shape[0] // steps
    assert wgu2.shape[0] % steps == 0 and wdn2.shape[0] % steps == 0 and gu_rows % 16 == 0 and dn_rows % 16 == 0
    slab = lambda b, h: (b * hsteps + h, 0)
    hw = hps * V_DIM
    kern = functools.partial(_attn_kernel, seq=seq, tq=ATTN_TILE, heads=hps, lam_init=lam_init)
    o, wgu_bf, wdn_bf = pl.pallas_call(
        kern,
        grid=(batch, hsteps),
        in_specs=[
            pl.BlockSpec(memory_space=pltpu.SMEM),
            pl.BlockSpec((seq, hw), lambda b, h: (b, h)),
            pl.BlockSpec((hw, seq), lambda b, h: (h, b)),
            pl.BlockSpec((seq, hw), lambda b, h: (b, h)),
            pl.BlockSpec((1, V_DIM), lambda b, h: (0, 0)),
            pl.BlockSpec((4, HEAD_DIM), lambda b, h: (0, 0)),
            pl.BlockSpec((1, V_DIM), lambda b, h: (0, 0)),
            pl.BlockSpec((gu_rows, wgu2.shape[1]), slab),
            pl.BlockSpec((dn_rows, wdn2.shape[1]), slab),
        ],
        out_specs=[
            pl.BlockSpec((seq, hw), lambda b, h: (b, h)),
            pl.BlockSpec((gu_rows, wgu2.shape[1]), slab),
            pl.BlockSpec((dn_rows, wdn2.shape[1]), slab),
        ],
        out_shape=[
            jax.ShapeDtypeStruct((t, d), BF16),
            jax.ShapeDtypeStruct(wgu2.shape, BF16),
            jax.ShapeDtypeStruct(wdn2.shape, BF16),
        ],
        scratch_shapes=[pltpu.VMEM((seq, 2 * V_DIM), BF16), pltpu.VMEM((seq, V_DIM), BF16)],
        compiler_params=_params("parallel", "parallel"),
        name="diff_attn",
    )(score_bound, q, kt, v, q_norm2, lam_params, sub_norm, wgu2, wdn2)
    return o, wgu_bf.reshape(w_gu.shape), wdn_bf.reshape(w_down.shape)


def _nt_dot(a, b):
    return lax.dot_general(a, b, (((1,), (1,)), ((), ())), preferred_element_type=F32)


def _oproj_router_kernel(x_ref, o_ref, wo_ref, g_ref, rwt_ref,
                         x3_ref, h8_ref, gate_ref, route_ref, cnt_ref, carry,
                         *, tm, d, n_experts):
    i = pl.program_id(0)

    @pl.when(i == 0)
    def _():
        carry[...] = jnp.zeros_like(carry)

    x3 = x_ref[...] + _dot(o_ref[...], wo_ref[...].astype(BF16))
    x3_ref[...] = x3
    h = _rms_scale(x3) * g_ref[...]
    for c in range(d // LANES):
        h8_ref[pl.ds(c, tm, stride=d // LANES), :] = h[:, c * LANES:(c + 1) * LANES]

    rwt = rwt_ref[...]
    h_hi = h.astype(BF16)
    h_lo = (h - h_hi.astype(F32)).astype(BF16)
    w_hi = rwt.astype(BF16)
    w_lo = (rwt - w_hi.astype(F32)).astype(BF16)
    logits = _nt_dot(w_hi, h_hi) + (_nt_dot(w_lo, h_hi) + _nt_dot(w_hi, h_lo))

    rows = rwt.shape[0]
    sub = lax.broadcasted_iota(jnp.int32, (rows, tm), 0)
    lane = lax.broadcasted_iota(jnp.int32, (rows, tm), 1)
    neg = -jnp.inf
    l1 = jnp.where(sub < n_experts, logits, neg)
    m1 = jnp.max(l1, axis=0, keepdims=True)
    i1 = jnp.min(jnp.where(l1 == m1, sub, rows), axis=0, keepdims=True)
    l2 = jnp.where(sub == i1, neg, l1)
    m2 = jnp.max(l2, axis=0, keepdims=True)
    i2 = jnp.min(jnp.where(l2 == m2, sub, rows), axis=0, keepdims=True)
    e2 = jnp.exp(m2 - m1)
    den = 1.0 + e2
    gate_ref[...] = jnp.where(sub == 0, 1.0 / den, jnp.where(sub == 1, e2 / den, 0.0))

    oh1 = sub == i1
    oh2 = sub == i2
    cnt = jnp.where(oh1, 1.0, 0.0) + jnp.where(oh2, 1.0, 0.0)
    incl = cnt
    shift = 1
    while shift < tm:
        incl = incl + jnp.where(lane >= shift, pltpu.roll(incl, shift, 1), 0.0)
        shift *= 2
    before = incl - cnt + carry[:, 0:1]
    r1 = jnp.sum(jnp.where(oh1, before, 0.0), axis=0, keepdims=True)
    r2 = jnp.sum(jnp.where(oh2, before, 0.0), axis=0, keepdims=True)
    route = jnp.where(sub == 0, i1.astype(F32), jnp.where(sub == 1, i2.astype(F32),
                      jnp.where(sub == 2, r1, jnp.where(sub == 3, r2, 0.0))))
    route_ref[...] = route.astype(jnp.int32)
    total = carry[...] + jnp.sum(cnt, axis=1, keepdims=True)
    carry[...] = total
    cnt_ref[...] = total.astype(jnp.int32)


def _oproj_router(x2d, o, w_o, g, rw_t, n_experts):
    t, d = x2d.shape
    tm = PROJ_TILE
    const = lambda i: (0, 0)
    kern = functools.partial(_oproj_router_kernel, tm=tm, d=d, n_experts=n_experts)
    return pl.pallas_call(
        kern,
        grid=(t // tm,),
        in_specs=[
            pl.BlockSpec((tm, d), lambda i: (i, 0)),
            pl.BlockSpec((tm, d), lambda i: (i, 0)),
            _resident((d, d)),
            pl.BlockSpec((1, d), const),
            pl.BlockSpec((SUBLANES, d), const),
        ],
        out_specs=[
            pl.BlockSpec((tm, d), lambda i: (i, 0)),
            pl.BlockSpec((tm * (d // LANES), LANES), lambda i: (i, 0)),
            pl.BlockSpec((SUBLANES, tm), lambda i: (0, i)),
            pl.BlockSpec((SUBLANES, tm), lambda i: (0, i)),
            pl.BlockSpec((SUBLANES, LANES), const),
        ],
        out_shape=[
            jax.ShapeDtypeStruct((t, d), F32),
            jax.ShapeDtypeStruct((t * (d // LANES), LANES), F32),
            jax.ShapeDtypeStruct((SUBLANES, t), F32),
            jax.ShapeDtypeStruct((SUBLANES, t), jnp.int32),
            jax.ShapeDtypeStruct((SUBLANES, LANES), jnp.int32),
        ],
        scratch_shapes=[pltpu.VMEM((SUBLANES, LANES), F32)],
        compiler_params=_params("arbitrary"),
        name="oproj_router",
    )(x2d, o, w_o, g, rw_t)


def _pad_bits(tile):
    return [1 << b for b in reversed(range(int(math.log2(tile))))]


def _dispatch_kernel(s0_ref, s1_ref, pad_start_ref, pad_len_ref, h_ref, hs_hbm, zeros, sem, zsem,
                     *, tb, nc, n_experts, tile):
    i = pl.program_id(0)
    base = i * tb

    def rows(ref, first, count):
        start = first * nc
        if not isinstance(start, int):
            start = pl.multiple_of(start, nc)
        return ref.at[pl.ds(start, count * nc), :]

    def row_copy(r, slot):
        return pltpu.make_async_copy(rows(h_ref, r, 1), rows(hs_hbm, slot, 1), sem)

    def issue(g, carry):
        for u in range(DMA_UNROLL):
            r = g * DMA_UNROLL + u
            row_copy(r, s0_ref[base + r]).start(priority=0)
            row_copy(r, s1_ref[base + r]).start(priority=1)
        return carry

    lax.fori_loop(0, tb // DMA_UNROLL, issue, 0)

    def pad_copies(do):
        for e in range(n_experts):
            n = pad_len_ref[e]
            pos = pad_start_ref[e]
            for bit in _pad_bits(tile):
                @pl.when((n & bit) != 0)
                def _(pos=pos, bit=bit):
                    do(pltpu.make_async_copy(rows(zeros, 0, bit), rows(hs_hbm, pos, bit), zsem))
                pos = pos + (n & bit)
        for k in range(n_experts):
            @pl.when(k < pad_len_ref[n_experts])
            def _(k=k):
                pos = pad_start_ref[n_experts] + k * tile
                do(pltpu.make_async_copy(zeros, rows(hs_hbm, pos, tile), zsem))

    @pl.when(i == 0)
    def _():
        zeros[...] = jnp.zeros_like(zeros)
        pad_copies(lambda cp: cp.start())
        pad_copies(lambda cp: cp.wait())

    def drain(g, carry):
        for u in range(2 * DMA_UNROLL):
            row_copy(0, 0).wait()
        return carry

    lax.fori_loop(0, tb // DMA_UNROLL, drain, 0)


def _dispatch(slot0, slot1, pad_start, pad_len, h8, nc, n_slots, n_experts):
    t = h8.shape[0] // nc
    tb = DISPATCH_BLOCK
    kern = functools.partial(_dispatch_kernel, tb=tb, nc=nc, n_experts=n_experts, tile=GROUP_TILE)
    grid_spec = pltpu.PrefetchScalarGridSpec(
        num_scalar_prefetch=4,
        grid=(t // tb,),
        in_specs=[pl.BlockSpec((tb * nc, LANES), lambda i, *_: (i, 0))],
        out_specs=pl.BlockSpec(memory_space=pl.ANY),
        scratch_shapes=[
            pltpu.VMEM((GROUP_TILE * nc, LANES), F32),
            pltpu.SemaphoreType.DMA,
            pltpu.SemaphoreType.DMA,
        ],
    )
    return pl.pallas_call(
        kern,
        grid_spec=grid_spec,
        out_shape=jax.ShapeDtypeStruct((n_slots * nc, LANES), F32),
        compiler_params=_params("arbitrary"),
        name="moe_dispatch",
    )(slot0, slot1, pad_start, pad_len, h8)


def _moe_group_kernel(te_ref, nt_ref, valid_ref, hs_ref, wg_ref, wu_ref, wd_ref, ys_ref, *, tm, d, f_exp):
    j = pl.program_id(0)
    nc = d // LANES
    valid = valid_ref[j]

    def expert_ffn(rows):
        h = jnp.concatenate([hs_ref[pl.ds(c, rows, stride=nc), :].astype(BF16) for c in range(nc)], axis=1)
        y = None
        for lo in range(0, f_exp, MOE_F_CHUNK):
            hi = min(lo + MOE_F_CHUNK, f_exp)
            gate = _dot(h, wg_ref[:, lo:hi])
            up = _dot(h, wu_ref[:, lo:hi])
            a = (gate * jax.nn.sigmoid(gate) * up).astype(BF16)
            p = _dot(a, wd_ref[lo:hi, :])
            y = p if y is None else y + p
        for c in range(nc):
            ys_ref[pl.ds(c, rows, stride=nc), :] = y[:, c * LANES:(c + 1) * LANES]

    half = tm // 2

    @pl.when(valid > half)
    def _():
        expert_ffn(tm)

    @pl.when(jnp.logical_and(valid > 0, valid <= half))
    def _():
        expert_ffn(half)
        ys_ref[half * nc:, :] = jnp.zeros(((tm - half) * nc, LANES), F32)

    @pl.when(valid == 0)
    def _():
        ys_ref[...] = jnp.zeros_like(ys_ref)


def _moe_group(tile_expert, n_tiles, tile_valid, hs2d, w_gu, w_down, max_tiles):
    n_e, d, f2 = w_gu.shape
    f_exp = f2 // 2
    tm = GROUP_TILE
    nc = d // LANES

    def row_map(j, te, nt, valid):
        return (jnp.minimum(j, nt[0] - 1), 0)

    kern = functools.partial(_moe_group_kernel, tm=tm, d=d, f_exp=f_exp)
    grid_spec = pltpu.PrefetchScalarGridSpec(
        num_scalar_prefetch=3,
        grid=(max_tiles,),
        in_specs=[
            pl.BlockSpec((tm * nc, LANES), row_map),
            pl.BlockSpec((None, d, f_exp), lambda j, te, nt, valid: (te[j], 0, 0)),
            pl.BlockSpec((None, d, f_exp), lambda j, te, nt, valid: (te[j], 0, 1)),
            pl.BlockSpec((None, f_exp, d), lambda j, te, nt, valid: (te[j], 0, 0)),
        ],
        out_specs=pl.BlockSpec((tm * nc, LANES), lambda j, te, nt, valid: (j, 0)),
    )
    return pl.pallas_call(
        kern,
        grid_spec=grid_spec,
        out_shape=jax.ShapeDtypeStruct(hs2d.shape, F32),
        compiler_params=pltpu.CompilerParams(dimension_semantics=("arbitrary",),
                                             vmem_limit_bytes=MOE_VMEM_LIMIT),
        name="moe_group",
    )(tile_expert, n_tiles, tile_valid, hs2d, w_gu, w_gu, w_down)


def _combine_kernel(s0_ref, s1_ref, x_ref, gate_ref, ys_hbm, o_ref, b00, b01, b10, b11, sems, *, tm, d):
    i = pl.program_id(0)
    n = pl.num_programs(0)
    nc = d // LANES
    bufs = ((b00, b01), (b10, b11))

    def row_copy(slot, par, k, r):
        start = r * nc
        if not isinstance(start, int):
            start = pl.multiple_of(start, nc)
        src = ys_hbm.at[pl.ds(pl.multiple_of(slot * nc, nc), nc), :]
        return pltpu.make_async_copy(src, bufs[par][k].at[pl.ds(start, nc), :], sems.at[par])

    def issue(block, par):
        base = block * tm

        def body(g, carry):
            for u in range(DMA_UNROLL):
                r = g * DMA_UNROLL + u
                row_copy(s0_ref[base + r], par, 0, r).start(priority=0)
                row_copy(s1_ref[base + r], par, 1, r).start(priority=1)
            return carry

        lax.fori_loop(0, tm // DMA_UNROLL, body, 0)

    def drain(par):
        def body(g, carry):
            for u in range(DMA_UNROLL):
                row_copy(0, par, 0, 0).wait()
                row_copy(0, par, 1, 0).wait()
            return carry

        lax.fori_loop(0, tm // DMA_UNROLL, body, 0)

    @pl.when(i == 0)
    def _():
        issue(0, 0)

    for par in range(2):
        @pl.when(i % 2 == par)
        def _(par=par):
            @pl.when(i + 1 < n)
            def _():
                issue(i + 1, 1 - par)

            drain(par)
            gates = jnp.transpose(gate_ref[...])
            g0 = gates[:, 0:1]
            g1 = gates[:, 1:2]
            for c in range(nc):
                y0 = bufs[par][0][pl.ds(c, tm, stride=nc), :]
                y1 = bufs[par][1][pl.ds(c, tm, stride=nc), :]
                cols = slice(c * LANES, (c + 1) * LANES)
                o_ref[:, cols] = x_ref[:, cols] + (g0 * y0 + g1 * y1)


def _combine(slot0, slot1, x3, gates, ys2d):
    t, d = x3.shape
    tm = COMBINE_BLOCK
    nc = d // LANES
    grid_spec = pltpu.PrefetchScalarGridSpec(
        num_scalar_prefetch=2,
        grid=(t // tm,),
        in_specs=[
            pl.BlockSpec((tm, d), lambda i, s0, s1: (i, 0)),
            pl.BlockSpec((SUBLANES, tm), lambda i, s0, s1: (0, i)),
            pl.BlockSpec(memory_space=pl.ANY),
        ],
        out_specs=pl.BlockSpec((tm, d), lambda i, s0, s1: (i, 0)),
        scratch_shapes=[
            pltpu.VMEM((tm * nc, LANES), F32),
            pltpu.VMEM((tm * nc, LANES), F32),
            pltpu.VMEM((tm * nc, LANES), F32),
            pltpu.VMEM((tm * nc, LANES), F32),
            pltpu.SemaphoreType.DMA((2,)),
        ],
    )
    return pl.pallas_call(
        functools.partial(_combine_kernel, tm=tm, d=d),
        grid_spec=grid_spec,
        out_shape=jax.ShapeDtypeStruct((t, d), F32),
        compiler_params=_params("arbitrary"),
        name="moe_combine",
    )(slot0, slot1, x3, gates, ys2d)


def _routing_tables(route, counts, n_experts, max_tiles):
    tile = GROUP_TILE
    padded = ((counts + tile - 1) // tile) * tile
    ends = jnp.cumsum(padded)
    offs = ends - padded
    slots = []
    for k in range(TOP_K):
        expert, rank = route[k], route[TOP_K + k]
        base = jnp.zeros_like(expert)
        for e in range(n_experts):
            base = jnp.where(expert == e, offs[e], base)
        slots.append(base + rank)
    tile_ends = ends // tile
    n_tiles = tile_ends[-1]
    j = jnp.arange(max_tiles, dtype=jnp.int32)
    tile_expert = jnp.sum((jnp.minimum(j, n_tiles - 1)[:, None] >= tile_ends[None, :]).astype(jnp.int32), axis=-1)
    row_end = jnp.sum(jnp.where(tile_expert[:, None] == jnp.arange(n_experts, dtype=jnp.int32)[None, :],
                                (offs + counts)[None, :], 0), axis=-1)
    tile_valid = jnp.where(j < n_tiles, jnp.clip(row_end - j * tile, 0, tile), 0)
    pad_start = jnp.concatenate([offs + counts, ends[-1:]]).astype(jnp.int32)
    pad_len = jnp.concatenate([padded - counts, max_tiles - n_tiles.reshape(1)]).astype(jnp.int32)
    return (slots[0], slots[1], pad_start, pad_len,
            tile_expert.astype(jnp.int32), n_tiles.reshape(1).astype(jnp.int32), tile_valid.astype(jnp.int32))


def _lambda_init(layer_idx_1based):
    return 0.8 - 0.6 * math.exp(-0.3 * (layer_idx_1based - 1))


def kernel(x, ln_mix, ln_ffn, conv_w_in, conv_w, conv_w_out, ln_kv, w_kv, k_norm, attn_w_q, q_norm, lam_params,
           sub_norm, attn_w_o, ffn_w_gu, ffn_w_down, router_w, moe_w_gu, moe_w_down):
    batch, seq, d = x.shape
    t = batch * seq
    n_experts = router_w.shape[-1]
    assert ln_mix.shape[0] == 2 and conv_w_in.shape[0] == 1 and attn_w_q.shape[0] == 1
    assert seq % ROW_TILE == 0 and seq % ATTN_TILE == 0 and d % LANES == 0
    nc = d // LANES
    x2d = x.reshape(t, d)

    x1 = _mixer_a(x2d, ln_mix[0:1], conv_w_in[0], conv_w[0], conv_w_out[0], seq)
    x2 = _ffn_dense(x1, ln_ffn[0:1], ffn_w_gu[0], ffn_w_down[0])

    assert w_kv.shape == (d, 2 * d)
    q, kt, v = _qkv(x2, ln_mix[1:2], ln_kv.reshape(1, d), attn_w_q[0], w_kv, k_norm.reshape(HEAD_DIM, 1))

    score_bound = (math.sqrt(HEAD_DIM) * jnp.max(jnp.abs(q_norm[0])) * jnp.max(jnp.abs(k_norm))).reshape(1)
    o, moe_gu_bf, moe_down_bf = _attention(score_bound, q, kt, v, jnp.tile(q_norm[0:1], (1, 2)), lam_params[0],
                                           sub_norm[0:1], moe_w_gu[0], moe_w_down[0], batch, seq, _lambda_init(2))

    assert n_experts <= SUBLANES
    rw_t = jnp.pad(router_w[0].T, ((0, SUBLANES - n_experts), (0, 0)))
    x3, h8, gates, route, counts = _oproj_router(x2, o, attn_w_o[0], ln_ffn[1:2], rw_t, n_experts)
    max_tiles = (TOP_K * t) // GROUP_TILE + n_experts
    n_slots = max_tiles * GROUP_TILE
    slot0, slot1, pad_start, pad_len, tile_expert, n_tiles, tile_valid = _routing_tables(
        route, counts[:n_experts, 0], n_experts, max_tiles)
    hs = _dispatch(slot0, slot1, pad_start, pad_len, h8, nc, n_slots, n_experts)
    ys = _moe_group(tile_expert, n_tiles, tile_valid, hs, moe_gu_bf, moe_down_bf, max_tiles)
    out = _combine(slot0, slot1, x3, gates, ys)
    return out.reshape(batch, seq, d)
```

```python
import functools
import math

import jax
import jax.numpy as jnp
from jax import lax
from jax.experimental import pallas as pl
from jax.experimental.pallas import tpu as pltpu

F32 = jnp.float32
BF16 = jnp.bfloat16

EPS = 1e-6
HEAD_DIM = 64
V_DIM = 2 * HEAD_DIM
CONV_WIDTH = 3
TOP_K = 2

LANES = 128
SUBLANES = 8
VMEM_LIMIT = 56 * 1024 * 1024

ROW_TILE = 512
PROJ_TILE = 1024
ATTN_TILE = 256
ATTN_HEADS_PER_STEP = 2
GROUP_TILE = 512
MOE_F_CHUNK = 1792
MOE_VMEM_LIMIT = 62 * 1024 * 1024
DISPATCH_BLOCK = 2048
COMBINE_BLOCK = 256
DMA_UNROLL = 16

LOG2E = 1.4426950408889634
SAFE_SHIFT = 40.0


def _rms_scale(x):
    return x * lax.rsqrt(jnp.mean(x * x, axis=-1, keepdims=True) + EPS)


def _dot(a, b):
    return jnp.dot(a, b, preferred_element_type=F32)


def _resident(shape):
    return pl.BlockSpec(shape, lambda *_: (0,) * len(shape), pipeline_mode=pl.Buffered(1))


def _params(*sem):
    return pltpu.CompilerParams(dimension_semantics=sem, vmem_limit_bytes=VMEM_LIMIT)


def _mixer_a_kernel(x_ref, g_ref, win_ref, cw_ref, wout_ref, o_ref, ubuf, *, tm, d, tiles_per_seq):
    i = pl.program_id(0)
    x = x_ref[...]
    h = (_rms_scale(x) * g_ref[...]).astype(BF16)
    c = _dot(h, win_ref[:, d:2 * d].astype(BF16))
    v = _dot(h, win_ref[:, 2 * d:].astype(BF16))
    u = c * v

    @pl.when(i % tiles_per_seq == 0)
    def _():
        ubuf[0:SUBLANES, :] = jnp.zeros((SUBLANES, d), F32)

    ubuf[SUBLANES:tm + SUBLANES, :] = u
    u1 = ubuf[SUBLANES - 1:tm + SUBLANES - 1, :]
    u2 = ubuf[SUBLANES - 2:tm + SUBLANES - 2, :]
    cw = cw_ref[...]
    z = u2 * cw[0:1] + u1 * cw[1:2] + u * cw[2:3]
    b = _dot(h, win_ref[:, 0:d].astype(BF16))
    y = (b * z).astype(BF16)
    o_ref[...] = x + _dot(y, wout_ref[...].astype(BF16))
    ubuf[0:SUBLANES, :] = ubuf[tm:tm + SUBLANES, :]


def _mixer_a(x2d, g, w_in, conv_w, w_out, seq):
    t, d = x2d.shape
    tm = PROJ_TILE
    kern = functools.partial(_mixer_a_kernel, tm=tm, d=d, tiles_per_seq=seq // tm)
    return pl.pallas_call(
        kern,
        grid=(t // tm,),
        in_specs=[
            pl.BlockSpec((tm, d), lambda i: (i, 0)),
            pl.BlockSpec((1, d), lambda i: (0, 0)),
            _resident((d, 3 * d)),
            pl.BlockSpec((CONV_WIDTH, d), lambda i: (0, 0)),
            _resident((d, d)),
        ],
        out_specs=pl.BlockSpec((tm, d), lambda i: (i, 0)),
        out_shape=jax.ShapeDtypeStruct((t, d), F32),
        scratch_shapes=[pltpu.VMEM((tm + SUBLANES, d), F32)],
        compiler_params=_params("arbitrary"),
        name="mixer_a",
    )(x2d, g, w_in, conv_w, w_out)


def _ffn_chunks(f):
    step = 768
    return [(lo, min(lo + step, f)) for lo in range(0, f, step)]


def _ffn_kernel(x_ref, g_ref, wgu_ref, wd_ref, o_ref, *, f):
    x = x_ref[...]
    h = (_rms_scale(x) * g_ref[...]).astype(BF16)
    acc = x
    for lo, hi in _ffn_chunks(f):
        gate = _dot(h, wgu_ref[:, lo:hi].astype(BF16))
        up = _dot(h, wgu_ref[:, f + lo:f + hi].astype(BF16))
        a = (gate * jax.nn.sigmoid(gate) * up).astype(BF16)
        acc = acc + _dot(a, wd_ref[lo:hi, :].astype(BF16))
    o_ref[...] = acc


def _ffn_dense(x2d, g, w_gu, w_down):
    t, d = x2d.shape
    f = w_down.shape[0]
    tm = ROW_TILE
    return pl.pallas_call(
        functools.partial(_ffn_kernel, f=f),
        grid=(t // tm,),
        in_specs=[
            pl.BlockSpec((tm, d), lambda i: (i, 0)),
            pl.BlockSpec((1, d), lambda i: (0, 0)),
            _resident((d, 2 * f)),
            _resident((f, d)),
        ],
        out_specs=pl.BlockSpec((tm, d), lambda i: (i, 0)),
        out_shape=jax.ShapeDtypeStruct((t, d), F32),
        compiler_params=_params("parallel"),
        name="ffn_dense",
    )(x2d, g, w_gu, w_down)


def _qkv_kernel(x_ref, gq_ref, gkv_ref, wq_ref, wkv_ref, kn_ref, q_ref, kt_ref, v_ref, wkt_s, *, tm, d):
    @pl.when(pl.program_id(0) == 0)
    def _():
        wkt_s[...] = jnp.transpose(wkv_ref[:, 0:d]).astype(BF16)

    y = _rms_scale(x_ref[...])
    hq = (y * gq_ref[...]).astype(BF16)
    hkv = (y * gkv_ref[...]).astype(BF16)
    q_ref[...] = _dot(hq, wq_ref[...].astype(BF16)).astype(BF16)
    v_ref[...] = _dot(hkv, wkv_ref[:, d:].astype(BF16)).astype(BF16)
    kt = lax.dot_general(wkt_s[...], hkv, (((1,), (1,)), ((), ())), preferred_element_type=F32)
    k3 = kt.reshape(d // HEAD_DIM, HEAD_DIM, tm)
    k3 = k3 * lax.rsqrt(jnp.mean(k3 * k3, axis=1, keepdims=True) + EPS) * kn_ref[...][None]
    kt_ref[...] = k3.reshape(d, tm).astype(BF16)


def _qkv(x2d, g_q, g_kv, w_q, w_kv, k_norm_col):
    t, d = x2d.shape
    tm = PROJ_TILE
    const = lambda i: (0, 0)
    return pl.pallas_call(
        functools.partial(_qkv_kernel, tm=tm, d=d),
        grid=(t // tm,),
        in_specs=[
            pl.BlockSpec((tm, d), lambda i: (i, 0)),
            pl.BlockSpec((1, d), const),
            pl.BlockSpec((1, d), const),
            _resident((d, d)),
            _resident((d, 2 * d)),
            pl.BlockSpec((HEAD_DIM, 1), const),
        ],
        out_specs=[
            pl.BlockSpec((tm, d), lambda i: (i, 0)),
            pl.BlockSpec((d, tm), lambda i: (0, i)),
            pl.BlockSpec((tm, d), lambda i: (i, 0)),
        ],
        out_shape=[
            jax.ShapeDtypeStruct((t, d), BF16),
            jax.ShapeDtypeStruct((d, t), BF16),
            jax.ShapeDtypeStruct((t, d), BF16),
        ],
        scratch_shapes=[pltpu.VMEM((d, d), BF16)],
        compiler_params=_params("arbitrary"),
        name="qkv_proj",
    )(x2d, g_q, g_kv, w_q, w_kv, k_norm_col)


def _group_mean_sq(x, ones_bd, group):
    return _dot((x * x).astype(BF16), ones_bd) * (1.0 / group)


def _attn_kernel(q_ref, kt_ref, v_ref, qn_ref, kn_ref, lam_ref, sn_ref, wgu_ref, wdn_ref,
                 o_ref, wgu_bf_ref, wdn_bf_ref, v1_ref, qn_s, *, seq, tq, heads, lam_init):
    wgu_bf_ref[...] = wgu_ref[...].astype(BF16)
    wdn_bf_ref[...] = wdn_ref[...].astype(BF16)
    lp = lam_ref[...]
    lam = (jnp.exp(jnp.sum(lp[0:1] * lp[1:2], axis=-1, keepdims=True))
           - jnp.exp(jnp.sum(lp[2:3] * lp[3:4], axis=-1, keepdims=True)) + lam_init)
    row = lax.broadcasted_iota(jnp.int32, (tq, tq), 0)
    col = lax.broadcasted_iota(jnp.int32, (tq, tq), 1)
    causal = col <= row
    gi = lax.broadcasted_iota(jnp.int32, (V_DIM, V_DIM), 0) // HEAD_DIM
    gj = lax.broadcasted_iota(jnp.int32, (V_DIM, V_DIM), 1) // HEAD_DIM
    ones_bd = jnp.where(gi == gj, 1.0, 0.0).astype(BF16)
    qgain = qn_ref[...] * (HEAD_DIM ** -0.5 * LOG2E)
    bound = math.sqrt(HEAD_DIM) * jnp.max(jnp.abs(qn_ref[...])) * jnp.max(jnp.abs(kn_ref[...]))
    maps = [slice(c * HEAD_DIM, (c + 1) * HEAD_DIM) for c in range(2)]
    n_q = seq // tq

    v1_ref[:, V_DIM:] = jnp.ones((seq, V_DIM), BF16)
    for hh in range(heads):
        hcols = slice(hh * V_DIM, (hh + 1) * V_DIM)
        hrow = hh * V_DIM
        v1_ref[:, 0:V_DIM] = v_ref[:, hcols]
        q = q_ref[:, hcols].astype(F32)
        ms = _group_mean_sq(q, ones_bd, HEAD_DIM)
        qn_s[...] = (q * lax.rsqrt(ms + EPS) * qgain).astype(BF16)

        def finish(q0, r1, r2, hcols=hcols):
            o = r1[:, 0:V_DIM] / r1[:, V_DIM:] - lam * (r2[:, 0:V_DIM] / r2[:, V_DIM:])
            o = _rms_scale(o) * sn_ref[...] * (1.0 - lam_init)
            o_ref[pl.ds(q0, tq), hcols] = o.astype(BF16)

        def kt_rows(sl, hrow=hrow):
            return slice(hrow + sl.start, hrow + sl.stop)

        @pl.when(bound <= SAFE_SHIFT)
        def _fixed_shift(finish=finish, kt_rows=kt_rows):
            shift = bound * LOG2E

            def scores(qi):
                q0 = qi * tq
                out = []
                for sl in maps:
                    qc = qn_s[q0:q0 + tq, sl]
                    s_diag = _dot(qc, kt_ref[kt_rows(sl), q0:q0 + tq])
                    s_low = _dot(qc, kt_ref[kt_rows(sl), 0:q0]) if q0 > 0 else None
                    out.append((s_diag, s_low))
                return out

            order = list(reversed(range(n_q)))
            pending = scores(order[0])
            for idx, qi in enumerate(order):
                q0 = qi * tq
                cur = pending
                if idx + 1 < n_q:
                    pending = scores(order[idx + 1])
                res = []
                for s_diag, s_low in cur:
                    p = jnp.where(causal, jnp.exp2(s_diag - shift), 0.0).astype(BF16)
                    r = _dot(p, v1_ref[q0:q0 + tq, :])
                    if s_low is not None:
                        r = r + _dot(jnp.exp2(s_low - shift).astype(BF16), v1_ref[0:q0, :])
                    res.append(r)
                finish(q0, *res)

        @pl.when(jnp.logical_not(bound <= SAFE_SHIFT))
        def _running_max(finish=finish, kt_rows=kt_rows):
            def q_body(qi, carry):
                q0 = pl.multiple_of(qi * tq, tq)
                qs = [qn_s[pl.ds(q0, tq), sl] for sl in maps]

                def tile(j, state, masked):
                    k0 = pl.multiple_of(j * tq, tq)
                    out = []
                    for c, sl in enumerate(maps):
                        m, r = state[c]
                        s = _dot(qs[c], kt_ref[kt_rows(sl), pl.ds(k0, tq)])
                        if masked:
                            s = jnp.where(causal, s, -jnp.inf)
                        m_new = jnp.maximum(m, jnp.max(s, axis=-1, keepdims=True))
                        p = jnp.exp2(s - m_new).astype(BF16)
                        r = jnp.exp2(m - m_new) * r + _dot(p, v1_ref[pl.ds(k0, tq), :])
                        out.append((m_new, r))
                    return tuple(out)

                one = (jnp.full((tq, 1), -jnp.inf, F32), jnp.zeros((tq, 2 * V_DIM), F32))
                state = lax.fori_loop(0, qi, lambda j, st: tile(j, st, False), (one, one))
                (_, r1), (_, r2) = tile(qi, state, True)
                finish(q0, r1, r2)
                return carry

            lax.fori_loop(0, n_q, q_body, 0)


def _attention(q, kt, v, q_norm2, k_norm2, lam_params, sub_norm, w_gu, w_down, batch, seq, lam_init):
    t, d = q.shape
    n_heads = d // V_DIM
    hps = ATTN_HEADS_PER_STEP
    assert n_heads % hps == 0
    hsteps = n_heads // hps
    steps = batch * hsteps
    wgu2 = w_gu.reshape(-1, w_gu.shape[-1])
    wdn2 = w_down.reshape(-1, w_down.shape[-1])
    gu_rows, dn_rows = wgu2.shape[0] // steps, wdn2.shape[0] // steps
    assert wgu2.shape[0] % steps == 0 and wdn2.shape[0] % steps == 0 and gu_rows % 16 == 0 and dn_rows % 16 == 0
    slab = lambda b, h: (b * hsteps + h, 0)
    hw = hps * V_DIM
    kern = functools.partial(_attn_kernel, seq=seq, tq=ATTN_TILE, heads=hps, lam_init=lam_init)
    o, wgu_bf, wdn_bf = pl.pallas_call(
        kern,
        grid=(batch, hsteps),
        in_specs=[
            pl.BlockSpec((seq, hw), lambda b, h: (b, h)),
            pl.BlockSpec((hw, seq), lambda b, h: (h, b)),
            pl.BlockSpec((seq, hw), lambda b, h: (b, h)),
            pl.BlockSpec((1, V_DIM), lambda b, h: (0, 0)),
            pl.BlockSpec((1, V_DIM), lambda b, h: (0, 0)),
            pl.BlockSpec((4, HEAD_DIM), lambda b, h: (0, 0)),
            pl.BlockSpec((1, V_DIM), lambda b, h: (0, 0)),
            pl.BlockSpec((gu_rows, wgu2.shape[1]), slab),
            pl.BlockSpec((dn_rows, wdn2.shape[1]), slab),
        ],
        out_specs=[
            pl.BlockSpec((seq, hw), lambda b, h: (b, h)),
            pl.BlockSpec((gu_rows, wgu2.shape[1]), slab),
            pl.BlockSpec((dn_rows, wdn2.shape[1]), slab),
        ],
        out_shape=[
            jax.ShapeDtypeStruct((t, d), BF16),
            jax.ShapeDtypeStruct(wgu2.shape, BF16),
            jax.ShapeDtypeStruct(wdn2.shape, BF16),
        ],
        scratch_shapes=[pltpu.VMEM((seq, 2 * V_DIM), BF16), pltpu.VMEM((seq, V_DIM), BF16)],
        compiler_params=_params("parallel", "parallel"),
        name="diff_attn",
    )(q, kt, v, q_norm2, k_norm2, lam_params, sub_norm, wgu2, wdn2)
    return o, wgu_bf.reshape(w_gu.shape), wdn_bf.reshape(w_down.shape)


def _nt_dot(a, b):
    return lax.dot_general(a, b, (((1,), (1,)), ((), ())), preferred_element_type=F32)


def _oproj_router_kernel(x_ref, o_ref, wo_ref, g_ref, rwt_ref,
                         x3_ref, h8_ref, gate_ref, route_ref, cnt_ref, carry,
                         *, tm, d, n_experts):
    i = pl.program_id(0)

    @pl.when(i == 0)
    def _():
        carry[...] = jnp.zeros_like(carry)

    x3 = x_ref[...] + _dot(o_ref[...], wo_ref[...].astype(BF16))
    x3_ref[...] = x3
    h = _rms_scale(x3) * g_ref[...]
    for c in range(d // LANES):
        h8_ref[pl.ds(c, tm, stride=d // LANES), :] = h[:, c * LANES:(c + 1) * LANES]

    rwt = rwt_ref[...]
    h_hi = h.astype(BF16)
    h_lo = (h - h_hi.astype(F32)).astype(BF16)
    w_hi = rwt.astype(BF16)
    w_lo = (rwt - w_hi.astype(F32)).astype(BF16)
    logits = _nt_dot(w_hi, h_hi) + (_nt_dot(w_lo, h_hi) + _nt_dot(w_hi, h_lo))

    rows = rwt.shape[0]
    sub = lax.broadcasted_iota(jnp.int32, (rows, tm), 0)
    lane = lax.broadcasted_iota(jnp.int32, (rows, tm), 1)
    neg = -jnp.inf
    l1 = jnp.where(sub < n_experts, logits, neg)
    m1 = jnp.max(l1, axis=0, keepdims=True)
    i1 = jnp.min(jnp.where(l1 == m1, sub, rows), axis=0, keepdims=True)
    l2 = jnp.where(sub == i1, neg, l1)
    m2 = jnp.max(l2, axis=0, keepdims=True)
    i2 = jnp.min(jnp.where(l2 == m2, sub, rows), axis=0, keepdims=True)
    e2 = jnp.exp(m2 - m1)
    den = 1.0 + e2
    gate_ref[...] = jnp.where(sub == 0, 1.0 / den, jnp.where(sub == 1, e2 / den, 0.0))

    oh1 = sub == i1
    oh2 = sub == i2
    cnt = jnp.where(oh1, 1.0, 0.0) + jnp.where(oh2, 1.0, 0.0)
    incl = cnt
    shift = 1
    while shift < tm:
        incl = incl + jnp.where(lane >= shift, pltpu.roll(incl, shift, 1), 0.0)
        shift *= 2
    before = incl - cnt + carry[:, 0:1]
    r1 = jnp.sum(jnp.where(oh1, before, 0.0), axis=0, keepdims=True)
    r2 = jnp.sum(jnp.where(oh2, before, 0.0), axis=0, keepdims=True)
    route = jnp.where(sub == 0, i1.astype(F32), jnp.where(sub == 1, i2.astype(F32),
                      jnp.where(sub == 2, r1, jnp.where(sub == 3, r2, 0.0))))
    route_ref[...] = route.astype(jnp.int32)
    total = carry[...] + jnp.sum(cnt, axis=1, keepdims=True)
    carry[...] = total
    cnt_ref[...] = total.astype(jnp.int32)


def _oproj_router(x2d, o, w_o, g, rw_t, n_experts):
    t, d = x2d.shape
    tm = PROJ_TILE
    const = lambda i: (0, 0)
    kern = functools.partial(_oproj_router_kernel, tm=tm, d=d, n_experts=n_experts)
    return pl.pallas_call(
        kern,
        grid=(t // tm,),
        in_specs=[
            pl.BlockSpec((tm, d), lambda i: (i, 0)),
            pl.BlockSpec((tm, d), lambda i: (i, 0)),
            _resident((d, d)),
            pl.BlockSpec((1, d), const),
            pl.BlockSpec((SUBLANES, d), const),
        ],
        out_specs=[
            pl.BlockSpec((tm, d), lambda i: (i, 0)),
            pl.BlockSpec((tm * (d // LANES), LANES), lambda i: (i, 0)),
            pl.BlockSpec((SUBLANES, tm), lambda i: (0, i)),
            pl.BlockSpec((SUBLANES, tm), lambda i: (0, i)),
            pl.BlockSpec((SUBLANES, LANES), const),
        ],
        out_shape=[
            jax.ShapeDtypeStruct((t, d), F32),
            jax.ShapeDtypeStruct((t * (d // LANES), LANES), F32),
            jax.ShapeDtypeStruct((SUBLANES, t), F32),
            jax.ShapeDtypeStruct((SUBLANES, t), jnp.int32),
            jax.ShapeDtypeStruct((SUBLANES, LANES), jnp.int32),
        ],
        scratch_shapes=[pltpu.VMEM((SUBLANES, LANES), F32)],
        compiler_params=_params("arbitrary"),
        name="oproj_router",
    )(x2d, o, w_o, g, rw_t)


def _pad_bits(tile):
    return [1 << b for b in reversed(range(int(math.log2(tile))))]


def _dispatch_kernel(s0_ref, s1_ref, pad_start_ref, pad_len_ref, h_ref, hs_hbm, zeros, sem, zsem,
                     *, tb, nc, n_experts, tile):
    i = pl.program_id(0)
    base = i * tb

    def rows(ref, first, count):
        start = first * nc
        if not isinstance(start, int):
            start = pl.multiple_of(start, nc)
        return ref.at[pl.ds(start, count * nc), :]

    def row_copy(r, slot):
        return pltpu.make_async_copy(rows(h_ref, r, 1), rows(hs_hbm, slot, 1), sem)

    def issue(g, carry):
        for u in range(DMA_UNROLL):
            r = g * DMA_UNROLL + u
            row_copy(r, s0_ref[base + r]).start(priority=0)
            row_copy(r, s1_ref[base + r]).start(priority=1)
        return carry

    lax.fori_loop(0, tb // DMA_UNROLL, issue, 0)

    def pad_copies(do):
        for e in range(n_experts):
            n = pad_len_ref[e]
            pos = pad_start_ref[e]
            for bit in _pad_bits(tile):
                @pl.when((n & bit) != 0)
                def _(pos=pos, bit=bit):
                    do(pltpu.make_async_copy(rows(zeros, 0, bit), rows(hs_hbm, pos, bit), zsem))
                pos = pos + (n & bit)
        for k in range(n_experts):
            @pl.when(k < pad_len_ref[n_experts])
            def _(k=k):
                pos = pad_start_ref[n_experts] + k * tile
                do(pltpu.make_async_copy(zeros, rows(hs_hbm, pos, tile), zsem))

    @pl.when(i == 0)
    def _():
        zeros[...] = jnp.zeros_like(zeros)
        pad_copies(lambda cp: cp.start())
        pad_copies(lambda cp: cp.wait())

    def drain(g, carry):
        for u in range(2 * DMA_UNROLL):
            row_copy(0, 0).wait()
        return carry

    lax.fori_loop(0, tb // DMA_UNROLL, drain, 0)


def _dispatch(slot0, slot1, pad_start, pad_len, h8, nc, n_slots, n_experts):
    t = h8.shape[0] // nc
    tb = DISPATCH_BLOCK
    kern = functools.partial(_dispatch_kernel, tb=tb, nc=nc, n_experts=n_experts, tile=GROUP_TILE)
    grid_spec = pltpu.PrefetchScalarGridSpec(
        num_scalar_prefetch=4,
        grid=(t // tb,),
        in_specs=[pl.BlockSpec((tb * nc, LANES), lambda i, *_: (i, 0))],
        out_specs=pl.BlockSpec(memory_space=pl.ANY),
        scratch_shapes=[
            pltpu.VMEM((GROUP_TILE * nc, LANES), F32),
            pltpu.SemaphoreType.DMA,
            pltpu.SemaphoreType.DMA,
        ],
    )
    return pl.pallas_call(
        kern,
        grid_spec=grid_spec,
        out_shape=jax.ShapeDtypeStruct((n_slots * nc, LANES), F32),
        compiler_params=_params("arbitrary"),
        name="moe_dispatch",
    )(slot0, slot1, pad_start, pad_len, h8)


def _moe_group_kernel(te_ref, nt_ref, valid_ref, hs_ref, wg_ref, wu_ref, wd_ref, ys_ref, *, tm, d, f_exp):
    j = pl.program_id(0)
    nc = d // LANES
    valid = valid_ref[j]

    def expert_ffn(rows):
        h = jnp.concatenate([hs_ref[pl.ds(c, rows, stride=nc), :].astype(BF16) for c in range(nc)], axis=1)
        y = None
        for lo in range(0, f_exp, MOE_F_CHUNK):
            hi = min(lo + MOE_F_CHUNK, f_exp)
            gate = _dot(h, wg_ref[:, lo:hi])
            up = _dot(h, wu_ref[:, lo:hi])
            a = (gate * jax.nn.sigmoid(gate) * up).astype(BF16)
            p = _dot(a, wd_ref[lo:hi, :])
            y = p if y is None else y + p
        for c in range(nc):
            ys_ref[pl.ds(c, rows, stride=nc), :] = y[:, c * LANES:(c + 1) * LANES]

    half = tm // 2

    @pl.when(valid > half)
    def _():
        expert_ffn(tm)

    @pl.when(jnp.logical_and(valid > 0, valid <= half))
    def _():
        expert_ffn(half)
        ys_ref[half * nc:, :] = jnp.zeros(((tm - half) * nc, LANES), F32)

    @pl.when(valid == 0)
    def _():
        ys_ref[...] = jnp.zeros_like(ys_ref)


def _moe_group(tile_expert, n_tiles, tile_valid, hs2d, w_gu, w_down, max_tiles):
    n_e, d, f2 = w_gu.shape
    f_exp = f2 // 2
    tm = GROUP_TILE
    nc = d // LANES

    def row_map(j, te, nt, valid):
        return (jnp.minimum(j, nt[0] - 1), 0)

    kern = functools.partial(_moe_group_kernel, tm=tm, d=d, f_exp=f_exp)
    grid_spec = pltpu.PrefetchScalarGridSpec(
        num_scalar_prefetch=3,
        grid=(max_tiles,),
        in_specs=[
            pl.BlockSpec((tm * nc, LANES), row_map),
            pl.BlockSpec((None, d, f_exp), lambda j, te, nt, valid: (te[j], 0, 0)),
            pl.BlockSpec((None, d, f_exp), lambda j, te, nt, valid: (te[j], 0, 1)),
            pl.BlockSpec((None, f_exp, d), lambda j, te, nt, valid: (te[j], 0, 0)),
        ],
        out_specs=pl.BlockSpec((tm * nc, LANES), lambda j, te, nt, valid: (j, 0)),
    )
    return pl.pallas_call(
        kern,
        grid_spec=grid_spec,
        out_shape=jax.ShapeDtypeStruct(hs2d.shape, F32),
        compiler_params=pltpu.CompilerParams(dimension_semantics=("arbitrary",),
                                             vmem_limit_bytes=MOE_VMEM_LIMIT),
        name="moe_group",
    )(tile_expert, n_tiles, tile_valid, hs2d, w_gu, w_gu, w_down)


def _combine_kernel(s0_ref, s1_ref, x_ref, gate_ref, ys_hbm, o_ref, b00, b01, b10, b11, sems, *, tm, d):
    i = pl.program_id(0)
    n = pl.num_programs(0)
    nc = d // LANES
    bufs = ((b00, b01), (b10, b11))

    def row_copy(slot, par, k, r):
        start = r * nc
        if not isinstance(start, int):
            start = pl.multiple_of(start, nc)
        src = ys_hbm.at[pl.ds(pl.multiple_of(slot * nc, nc), nc), :]
        return pltpu.make_async_copy(src, bufs[par][k].at[pl.ds(start, nc), :], sems.at[par])

    def issue(block, par):
        base = block * tm

        def body(g, carry):
            for u in range(DMA_UNROLL):
                r = g * DMA_UNROLL + u
                row_copy(s0_ref[base + r], par, 0, r).start(priority=0)
                row_copy(s1_ref[base + r], par, 1, r).start(priority=1)
            return carry

        lax.fori_loop(0, tm // DMA_UNROLL, body, 0)

    def drain(par):
        def body(g, carry):
            for u in range(DMA_UNROLL):
                row_copy(0, par, 0, 0).wait()
                row_copy(0, par, 1, 0).wait()
            return carry

        lax.fori_loop(0, tm // DMA_UNROLL, body, 0)

    @pl.when(i == 0)
    def _():
        issue(0, 0)

    for par in range(2):
        @pl.when(i % 2 == par)
        def _(par=par):
            @pl.when(i + 1 < n)
            def _():
                issue(i + 1, 1 - par)

            drain(par)
            gates = jnp.transpose(gate_ref[...])
            g0 = gates[:, 0:1]
            g1 = gates[:, 1:2]
            for c in range(nc):
                y0 = bufs[par][0][pl.ds(c, tm, stride=nc), :]
                y1 = bufs[par][1][pl.ds(c, tm, stride=nc), :]
                cols = slice(c * LANES, (c + 1) * LANES)
                o_ref[:, cols] = x_ref[:, cols] + (g0 * y0 + g1 * y1)


def _combine(slot0, slot1, x3, gates, ys2d):
    t, d = x3.shape
    tm = COMBINE_BLOCK
    nc = d // LANES
    grid_spec = pltpu.PrefetchScalarGridSpec(
        num_scalar_prefetch=2,
        grid=(t // tm,),
        in_specs=[
            pl.BlockSpec((tm, d), lambda i, s0, s1: (i, 0)),
            pl.BlockSpec((SUBLANES, tm), lambda i, s0, s1: (0, i)),
            pl.BlockSpec(memory_space=pl.ANY),
        ],
        out_specs=pl.BlockSpec((tm, d), lambda i, s0, s1: (i, 0)),
        scratch_shapes=[
            pltpu.VMEM((tm * nc, LANES), F32),
            pltpu.VMEM((tm * nc, LANES), F32),
            pltpu.VMEM((tm * nc, LANES), F32),
            pltpu.VMEM((tm * nc, LANES), F32),
            pltpu.SemaphoreType.DMA((2,)),
        ],
    )
    return pl.pallas_call(
        functools.partial(_combine_kernel, tm=tm, d=d),
        grid_spec=grid_spec,
        out_shape=jax.ShapeDtypeStruct((t, d), F32),
        compiler_params=_params("arbitrary"),
        name="moe_combine",
    )(slot0, slot1, x3, gates, ys2d)


def _routing_tables(route, counts, n_experts, max_tiles):
    tile = GROUP_TILE
    padded = ((counts + tile - 1) // tile) * tile
    ends = jnp.cumsum(padded)
    offs = ends - padded
    slots = []
    for k in range(TOP_K):
        expert, rank = route[k], route[TOP_K + k]
        base = jnp.zeros_like(expert)
        for e in range(n_experts):
            base = jnp.where(expert == e, offs[e], base)
        slots.append(base + rank)
    tile_ends = ends // tile
    n_tiles = tile_ends[-1]
    j = jnp.arange(max_tiles, dtype=jnp.int32)
    tile_expert = jnp.sum((jnp.minimum(j, n_tiles - 1)[:, None] >= tile_ends[None, :]).astype(jnp.int32), axis=-1)
    row_end = jnp.sum(jnp.where(tile_expert[:, None] == jnp.arange(n_experts, dtype=jnp.int32)[None, :],
                                (offs + counts)[None, :], 0), axis=-1)
    tile_valid = jnp.where(j < n_tiles, jnp.clip(row_end - j * tile, 0, tile), 0)
    pad_start = jnp.concatenate([offs + counts, ends[-1:]]).astype(jnp.int32)
    pad_len = jnp.concatenate([padded - counts, max_tiles - n_tiles.reshape(1)]).astype(jnp.int32)
    return (slots[0], slots[1], pad_start, pad_len,
            tile_expert.astype(jnp.int32), n_tiles.reshape(1).astype(jnp.int32), tile_valid.astype(jnp.int32))


def _lambda_init(layer_idx_1based):
    return 0.8 - 0.6 * math.exp(-0.3 * (layer_idx_1based - 1))


def kernel(x, ln_mix, ln_ffn, conv_w_in, conv_w, conv_w_out, ln_kv, w_kv, k_norm, attn_w_q, q_norm, lam_params,
           sub_norm, attn_w_o, ffn_w_gu, ffn_w_down, router_w, moe_w_gu, moe_w_down):
    batch, seq, d = x.shape
    t = batch * seq
    n_experts = router_w.shape[-1]
    assert ln_mix.shape[0] == 2 and conv_w_in.shape[0] == 1 and attn_w_q.shape[0] == 1
    assert seq % ROW_TILE == 0 and seq % ATTN_TILE == 0 and d % LANES == 0
    nc = d // LANES
    x2d = x.reshape(t, d)

    x1 = _mixer_a(x2d, ln_mix[0:1], conv_w_in[0], conv_w[0], conv_w_out[0], seq)
    x2 = _ffn_dense(x1, ln_ffn[0:1], ffn_w_gu[0], ffn_w_down[0])

    assert w_kv.shape == (d, 2 * d)
    q, kt, v = _qkv(x2, ln_mix[1:2], ln_kv.reshape(1, d), attn_w_q[0], w_kv, k_norm.reshape(HEAD_DIM, 1))

    o, moe_gu_bf, moe_down_bf = _attention(q, kt, v, jnp.tile(q_norm[0:1], (1, 2)),
                                           jnp.tile(k_norm.reshape(1, HEAD_DIM), (1, 2)), lam_params[0],
                                           sub_norm[0:1], moe_w_gu[0], moe_w_down[0], batch, seq, _lambda_init(2))

    assert n_experts <= SUBLANES
    rw_t = jnp.pad(router_w[0].T, ((0, SUBLANES - n_experts), (0, 0)))
    x3, h8, gates, route, counts = _oproj_router(x2, o, attn_w_o[0], ln_ffn[1:2], rw_t, n_experts)
    max_tiles = (TOP_K * t) // GROUP_TILE + n_experts
    n_slots = max_tiles * GROUP_TILE
    slot0, slot1, pad_start, pad_len, tile_expert, n_tiles, tile_valid = _routing_tables(
        route, counts[:n_experts, 0], n_experts, max_tiles)
    hs = _dispatch(slot0, slot1, pad_start, pad_len, h8, nc, n_slots, n_experts)
    ys = _moe_group(tile_expert, n_tiles, tile_valid, hs, moe_gu_bf, moe_down_bf, max_tiles)
    out = _combine(slot0, slot1, x3, gates, ys)
    return out.reshape(batch, seq, d)
```
